```python
import jax
import jax.numpy as jnp
from jax import lax
import numpy as np

D_MODEL = 2048
BATCH = 16
SEQ = 2048
DEPTH = 4

F32 = jnp.float32
CHUNK = 64
N_META = 16
N_MIXERS = 3
CONV_WIDTH = 31
MLA_HEADS = D_MODEL // 128
MLA_Q_LORA = D_MODEL // 4
MLA_KV_LORA = D_MODEL // 8
MLA_NOPE = 128
MLA_ROPE = 64
MLA_V = 128
ROPE_THETA = 10000.0
Q_BLOCK = 128
HGRN_HEAD_DIM = 128
HGRN_HEADS = D_MODEL // HGRN_HEAD_DIM
HGRN_BLOCK = 16
N_EXPERTS = 32
N_GROUPS = 4
EXPERTS_PER_GROUP = N_EXPERTS // N_GROUPS
TOP_K = 2
EXPERT_FF = D_MODEL // 4
MOE_BLOCK = 256
DEEPNORM_ALPHA = (2 * DEPTH) ** 0.25
DEEPNORM_BETA = (8 * DEPTH) ** -0.25
N_CONV_LAYERS = len(range(0, DEPTH, N_MIXERS))
N_MLA_LAYERS = len(range(1, DEPTH, N_MIXERS))
N_HGRN_LAYERS = len(range(2, DEPTH, N_MIXERS))
LN_EPS = 1e-5
RMS_EPS = 1e-6

kernel_name = 'hybrid_chunk_causal_conv_mla_hgrn2_moe'


def layer_norm(x, g, b):
    xf = x.astype(F32)
    mu = xf.mean(-1, keepdims=True)
    var = jnp.square(xf - mu).mean(-1, keepdims=True)
    return ((xf - mu) * lax.rsqrt(var + LN_EPS) * g.astype(F32) + b.astype(F32)).astype(x.dtype)


def rms_norm(x, g):
    xf = x.astype(F32)
    y = xf * lax.rsqrt(jnp.mean(jnp.square(xf), -1, keepdims=True) + RMS_EPS)
    return (y * g.astype(F32)).astype(x.dtype)


def chunk_index(pos):
    return jnp.where(pos < N_META, 0, 1 + (pos - N_META) // CHUNK)


def rope_tables(length):
    inv = ROPE_THETA ** (-jnp.arange(0, MLA_ROPE, 2, dtype=F32) / MLA_ROPE)
    ang = jnp.arange(length, dtype=F32)[:, None] * inv[None, :]
    return jnp.cos(ang), jnp.sin(ang)


def apply_rope(x, cos, sin):
    xf = x.astype(F32)
    x1, x2 = jnp.split(xf, 2, axis=-1)
    return jnp.concatenate([x1 * cos - x2 * sin, x2 * cos + x1 * sin], axis=-1).astype(x.dtype)


def conformer_conv(x, pw1_w, pw1_b, dw_w, dw_b, ln_g, ln_b, pw2_w, pw2_b):
    a, gate = jnp.split(x @ pw1_w + pw1_b, 2, axis=-1)
    u = a * jax.nn.sigmoid(gate)
    u = lax.conv_general_dilated(
        u, dw_w[:, None, :], window_strides=(1,), padding=[(CONV_WIDTH - 1, 0)],
        dimension_numbers=('NWC', 'WIO', 'NWC'), feature_group_count=u.shape[-1]) + dw_b
    u = jax.nn.silu(layer_norm(u, ln_g, ln_b))
    return u @ pw2_w + pw2_b


def mla_attention(x, key_chunk, cos, sin, wdq, q_norm_g, wuq, wdkv, kv_norm_g, wukv, wo):
    B, L, _ = x.shape
    cq = rms_norm(x @ wdq, q_norm_g)
    q = (cq @ wuq).reshape(B, L, MLA_HEADS, MLA_NOPE + MLA_ROPE)
    q_nope = q[..., :MLA_NOPE]
    q_rope = apply_rope(q[..., MLA_NOPE:], cos[:, None, :], sin[:, None, :])
    ckv = x @ wdkv
    c_kv = rms_norm(ckv[..., :MLA_KV_LORA], kv_norm_g)
    k_rope = apply_rope(ckv[..., MLA_KV_LORA:], cos, sin)
    kv = (c_kv @ wukv).reshape(B, L, MLA_HEADS, MLA_NOPE + MLA_V)
    k_nope = kv[..., :MLA_NOPE]
    v = kv[..., MLA_NOPE:]
    n_blk = -(-L // Q_BLOCK)
    lq = n_blk * Q_BLOCK

    def blocks(t):
        t = jnp.pad(t, ((0, 0), (0, lq - L), (0, 0), (0, 0)))
        return t.reshape(B, n_blk, Q_BLOCK, t.shape[2], t.shape[3]).swapaxes(0, 1)

    q_chunk = chunk_index(jnp.arange(lq)).reshape(n_blk, Q_BLOCK)
    scale = (MLA_NOPE + MLA_ROPE) ** -0.5

    def attend(args):
        qn, qr, qc = args
        s = jnp.einsum('bqhd,bkhd->bhqk', qn, k_nope) + jnp.einsum('bqhr,bkr->bhqk', qr, k_rope)
        s = s.astype(F32) * scale
        visible = key_chunk[None, :] <= qc[:, None]
        p = jax.nn.softmax(jnp.where(visible, s, -jnp.inf), axis=-1).astype(v.dtype)
        return jnp.einsum('bhqk,bkhd->bqhd', p, v)

    o = lax.map(attend, (blocks(q_nope), blocks(q_rope), q_chunk))
    o = o.swapaxes(0, 1).reshape(B, lq, MLA_HEADS * MLA_V)[:, :L]
    return o @ wo


def hgrn2(x, w_in, lb, norm_g, wo):
    B, L, D = x.shape
    q, fz, i, g = jnp.split(x @ w_in, 4, axis=-1)
    fz = fz.astype(F32)
    log_f = jnp.logaddexp(jnp.log(lb), jnp.log1p(-lb) + jax.nn.log_sigmoid(fz))
    k = (1.0 - lb) * jax.nn.sigmoid(-fz)
    n = L // HGRN_BLOCK
    c = HGRN_BLOCK

    def to_blocks(t):
        return t.astype(F32).reshape(B, n, c, HGRN_HEADS, HGRN_HEAD_DIM).transpose(1, 0, 3, 2, 4)

    qb, kb, ib = to_blocks(q), to_blocks(k), to_blocks(i)
    bb = jnp.cumsum(to_blocks(log_f), axis=3)
    tri = jnp.arange(c)[:, None] >= jnp.arange(c)[None, :]

    def step(S, inp):
        qc, kc, ic, bc = inp
        o_inter = jnp.einsum('bhtd,bhde->bhte', qc * jnp.exp(bc), S)
        diff = bc[:, :, :, None, :] - bc[:, :, None, :, :]
        decay = jnp.exp(jnp.where(tri[:, :, None], diff, -jnp.inf))
        scores = jnp.einsum('bhtd,bhsd,bhtsd->bhts', qc, kc, decay)
        o = o_inter + jnp.einsum('bhts,bhse->bhte', scores, ic)
        b_last = bc[:, :, -1:, :]
        S = jnp.exp(b_last[:, :, 0, :, None]) * S + jnp.einsum('bhsd,bhse->bhde', kc * jnp.exp(b_last - bc), ic)
        return S, o

    s0 = jnp.zeros((B, HGRN_HEADS, HGRN_HEAD_DIM, HGRN_HEAD_DIM), F32)
    _, o = lax.scan(step, s0, (qb, kb, ib, bb))
    o = o.transpose(1, 0, 3, 2, 4).reshape(B, L, HGRN_HEADS, HGRN_HEAD_DIM)
    o = rms_norm(o, norm_g) * jax.nn.silu(g.astype(F32).reshape(B, L, HGRN_HEADS, HGRN_HEAD_DIM))
    return o.reshape(B, L, D).astype(x.dtype) @ wo


def moe_ffn(h, router_w, router_b, w_gate, w_up, w_down):
    T, D = h.shape
    scores = jax.nn.sigmoid(h.astype(F32) @ router_w.astype(F32))
    sel = (scores + router_b.astype(F32)).reshape(T, N_GROUPS, EXPERTS_PER_GROUP)
    group_score = lax.top_k(sel, 2)[0].sum(-1)
    g_top = jnp.argmax(group_score, axis=-1).astype(jnp.int32)
    in_group = sel[jnp.arange(T), g_top]
    _, local = lax.top_k(in_group, TOP_K)
    expert_idx = (g_top[:, None] * EXPERTS_PER_GROUP + local).astype(jnp.int32)
    gate = jnp.take_along_axis(scores, expert_idx, axis=1)
    gate = gate / gate.sum(-1, keepdims=True)
    n_assign = T * TOP_K
    flat_e = expert_idx.reshape(n_assign)
    flat_tok = jnp.repeat(jnp.arange(T, dtype=jnp.int32), TOP_K)
    order = jnp.argsort(flat_e)
    se, stok, sgate = flat_e[order], flat_tok[order], gate.reshape(n_assign)[order]
    counts = jnp.bincount(flat_e, length=N_EXPERTS)
    starts = jnp.cumsum(counts) - counts
    padded = (counts + MOE_BLOCK - 1) // MOE_BLOCK * MOE_BLOCK
    pends = jnp.cumsum(padded)
    pstarts = pends - padded
    dest = pstarts[se] + jnp.arange(n_assign, dtype=jnp.int32) - starts[se]
    n_blocks = -(-n_assign // MOE_BLOCK) + N_EXPERTS
    slot_tok = jnp.full((n_blocks * MOE_BLOCK,), T, jnp.int32).at[dest].set(stok)
    h_pad = jnp.concatenate([h, jnp.zeros((1, D), h.dtype)], axis=0)
    xb = h_pad[slot_tok].reshape(n_blocks, MOE_BLOCK, D)
    block_start = jnp.arange(n_blocks, dtype=jnp.int32) * MOE_BLOCK
    block_expert = jnp.minimum(jnp.searchsorted(pends, block_start, side='right'), N_EXPERTS - 1)

    def expert_block(args):
        xblk, e = args
        return (jax.nn.silu(xblk @ w_gate[e]) * (xblk @ w_up[e])) @ w_down[e]

    yb = lax.map(expert_block, (xb, block_expert)).reshape(n_blocks * MOE_BLOCK, D)
    return jnp.zeros((T, D), h.dtype).at[stok].add(yb[dest] * sgate[:, None].astype(h.dtype))


def setup_inputs(seed: int = 0) -> dict:
    key = jax.random.key(seed)
    k = jax.random.split(key, 32)

    def dense(kk, shape, fan_in, scale=1.0):
        return jax.random.normal(kk, shape, F32) * (scale * fan_in ** -0.5)

    def gain(kk, shape):
        return 1.0 + 0.02 * jax.random.normal(kk, shape, F32)

    def small(kk, shape, s=0.02):
        return s * jax.random.normal(kk, shape, F32)

    qk_dim = MLA_NOPE + MLA_ROPE
    ukv_scale = jnp.tile(jnp.concatenate([jnp.ones((MLA_NOPE,), F32), jnp.full((MLA_V,), DEEPNORM_BETA, F32)]), MLA_HEADS)
    hin_scale = jnp.concatenate([jnp.ones((2 * D_MODEL,), F32), jnp.full((D_MODEL,), DEEPNORM_BETA, F32), jnp.ones((D_MODEL,), F32)])
    return {
        'x': jax.random.normal(k[0], (BATCH, SEQ, D_MODEL), F32),
        'meta_tokens': jax.random.normal(k[1], (N_META, D_MODEL), F32),
        'ln_mix_g': gain(k[2], (DEPTH, D_MODEL)),
        'ln_mix_b': small(k[3], (DEPTH, D_MODEL)),
        'ln_ffn_g': gain(k[4], (DEPTH, D_MODEL)),
        'ln_ffn_b': small(k[5], (DEPTH, D_MODEL)),
        'conv_pw1_w': dense(k[6], (N_CONV_LAYERS, D_MODEL, 2 * D_MODEL), D_MODEL),
        'conv_pw1_b': small(k[7], (N_CONV_LAYERS, 2 * D_MODEL)),
        'conv_dw_w': dense(k[8], (N_CONV_LAYERS, CONV_WIDTH, D_MODEL), CONV_WIDTH),
        'conv_dw_b': small(k[9], (N_CONV_LAYERS, D_MODEL)),
        'conv_ln_g': gain(k[10], (N_CONV_LAYERS, D_MODEL)),
        'conv_ln_b': small(k[11], (N_CONV_LAYERS, D_MODEL)),
        'conv_pw2_w': dense(k[12], (N_CONV_LAYERS, D_MODEL, D_MODEL), D_MODEL, DEEPNORM_BETA),
        'conv_pw2_b': small(k[13], (N_CONV_LAYERS, D_MODEL)),
        'mla_wdq': dense(k[14], (N_MLA_LAYERS, D_MODEL, MLA_Q_LORA), D_MODEL),
        'mla_q_norm_g': gain(k[15], (N_MLA_LAYERS, MLA_Q_LORA)),
        'mla_wuq': dense(k[16], (N_MLA_LAYERS, MLA_Q_LORA, MLA_HEADS * qk_dim), MLA_Q_LORA),
        'mla_wdkv': dense(k[17], (N_MLA_LAYERS, D_MODEL, MLA_KV_LORA + MLA_ROPE), D_MODEL),
        'mla_kv_norm_g': gain(k[18], (N_MLA_LAYERS, MLA_KV_LORA)),
        'mla_wukv': dense(k[19], (N_MLA_LAYERS, MLA_KV_LORA, MLA_HEADS * (MLA_NOPE + MLA_V)), MLA_KV_LORA) * ukv_scale,
        'mla_wo': dense(k[20], (N_MLA_LAYERS, MLA_HEADS * MLA_V, D_MODEL), MLA_HEADS * MLA_V, DEEPNORM_BETA),
        'hgrn_w_in': dense(k[21], (N_HGRN_LAYERS, D_MODEL, 4 * D_MODEL), D_MODEL) * hin_scale,
        'hgrn_lb_logits': jax.random.normal(k[22], (DEPTH, D_MODEL), F32),
        'hgrn_norm_g': gain(k[23], (N_HGRN_LAYERS, HGRN_HEAD_DIM)),
        'hgrn_wo': dense(k[24], (N_HGRN_LAYERS, D_MODEL, D_MODEL), D_MODEL, DEEPNORM_BETA),
        'router_w': dense(k[25], (D_MODEL, N_EXPERTS), D_MODEL),
        'router_b': small(k[26], (N_EXPERTS,), 0.01),
        'moe_w_gate': dense(k[27], (DEPTH, N_EXPERTS, D_MODEL, EXPERT_FF), D_MODEL),
        'moe_w_up': dense(k[28], (DEPTH, N_EXPERTS, D_MODEL, EXPERT_FF), D_MODEL, DEEPNORM_BETA),
        'moe_w_down': dense(k[29], (DEPTH, N_EXPERTS, EXPERT_FF, D_MODEL), EXPERT_FF, DEEPNORM_BETA),
    }


def reference(x, meta_tokens, ln_mix_g, ln_mix_b, ln_ffn_g, ln_ffn_b,
              conv_pw1_w, conv_pw1_b, conv_dw_w, conv_dw_b, conv_ln_g, conv_ln_b, conv_pw2_w, conv_pw2_b,
              mla_wdq, mla_q_norm_g, mla_wuq, mla_wdkv, mla_kv_norm_g, mla_wukv, mla_wo,
              hgrn_w_in, hgrn_lb_logits, hgrn_norm_g, hgrn_wo,
              router_w, router_b, moe_w_gate, moe_w_up, moe_w_down):
    B = x.shape[0]
    meta = jnp.broadcast_to(meta_tokens[None].astype(x.dtype), (B, N_META, D_MODEL))
    h = jnp.concatenate([meta, x], axis=1)
    L = h.shape[1]
    key_chunk = chunk_index(jnp.arange(L))
    cos, sin = rope_tables(L)
    p_lb = jax.nn.softmax(hgrn_lb_logits.astype(F32), axis=0)
    lower_bounds = jnp.cumsum(p_lb, axis=0) - p_lb[0]
    for i in range(DEPTH):
        j = i // N_MIXERS
        kind = i % N_MIXERS
        if kind == 0:
            mix = conformer_conv(h, conv_pw1_w[j], conv_pw1_b[j], conv_dw_w[j], conv_dw_b[j],
                                 conv_ln_g[j], conv_ln_b[j], conv_pw2_w[j], conv_pw2_b[j])
        elif kind == 1:
            mix = mla_attention(h, key_chunk, cos, sin, mla_wdq[j], mla_q_norm_g[j], mla_wuq[j],
                                mla_wdkv[j], mla_kv_norm_g[j], mla_wukv[j], mla_wo[j])
        else:
            mix = hgrn2(h, hgrn_w_in[j], lower_bounds[i], hgrn_norm_g[j], hgrn_wo[j])
        h = layer_norm(DEEPNORM_ALPHA * h + mix, ln_mix_g[i], ln_mix_b[i])
        ffn = moe_ffn(h.reshape(B * L, D_MODEL), router_w, router_b,
                      moe_w_gate[i], moe_w_up[i], moe_w_down[i]).reshape(B, L, D_MODEL)
        h = layer_norm(DEEPNORM_ALPHA * h + ffn, ln_ffn_g[i], ln_ffn_b[i])
    return h[:, N_META:]
```

```python
import functools

import jax
import jax.numpy as jnp
from jax import lax
from jax.experimental import pallas as pl
from jax.experimental.pallas import tpu as pltpu

F32 = jnp.float32
BF16 = jnp.bfloat16
I32 = jnp.int32

LANES = 128
CHUNK = 64
N_MIXERS = 3
CONV_WIDTH = 31
CONV_HALO = 32
HEAD_DIM = 128
MLA_ROPE = 64
ROPE_THETA = 10000.0
HGRN_BLOCK = 16
N_EXPERTS = 32
N_GROUPS = 4
EXPERTS_PER_GROUP = N_EXPERTS // N_GROUPS
MOE_BLOCK = 256
LN_EPS = 1e-5
RMS_EPS = 1e-6
VMEM_LIMIT = 52 * 1024 * 1024


def _pick(n, cands):
    for c in cands:
        if n % c == 0:
            return c
    raise ValueError(f"no tile for {n} in {cands}")


_ROW_TILES = (768, 512, 384, 256, 192, 176, 128, 96, 64, 48, 32, 16)


def _cparams(sem, vmem=VMEM_LIMIT):
    return pltpu.CompilerParams(dimension_semantics=sem, vmem_limit_bytes=vmem)


def _layer_norm(x, g, b):
    mu = jnp.mean(x, axis=-1, keepdims=True)
    xc = x - mu
    var = jnp.mean(xc * xc, axis=-1, keepdims=True)
    return xc * lax.rsqrt(var + LN_EPS) * g + b


def _rms_norm(x, g):
    return x * lax.rsqrt(jnp.mean(x * x, axis=-1, keepdims=True) + RMS_EPS) * g


def _sigmoid(x):
    return 1.0 / (1.0 + jnp.exp(-x))


def _rows_call(body, n_rows, tm, row_ins, full_ins, outs, name):
    grid = (n_rows // tm,)
    in_specs = [pl.BlockSpec((tm, a.shape[1]), lambda i: (i, 0)) for a in row_ins]
    in_specs += [pl.BlockSpec(a.shape, lambda i, nd=a.ndim: (0,) * nd) for a in full_ins]
    out_specs = [pl.BlockSpec((tm, n), lambda i: (i, 0)) for n, _ in outs]
    out_shape = [jax.ShapeDtypeStruct((n_rows, n), dt) for n, dt in outs]
    return pl.pallas_call(
        body, grid=grid, in_specs=in_specs, out_specs=out_specs, out_shape=out_shape,
        compiler_params=_cparams(("parallel",)), name=name)(*row_ins, *full_ins)


def _mm_res_ln_body(alpha, x_ref, h_ref, w_ref, bias_ref, g_ref, b_ref, ho_ref, hb_ref):
    acc = jnp.dot(x_ref[...], w_ref[...], preferred_element_type=F32)
    y = alpha * h_ref[...] + (acc + bias_ref[...])
    o = _layer_norm(y, g_ref[...], b_ref[...])
    ho_ref[...] = o
    hb_ref[...] = o.astype(BF16)


def mm_res_ln(x_bf, h, w_bf, bias, g, b, alpha):
    t, d = h.shape
    tm = _pick(t, (256, 192, 176, 128, 96, 64, 48, 32, 16))
    return _rows_call(functools.partial(_mm_res_ln_body, alpha), t, tm, [x_bf, h],
                      [w_bf, bias.reshape(1, d), g.reshape(1, d), b.reshape(1, d)],
                      [(d, F32), (d, BF16)], "mm_res_ln")


def _glu_body(x_ref, wa_ref, wg_ref, ba_ref, bg_ref, u_ref):
    x = x_ref[...]
    a = jnp.dot(x, wa_ref[...], preferred_element_type=F32) + ba_ref[...]
    g = jnp.dot(x, wg_ref[...], preferred_element_type=F32) + bg_ref[...]
    u_ref[...] = a * _sigmoid(g)


def mm_glu(x_bf, w_bf, bias):
    t, k = x_bf.shape
    d = w_bf.shape[1] // 2
    tm = _pick(t, _ROW_TILES)
    tn = min(d, 1024)
    nj = d // tn
    bias2 = bias.reshape(1, 2 * d)
    return pl.pallas_call(
        _glu_body, grid=(nj, t // tm),
        in_specs=[pl.BlockSpec((tm, k), lambda j, i: (i, 0)),
                  pl.BlockSpec((k, tn), lambda j, i: (0, j)),
                  pl.BlockSpec((k, tn), lambda j, i: (0, j + nj)),
                  pl.BlockSpec((1, tn), lambda j, i: (0, j)),
                  pl.BlockSpec((1, tn), lambda j, i: (0, j + nj))],
        out_specs=pl.BlockSpec((tm, tn), lambda j, i: (i, j)),
        out_shape=jax.ShapeDtypeStruct((t, d), F32),
        compiler_params=_cparams(("parallel", "parallel")), name="mm_glu")(x_bf, w_bf, w_bf, bias2, bias2)


def _mm_plain_body(x_ref, w_ref, o_ref):
    o_ref[...] = jnp.dot(x_ref[...], w_ref[...], preferred_element_type=F32).astype(o_ref.dtype)


def mm_plain(x_bf, w_bf, out_dtype=F32):
    t, k = x_bf.shape
    n = w_bf.shape[1]
    tm = _pick(t, _ROW_TILES)
    tn = min(n, 1024)
    return pl.pallas_call(
        _mm_plain_body, grid=(n // tn, t // tm),
        in_specs=[pl.BlockSpec((tm, k), lambda j, i: (i, 0)),
                  pl.BlockSpec((k, tn), lambda j, i: (0, j))],
        out_specs=pl.BlockSpec((tm, tn), lambda j, i: (i, j)),
        out_shape=jax.ShapeDtypeStruct((t, n), out_dtype),
        compiler_params=_cparams(("parallel", "parallel")), name="mm_plain")(x_bf, w_bf)


def _conv_rows(ext_ref, w_ref, y_ref, rows, rc, lc):
    d = y_ref.shape[1]
    shift = CONV_HALO - (CONV_WIDTH - 1)

    def lane_body(c, carry):
        l0 = pl.multiple_of(c * lc, lc)
        wv = w_ref[:, pl.ds(l0, lc)]
        for r in range(rows // rc):
            acc = None
            for k in range(CONV_WIDTH):
                term = ext_ref[pl.ds(r * rc + shift + k, rc), pl.ds(l0, lc)] * wv[k:k + 1, :]
                acc = term if acc is None else acc + term
            y_ref[pl.ds(r * rc, rc), pl.ds(l0, lc)] = acc
        return carry

    lax.fori_loop(0, d // lc, lane_body, 0)


def _conv_epilogue(y, dwb, g, b):
    z = _layer_norm(y + dwb, g, b)
    return (z * _sigmoid(z)).astype(BF16)


def _conv_main_body(tr, cur_ref, prev_ref, meta_ref, w_ref, dwb_ref, g_ref, b_ref, alias_ref, o_ref,
                    ext_ref, y_ref):
    del alias_ref
    s = pl.program_id(1)
    nm = meta_ref.shape[0]

    @pl.when(s == 0)
    def _():
        ext_ref[0:CONV_HALO - nm, :] = jnp.zeros((CONV_HALO - nm, ext_ref.shape[1]), F32)
        ext_ref[CONV_HALO - nm:CONV_HALO, :] = meta_ref[...]

    @pl.when(s > 0)
    def _():
        ext_ref[0:CONV_HALO, :] = prev_ref[...]

    ext_ref[CONV_HALO:CONV_HALO + tr, :] = cur_ref[...]
    _conv_rows(ext_ref, w_ref, y_ref, tr, 64, 256)
    o_ref[...] = _conv_epilogue(y_ref[...], dwb_ref[...], g_ref[...], b_ref[...])


def _conv_meta_body(meta_ref, w_ref, dwb_ref, g_ref, b_ref, o_ref, ext_ref, y_ref):
    nm = meta_ref.shape[0]
    ext_ref[0:CONV_HALO, :] = jnp.zeros((CONV_HALO, ext_ref.shape[1]), F32)
    ext_ref[CONV_HALO:CONV_HALO + nm, :] = meta_ref[...]
    _conv_rows(ext_ref, w_ref, y_ref, nm, nm, 256)
    o_ref[...] = _conv_epilogue(y_ref[...], dwb_ref[...], g_ref[...], b_ref[...])


def conv_ln_swish(u, dw_w, dw_b, ln_g, ln_b, nb, s_len, nm):
    t, d = u.shape
    tr_rows = nb * s_len
    tr = _pick(s_len, (512, 256))
    w_pad = jnp.concatenate([dw_w, jnp.zeros((CONV_HALO - CONV_WIDTH, d), F32)], axis=0)
    vecs = [dw_b.reshape(1, d), ln_g.reshape(1, d), ln_b.reshape(1, d)]
    meta_blk = tr_rows // nm
    out_meta = pl.pallas_call(
        _conv_meta_body, grid=(nb,),
        in_specs=[pl.BlockSpec((nm, d), lambda b: (meta_blk + b, 0)),
                  pl.BlockSpec((CONV_HALO, d), lambda b: (0, 0))]
        + [pl.BlockSpec((1, d), lambda b: (0, 0))] * 3,
        out_specs=pl.BlockSpec((nm, d), lambda b: (meta_blk + b, 0)),
        out_shape=jax.ShapeDtypeStruct((t, d), BF16),
        scratch_shapes=[pltpu.VMEM((CONV_HALO + nm, d), F32), pltpu.VMEM((nm, d), F32)],
        compiler_params=_cparams(("parallel",)), name="conv_meta")(u, w_pad, *vecs)
    nst = s_len // tr
    halo_per_tile = tr // CONV_HALO
    return pl.pallas_call(
        functools.partial(_conv_main_body, tr), grid=(nb, nst),
        in_specs=[pl.BlockSpec((tr, d), lambda b, s: (b * nst + s, 0)),
                  pl.BlockSpec((CONV_HALO, d),
                               lambda b, s: (jnp.maximum((b * nst + s) * halo_per_tile - 1, 0), 0)),
                  pl.BlockSpec((nm, d), lambda b, s: (meta_blk + b, 0)),
                  pl.BlockSpec((CONV_HALO, d), lambda b, s: (0, 0))]
        + [pl.BlockSpec((1, d), lambda b, s: (0, 0))] * 3
        + [pl.BlockSpec(memory_space=pl.ANY)],
        out_specs=pl.BlockSpec((tr, d), lambda b, s: (b * nst + s, 0)),
        out_shape=jax.ShapeDtypeStruct((t, d), BF16),
        scratch_shapes=[pltpu.VMEM((CONV_HALO + tr, d), F32), pltpu.VMEM((tr, d), F32)],
        input_output_aliases={7: 0},
        compiler_params=_cparams(("parallel", "arbitrary")), name="conv_main")(u, u, u, w_pad, *vecs, out_meta)


def _mla_proj_body(ql, kvl, hd, scale, x_ref, cos_ref, sin_ref, w1_ref, qg_ref, kvg_ref, wq_ref, wkv_ref,
                   qn_ref, qr_ref, kn_ref, v_ref, kr_ref):
    a = jnp.dot(x_ref[...], w1_ref[...], preferred_element_type=F32)
    cos = cos_ref[...]
    sin = sin_ref[...]
    cq = _rms_norm(a[:, :ql], qg_ref[...]).astype(BF16)
    ckv = _rms_norm(a[:, ql:ql + kvl], kvg_ref[...]).astype(BF16)
    r0 = ql + kvl
    kr_ref[...] = (a[:, r0:r0 + LANES] * cos + a[:, r0 + LANES:r0 + 2 * LANES] * sin).astype(BF16)
    qa = jnp.dot(cq, wq_ref[...], preferred_element_type=F32)
    nh = hd // LANES
    cos_t = jnp.tile(cos, (1, nh))
    sin_t = jnp.tile(sin, (1, nh))
    qn_ref[...] = (qa[:, :hd] * scale).astype(BF16)
    qr_ref[...] = ((qa[:, hd:2 * hd] * cos_t + qa[:, 2 * hd:] * sin_t) * scale).astype(BF16)
    kv = jnp.dot(ckv, wkv_ref[...], preferred_element_type=F32)
    kn_ref[...] = kv[:, :hd].astype(BF16)
    v_ref[...] = kv[:, hd:].astype(BF16)


def _attn_body(tq, qn_ref, qr_ref, kn_ref, kr_ref, v_ref, knm_ref, krm_ref, vm_ref, alias_ref, o_ref):
    del alias_ref
    i = pl.program_id(2)
    nt = (((1,), (1,)), ((), ()))
    q = jnp.concatenate([qn_ref[...], qr_ref[...]], axis=-1)
    km = jnp.concatenate([knm_ref[...], krm_ref[...]], axis=-1)
    s = lax.dot_general(q, km, nt, preferred_element_type=F32)
    m = jnp.max(s, axis=-1, keepdims=True)
    p = jnp.exp(s - m)
    l = jnp.sum(p, axis=-1, keepdims=True)
    acc = jnp.dot(p.astype(BF16), vm_ref[...], preferred_element_type=F32)
    row_c = lax.broadcasted_iota(I32, (tq, tq), 0) // CHUNK
    col_c = lax.broadcasted_iota(I32, (tq, tq), 1) // CHUNK
    visible = col_c <= row_c

    def step(j, carry, masked):
        m, l, acc = carry
        off = pl.multiple_of(j * tq, tq)
        k = jnp.concatenate([kn_ref[pl.ds(off, tq), :], kr_ref[pl.ds(off, tq), :]], axis=-1)
        s = lax.dot_general(q, k, nt, preferred_element_type=F32)
        if masked:
            s = jnp.where(visible, s, -jnp.inf)
        m_new = jnp.maximum(m, jnp.max(s, axis=-1, keepdims=True))
        a = jnp.exp(m - m_new)
        p = jnp.exp(s - m_new)
        l = a * l + jnp.sum(p, axis=-1, keepdims=True)
        acc = a * acc + jnp.dot(p.astype(BF16), v_ref[pl.ds(off, tq), :], preferred_element_type=F32)
        return m_new, l, acc

    carry = lax.fori_loop(0, i, lambda j, c: step(j, c, False), (m, l, acc))
    m, l, acc = step(i, carry, True)
    o_ref[...] = (acc / l).astype(BF16)


def _attn_meta_body(qn_ref, qr_ref, kn_ref, kr_ref, v_ref, o_ref):
    nt = (((1,), (1,)), ((), ()))
    q = jnp.concatenate([qn_ref[...], qr_ref[...]], axis=-1)
    k = jnp.concatenate([kn_ref[...], kr_ref[...]], axis=-1)
    s = lax.dot_general(q, k, nt, preferred_element_type=F32)
    p = jnp.exp(s - jnp.max(s, axis=-1, keepdims=True))
    l = jnp.sum(p, axis=-1, keepdims=True)
    o_ref[...] = (jnp.dot(p.astype(BF16), v_ref[...], preferred_element_type=F32) / l).astype(BF16)


def _rope_rows(n_pos):
    inv = ROPE_THETA ** (-jnp.arange(0, MLA_ROPE, 2, dtype=F32) / MLA_ROPE)
    ang = jnp.arange(n_pos, dtype=F32)[:, None] * inv[None, :]
    pad = jnp.zeros((n_pos, LANES - MLA_ROPE), F32)
    cos = jnp.concatenate([jnp.cos(ang), jnp.cos(ang), pad], axis=1)
    sin = jnp.concatenate([jnp.sin(ang), jnp.sin(ang), pad], axis=1)
    return cos, sin


def _pad_rope_cols(w):
    half = MLA_ROPE // 2
    z = jnp.zeros((w.shape[0], LANES - MLA_ROPE), w.dtype)
    rot = jnp.concatenate([-w[:, half:], w[:, :half]], axis=1)
    return jnp.concatenate([w, z], axis=1), jnp.concatenate([rot, z], axis=1)


def mla_mixer(hb, wdq, q_norm_g, wuq, wdkv, kv_norm_g, wukv, nb, s_len, nm):
    t, d = hb.shape
    ql = wdq.shape[1]
    kvl = kv_norm_g.shape[0]
    nh = wuq.shape[1] // (HEAD_DIM + MLA_ROPE)
    hd = nh * HEAD_DIM
    tr_rows = nb * s_len
    scale = float((HEAD_DIM + MLA_ROPE) ** -0.5)
    kr_w, kr_rot = _pad_rope_cols(wdkv[:, kvl:])
    w1 = jnp.concatenate([wdq, wdkv[:, :kvl], kr_w, kr_rot], axis=1).astype(BF16)
    wuq3 = wuq.reshape(ql, nh, HEAD_DIM + MLA_ROPE)
    q_rope = wuq3[:, :, HEAD_DIM:]
    half = MLA_ROPE // 2
    zq = jnp.zeros((ql, nh, LANES - MLA_ROPE), F32)
    q_rope_p = jnp.concatenate([q_rope, zq], axis=2).reshape(ql, hd)
    q_rot_p = jnp.concatenate([-q_rope[:, :, half:], q_rope[:, :, :half], zq], axis=2).reshape(ql, hd)
    wq = jnp.concatenate([wuq3[:, :, :HEAD_DIM].reshape(ql, hd), q_rope_p, q_rot_p], axis=1).astype(BF16)
    wukv3 = wukv.reshape(kvl, nh, 2 * HEAD_DIM)
    wkv = jnp.concatenate([wukv3[:, :, :HEAD_DIM].reshape(kvl, hd),
                           wukv3[:, :, HEAD_DIM:].reshape(kvl, hd)], axis=1).astype(BF16)
    cos_p, sin_p = _rope_rows(nm + s_len)
    cos_rows = jnp.concatenate([jnp.tile(cos_p[nm:], (nb, 1)), jnp.tile(cos_p[:nm], (nb, 1))], axis=0)
    sin_rows = jnp.concatenate([jnp.tile(sin_p[nm:], (nb, 1)), jnp.tile(sin_p[:nm], (nb, 1))], axis=0)

    tm = _pick(t, (128, 96, 64, 48, 32, 16))
    qn, qr, kn, v, kr = _rows_call(
        functools.partial(_mla_proj_body, ql, kvl, hd, scale), t, tm, [hb, cos_rows, sin_rows],
        [w1, q_norm_g.reshape(1, ql), kv_norm_g.reshape(1, kvl), wq, wkv],
        [(hd, BF16), (hd, BF16), (hd, BF16), (hd, BF16), (LANES, BF16)], "mla_proj")

    meta_blk = tr_rows // nm
    mspec = lambda: pl.BlockSpec((nm, LANES), lambda b, h: (meta_blk + b, h))
    mspec0 = lambda: pl.BlockSpec((nm, LANES), lambda b, h: (meta_blk + b, 0))
    o_meta = pl.pallas_call(
        _attn_meta_body, grid=(nb, nh),
        in_specs=[mspec(), mspec(), mspec(), mspec0(), mspec()],
        out_specs=mspec(), out_shape=jax.ShapeDtypeStruct((t, hd), BF16),
        compiler_params=_cparams(("parallel", "parallel")), name="attn_meta")(qn, qr, kn, kr, v)

    tq = 256
    nq = s_len // tq
    qspec = lambda: pl.BlockSpec((tq, LANES), lambda b, h, i: (b * nq + i, h))
    kspec = lambda: pl.BlockSpec((s_len, LANES), lambda b, h, i: (b, h))
    m3 = lambda: pl.BlockSpec((nm, LANES), lambda b, h, i: (meta_blk + b, h))
    return pl.pallas_call(
        functools.partial(_attn_body, tq), grid=(nb, nh, nq),
        in_specs=[qspec(), qspec(), kspec(), pl.BlockSpec((s_len, LANES), lambda b, h, i: (b, 0)), kspec(),
                  m3(), pl.BlockSpec((nm, LANES), lambda b, h, i: (meta_blk + b, 0)), m3(),
                  pl.BlockSpec(memory_space=pl.ANY)],
        out_specs=qspec(), out_shape=jax.ShapeDtypeStruct((t, hd), BF16),
        input_output_aliases={8: 0},
        compiler_params=_cparams(("parallel", "parallel", "arbitrary")), name="attn_main")(
            qn, qr, kn, kr, v, kn, kr, v, o_meta)


def _hgrn_chunk(q, fz, iv, lb, st):
    c = HGRN_BLOCK
    f = lb + (1.0 - lb) * _sigmoid(fz)
    k = (1.0 - lb) * _sigmoid(-fz)
    row = lax.broadcasted_iota(I32, (c, HEAD_DIM), 0)
    lane = lax.broadcasted_iota(I32, (c, HEAD_DIM), 1)
    khat = jnp.zeros((c, HEAD_DIM), F32)
    sc_t = jnp.zeros((c, HEAD_DIM), F32)
    p = jnp.ones((1, HEAD_DIM), F32)
    pm = jnp.zeros((c, HEAD_DIM), F32)
    for t in range(c):
        ft = f[t:t + 1, :]
        khat = khat * ft + jnp.where(row == t, k, 0.0)
        col = jnp.sum(khat * q[t:t + 1, :], axis=-1, keepdims=True)
        sc_t = jnp.where(lane == t, col, sc_t)
        p = p * ft
        pm = jnp.where(row == t, p, pm)
    nt = (((1,), (1,)), ((), ()))
    tn = (((0,), (0,)), ((), ()))
    ib = iv.astype(BF16)
    o_inter = lax.dot_general((q * pm).astype(BF16), st.astype(BF16), nt, preferred_element_type=F32)
    o_intra = lax.dot_general(sc_t.astype(BF16), ib, tn, preferred_element_type=F32)[:c]
    st_new = st * p + lax.dot_general(ib, khat.astype(BF16), tn, preferred_element_type=F32)
    return o_inter + o_intra, st_new


def _hgrn_out(o, gate, ng):
    o = o * lax.rsqrt(jnp.mean(o * o, axis=-1, keepdims=True) + RMS_EPS) * ng
    return (o * (gate * _sigmoid(gate))).astype(BF16)


def _hgrn_meta_body(gh, q_ref, fz_ref, i_ref, g_ref, lb_ref, ng_ref, o_ref, st_ref):
    for g in range(gh):
        sl = slice(g * HEAD_DIM, (g + 1) * HEAD_DIM)
        o, st = _hgrn_chunk(q_ref[:, sl], fz_ref[:, sl], i_ref[:, sl], lb_ref[:, sl],
                            jnp.zeros((HEAD_DIM, HEAD_DIM), F32))
        st_ref[0, g] = st
        o_ref[:, sl] = _hgrn_out(o, g_ref[:, sl], ng_ref[...])


def _hgrn_main_body(gh, ts, q_ref, fz_ref, i_ref, g_ref, lb_ref, ng_ref, st0_ref, alias_ref, o_ref, st_ref):
    del alias_ref

    @pl.when(pl.program_id(2) == 0)
    def _():
        st_ref[...] = st0_ref[0]

    def chunk_body(cidx, carry):
        r0 = pl.multiple_of(cidx * HGRN_BLOCK, HGRN_BLOCK)
        rows = pl.ds(r0, HGRN_BLOCK)
        for g in range(gh):
            sl = slice(g * HEAD_DIM, (g + 1) * HEAD_DIM)
            o, st = _hgrn_chunk(q_ref[rows, sl], fz_ref[rows, sl], i_ref[rows, sl], lb_ref[:, sl], st_ref[g])
            st_ref[g] = st
            o_ref[rows, sl] = _hgrn_out(o, g_ref[rows, sl], ng_ref[...])
        return carry

    lax.fori_loop(0, ts // HGRN_BLOCK, chunk_body, 0)


def hgrn_mixer(qfig, lb, norm_g, nb, s_len, nm):
    t, d4 = qfig.shape
    d = d4 // 4
    nh = d // HEAD_DIM
    gh = 4 if nh % 4 == 0 else 1
    gw = gh * HEAD_DIM
    ng_blocks = d // gw
    tr_rows = nb * s_len
    meta_blk = tr_rows // nm
    lb2 = lb.reshape(1, d)
    ng2 = norm_g.reshape(1, HEAD_DIM)

    def mspec(sec):
        return pl.BlockSpec((nm, gw), lambda b, h: (meta_blk + b, sec * ng_blocks + h))

    o_meta, st0 = pl.pallas_call(
        functools.partial(_hgrn_meta_body, gh), grid=(nb, ng_blocks),
        in_specs=[mspec(0), mspec(1), mspec(2), mspec(3),
                  pl.BlockSpec((1, gw), lambda b, h: (0, h)),
                  pl.BlockSpec((1, HEAD_DIM), lambda b, h: (0, 0))],
        out_specs=[pl.BlockSpec((nm, gw), lambda b, h: (meta_blk + b, h)),
                   pl.BlockSpec((1, gh, HEAD_DIM, HEAD_DIM), lambda b, h: (b, h, 0, 0))],
        out_shape=[jax.ShapeDtypeStruct((t, d), BF16),
                   jax.ShapeDtypeStruct((nb, nh, HEAD_DIM, HEAD_DIM), F32)],
        compiler_params=_cparams(("parallel", "parallel")), name="hgrn_meta")(qfig, qfig, qfig, qfig, lb2, ng2)

    ts = _pick(s_len, (512, 256))
    nst = s_len // ts

    def rspec(sec):
        return pl.BlockSpec((ts, gw), lambda b, h, s: (b * nst + s, sec * ng_blocks + h))

    return pl.pallas_call(
        functools.partial(_hgrn_main_body, gh, ts), grid=(nb, ng_blocks, nst),
        in_specs=[rspec(0), rspec(1), rspec(2), rspec(3),
                  pl.BlockSpec((1, gw), lambda b, h, s: (0, h)),
                  pl.BlockSpec((1, HEAD_DIM), lambda b, h, s: (0, 0)),
                  pl.BlockSpec((1, gh, HEAD_DIM, HEAD_DIM), lambda b, h, s: (b, h, 0, 0)),
                  pl.BlockSpec(memory_space=pl.ANY)],
        out_specs=pl.BlockSpec((ts, gw), lambda b, h, s: (b * nst + s, h)),
        out_shape=jax.ShapeDtypeStruct((t, d), BF16),
        scratch_shapes=[pltpu.VMEM((gh, HEAD_DIM, HEAD_DIM), F32)],
        input_output_aliases={7: 0},
        compiler_params=_cparams(("parallel", "parallel", "arbitrary")), name="hgrn_main")(
            qfig, qfig, qfig, qfig, lb2, ng2, st0, o_meta)


def _first_index_of_max(vals, idx, n, axis):
    mx = jnp.max(vals, axis=axis, keepdims=True)
    first = jnp.min(jnp.where(vals == mx, idx, n), axis=axis, keepdims=True)
    return mx, first


def _router_body(tm, h_ref, rw_ref, rb_ref, tri_ref, e_ref, gate_ref, rank_ref, cnt_ref, base_ref):
    @pl.when(pl.program_id(0) == 0)
    def _():
        base_ref[...] = jnp.zeros(base_ref.shape, F32)

    nt = (((1,), (1,)), ((), ()))
    logits = lax.dot_general(rw_ref[...], h_ref[...], nt, precision=lax.Precision.HIGHEST,
                             preferred_element_type=F32)
    scores = _sigmoid(logits)
    sel = scores + rb_ref[...]
    g, epg = N_GROUPS, EXPERTS_PER_GROUP
    sel3 = sel.reshape(g, epg, tm)
    sc3 = scores.reshape(g, epg, tm)
    idx3 = lax.broadcasted_iota(I32, (g, epg, tm), 1)
    m1, i1 = _first_index_of_max(sel3, idx3, epg, 1)
    rest = jnp.where(idx3 == i1, -jnp.inf, sel3)
    m2, i2 = _first_index_of_max(rest, idx3, epg, 1)
    gidx = lax.broadcasted_iota(I32, (g, 1, tm), 0)
    _, gtop3 = _first_index_of_max(m1 + m2, gidx, g, 0)
    pick = gidx == gtop3
    gtop = gtop3[0]
    l1 = jnp.sum(jnp.where(pick, i1, 0), axis=0)
    l2 = jnp.sum(jnp.where(pick, i2, 0), axis=0)
    sc_in = jnp.sum(jnp.where(pick, sc3, 0.0), axis=0)
    idx2 = lax.broadcasted_iota(I32, (epg, tm), 0)
    s1 = jnp.sum(jnp.where(idx2 == l1, sc_in, 0.0), axis=0, keepdims=True)
    s2 = jnp.sum(jnp.where(idx2 == l2, sc_in, 0.0), axis=0, keepdims=True)
    e1 = gtop * epg + l1
    e2 = gtop * epg + l2
    e_ref[0:1, :] = e1
    e_ref[1:2, :] = e2
    den = s1 + s2
    gate_ref[0:1, :] = s1 / den
    gate_ref[1:2, :] = s2 / den
    eidx = lax.broadcasted_iota(I32, (N_EXPERTS, tm), 0)
    oh1 = (eidx == e1).astype(F32)
    oh2 = (eidx == e2).astype(F32)
    oh = jnp.concatenate([oh1, oh2], axis=0).astype(BF16)
    pre = jnp.dot(oh, tri_ref[...], preferred_element_type=F32)
    base = base_ref[:, 0:1]
    tot1 = jnp.sum(oh1, axis=1, keepdims=True)
    tot2 = jnp.sum(oh2, axis=1, keepdims=True)
    r1 = jnp.sum(oh1 * (base + pre[:N_EXPERTS]), axis=0, keepdims=True)
    r2 = jnp.sum(oh2 * (base + tot1 + pre[N_EXPERTS:]), axis=0, keepdims=True)
    rank_ref[0:1, :] = r1.astype(I32)
    rank_ref[1:2, :] = r2.astype(I32)
    new_base = jnp.broadcast_to(base + tot1 + tot2, base_ref.shape)
    base_ref[...] = new_base
    cnt_ref[...] = new_base


def moe_route(h, router_w, router_b):
    t, d = h.shape
    tm = _pick(t, (256, 128, 96, 64, 32))
    tri = (lax.broadcasted_iota(I32, (tm, tm), 0) < lax.broadcasted_iota(I32, (tm, tm), 1)).astype(BF16)
    e, gate, rank, cnt = pl.pallas_call(
        functools.partial(_router_body, tm), grid=(t // tm,),
        in_specs=[pl.BlockSpec((tm, d), lambda i: (i, 0)),
                  pl.BlockSpec((N_EXPERTS, d), lambda i: (0, 0)),
                  pl.BlockSpec((N_EXPERTS, 1), lambda i: (0, 0)),
                  pl.BlockSpec((tm, tm), lambda i: (0, 0))],
        out_specs=[pl.BlockSpec((2, tm), lambda i: (0, i)),
                   pl.BlockSpec((2, tm), lambda i: (0, i)),
                   pl.BlockSpec((2, tm), lambda i: (0, i)),
                   pl.BlockSpec((N_EXPERTS, LANES), lambda i: (0, 0))],
        out_shape=[jax.ShapeDtypeStruct((2, t), I32), jax.ShapeDtypeStruct((2, t), F32),
                   jax.ShapeDtypeStruct((2, t), I32), jax.ShapeDtypeStruct((N_EXPERTS, LANES), F32)],
        scratch_shapes=[pltpu.VMEM((N_EXPERTS, LANES), F32)],
        compiler_params=_cparams(("arbitrary",)), name="moe_route")(
            h, router_w.T, router_b.reshape(N_EXPERTS, 1), tri)
    return e, gate, rank, cnt[:, 0].astype(I32)


def _dispatch_body(tt, zero_flag_ref, dest_ref, h_hbm, xb_hbm, zero_ref, sem):
    i = pl.program_id(0)

    @pl.when(i == 0)
    def _():
        zero_ref[...] = jnp.zeros(zero_ref.shape, F32)

        def zfill(b, carry):
            @pl.when(zero_flag_ref[b] > 0)
            def _():
                start = pl.multiple_of(b * MOE_BLOCK, MOE_BLOCK)
                cp = pltpu.make_async_copy(zero_ref, xb_hbm.at[pl.ds(start, MOE_BLOCK), :], sem)
                cp.start()
                cp.wait()
            return carry

        lax.fori_loop(0, zero_flag_ref.shape[0], zfill, 0)

    base = i * tt

    def issue(j, carry):
        for k in range(2):
            pltpu.make_async_copy(h_hbm.at[pl.ds(base + j, 1), :],
                                  xb_hbm.at[pl.ds(dest_ref[k, j], 1), :], sem).start()
        return carry

    lax.fori_loop(0, tt, issue, 0)
    pltpu.make_async_copy(h_hbm.at[pl.ds(0, 2 * tt), :], xb_hbm.at[pl.ds(0, 2 * tt), :], sem).wait()


def moe_dispatch(h, dest, zero_flag, n_slots):
    t, d = h.shape
    tt = _pick(t, (256, 128, 96, 64, 32))
    return pl.pallas_call(
        functools.partial(_dispatch_body, tt),
        grid_spec=pltpu.PrefetchScalarGridSpec(
            num_scalar_prefetch=1, grid=(t // tt,),
            in_specs=[pl.BlockSpec((2, tt), lambda i, zf: (0, i), memory_space=pltpu.SMEM),
                      pl.BlockSpec(memory_space=pl.ANY)],
            out_specs=pl.BlockSpec(memory_space=pl.ANY),
            scratch_shapes=[pltpu.VMEM((MOE_BLOCK, d), F32), pltpu.SemaphoreType.DMA(())]),
        out_shape=jax.ShapeDtypeStruct((n_slots, d), F32),
        compiler_params=_cparams(("arbitrary",)), name="moe_dispatch")(zero_flag, dest, h)


def _expert_body(be_ref, nu_ref, x_ref, wg_ref, wu_ref, wd_ref, y_ref):
    used = pl.program_id(0) < nu_ref[0]

    @pl.when(used)
    def _():
        x = x_ref[...].astype(BF16)
        g = jnp.dot(x, wg_ref[0], preferred_element_type=F32)
        u = jnp.dot(x, wu_ref[0], preferred_element_type=F32)
        a = (g * _sigmoid(g) * u).astype(BF16)
        y_ref[...] = jnp.dot(a, wd_ref[0], preferred_element_type=F32)

    @pl.when(jnp.logical_not(used))
    def _():
        y_ref[...] = jnp.zeros(y_ref.shape, F32)


def moe_experts(xb, blk_expert, n_used, wg, wu, wd):
    n_slots, d = xb.shape
    ff = wg.shape[2]
    nblk = n_slots // MOE_BLOCK
    return pl.pallas_call(
        _expert_body,
        grid_spec=pltpu.PrefetchScalarGridSpec(
            num_scalar_prefetch=2, grid=(nblk,),
            in_specs=[pl.BlockSpec((MOE_BLOCK, d), lambda i, be, nu: (i, 0)),
                      pl.BlockSpec((1, d, ff), lambda i, be, nu: (be[i], 0, 0)),
                      pl.BlockSpec((1, d, ff), lambda i, be, nu: (be[i], 0, 0)),
                      pl.BlockSpec((1, ff, d), lambda i, be, nu: (be[i], 0, 0))],
            out_specs=pl.BlockSpec((MOE_BLOCK, d), lambda i, be, nu: (i, 0))),
        out_shape=jax.ShapeDtypeStruct((n_slots, d), F32),
        compiler_params=_cparams(("arbitrary",)), name="moe_experts")(
            blk_expert, n_used, xb, wg, wu, wd)


def _combine_body(tt, alpha, dest_ref, gate_ref, h_ref, g_ref, b_ref, y_hbm, ho_ref, hb_ref, buf_ref, sem):
    def issue(j, carry):
        for k in range(2):
            pltpu.make_async_copy(y_hbm.at[pl.ds(dest_ref[k, j], 1), :],
                                  buf_ref.at[k, pl.ds(j, 1), :], sem).start()
        return carry

    lax.fori_loop(0, tt, issue, 0)
    for k in range(2):
        pltpu.make_async_copy(y_hbm.at[pl.ds(0, tt), :], buf_ref.at[k], sem).wait()
    gate = gate_ref[...]
    ffn = gate[:, 0:1] * buf_ref[0] + gate[:, 1:2] * buf_ref[1]
    o = _layer_norm(alpha * h_ref[...] + ffn, g_ref[...], b_ref[...])
    ho_ref[...] = o
    hb_ref[...] = o.astype(BF16)


def moe_combine(yb, dest, gate_t, h, ln_g, ln_b, alpha):
    t, d = h.shape
    tt = _pick(t, (256, 128, 96, 64, 32))
    return pl.pallas_call(
        functools.partial(_combine_body, tt, alpha), grid=(t // tt,),
        in_specs=[pl.BlockSpec((2, tt), lambda i: (0, i), memory_space=pltpu.SMEM),
                  pl.BlockSpec((tt, 2), lambda i: (i, 0)),
                  pl.BlockSpec((tt, d), lambda i: (i, 0)),
                  pl.BlockSpec((1, d), lambda i: (0, 0)),
                  pl.BlockSpec((1, d), lambda i: (0, 0)),
                  pl.BlockSpec(memory_space=pl.ANY)],
        out_specs=[pl.BlockSpec((tt, d), lambda i: (i, 0)), pl.BlockSpec((tt, d), lambda i: (i, 0))],
        out_shape=[jax.ShapeDtypeStruct((t, d), F32), jax.ShapeDtypeStruct((t, d), BF16)],
        scratch_shapes=[pltpu.VMEM((2, tt, d), F32), pltpu.SemaphoreType.DMA(())],
        compiler_params=_cparams(("arbitrary",)), name="moe_combine")(
            dest, gate_t, h, ln_g.reshape(1, d), ln_b.reshape(1, d), yb)


def moe_layer(h, router_w, router_b, w_gate, w_up, w_down, ln_g, ln_b, alpha):
    t, d = h.shape
    e_idx, gate, rank, counts = moe_route(h, router_w, router_b)
    padded = (counts + MOE_BLOCK - 1) // MOE_BLOCK * MOE_BLOCK
    pends = jnp.cumsum(padded)
    pstart = pends - padded
    dest = pstart[e_idx] + rank
    nblk = -(-(2 * t) // MOE_BLOCK) + N_EXPERTS
    n_used = (pends[-1] // MOE_BLOCK).astype(I32)
    blk = jnp.arange(nblk, dtype=I32)
    blk_expert = jnp.minimum(jnp.searchsorted(pends, jnp.minimum(blk, n_used - 1) * MOE_BLOCK, side='right'),
                             N_EXPERTS - 1).astype(I32)
    zero_flag = ((blk >= n_used) | (blk == pends[blk_expert] // MOE_BLOCK - 1)).astype(I32)
    xb = moe_dispatch(h, dest, zero_flag, nblk * MOE_BLOCK)
    yb = moe_experts(xb, blk_expert, n_used.reshape(1), w_gate.astype(BF16), w_up.astype(BF16),
                     w_down.astype(BF16))
    return moe_combine(yb, dest, gate.T, h, ln_g, ln_b, alpha)


def kernel(x, meta_tokens, ln_mix_g, ln_mix_b, ln_ffn_g, ln_ffn_b, conv_pw1_w, conv_pw1_b, conv_dw_w, conv_dw_b, conv_ln_g, conv_ln_b, conv_pw2_w, conv_pw2_b, mla_wdq, mla_q_norm_g, mla_wuq, mla_wdkv, mla_kv_norm_g, mla_wukv, mla_wo, hgrn_w_in, hgrn_lb_logits, hgrn_norm_g, hgrn_wo, router_w, router_b, moe_w_gate, moe_w_up, moe_w_down):
    nb, s_len, d = x.shape
    nm = meta_tokens.shape[0]
    depth = ln_mix_g.shape[0]
    alpha = float((2 * depth) ** 0.25)
    zero_bias = jnp.zeros((d,), F32)

    meta = jnp.broadcast_to(meta_tokens[None].astype(x.dtype), (nb, nm, d)).reshape(nb * nm, d)
    h = jnp.concatenate([x.reshape(nb * s_len, d), meta], axis=0)
    hb = h.astype(BF16)
    p_lb = jax.nn.softmax(hgrn_lb_logits.astype(F32), axis=0)
    lower_bounds = jnp.cumsum(p_lb, axis=0) - p_lb[0]

    for i in range(depth):
        j = i // N_MIXERS
        kind = i % N_MIXERS
        if kind == 0:
            u = mm_glu(hb, conv_pw1_w[j].astype(BF16), conv_pw1_b[j])
            mix_in = conv_ln_swish(u, conv_dw_w[j], conv_dw_b[j], conv_ln_g[j], conv_ln_b[j], nb, s_len, nm)
            w_out, b_out = conv_pw2_w[j], conv_pw2_b[j]
        elif kind == 1:
            mix_in = mla_mixer(hb, mla_wdq[j], mla_q_norm_g[j], mla_wuq[j], mla_wdkv[j], mla_kv_norm_g[j],
                               mla_wukv[j], nb, s_len, nm)
            w_out, b_out = mla_wo[j], zero_bias
        else:
            qfig = mm_plain(hb, hgrn_w_in[j].astype(BF16))
            mix_in = hgrn_mixer(qfig, lower_bounds[i], hgrn_norm_g[j], nb, s_len, nm)
            w_out, b_out = hgrn_wo[j], zero_bias
        h, hb = mm_res_ln(mix_in, h, w_out.astype(BF16), b_out, ln_mix_g[i], ln_mix_b[i], alpha)
        h, hb = moe_layer(h, router_w, router_b, moe_w_gate[i], moe_w_up[i], moe_w_down[i],
                          ln_ffn_g[i], ln_ffn_b[i], alpha)
    return h[:nb * s_len].reshape(nb, s_len, d)
```

```python
import functools

import jax
import jax.numpy as jnp
from jax import lax
from jax.experimental import pallas as pl
from jax.experimental.pallas import tpu as pltpu

F32 = jnp.float32
BF16 = jnp.bfloat16
I32 = jnp.int32

LANES = 128
CHUNK = 64
N_MIXERS = 3
CONV_WIDTH = 31
CONV_HALO = 32
CONV_LANES = 256
HEAD_DIM = 128
MLA_ROPE = 64
ROPE_THETA = 10000.0
HGRN_BLOCK = 16
N_EXPERTS = 32
N_GROUPS = 4
EXPERTS_PER_GROUP = N_EXPERTS // N_GROUPS
MOE_BLOCK = 256
LN_EPS = 1e-5
RMS_EPS = 1e-6
VMEM_LIMIT = 52 * 1024 * 1024


def _pick(n, cands):
    for c in cands:
        if n % c == 0:
            return c
    raise ValueError(f"no tile for {n} in {cands}")


_ROW_TILES = (768, 512, 384, 256, 192, 176, 128, 96, 64, 48, 32, 16)


def _cparams(sem, vmem=VMEM_LIMIT):
    return pltpu.CompilerParams(dimension_semantics=sem, vmem_limit_bytes=vmem)


def _layer_norm(x, g, b):
    mu = jnp.mean(x, axis=-1, keepdims=True)
    xc = x - mu
    var = jnp.mean(xc * xc, axis=-1, keepdims=True)
    return xc * lax.rsqrt(var + LN_EPS) * g + b


def _rms_norm(x, g):
    return x * lax.rsqrt(jnp.mean(x * x, axis=-1, keepdims=True) + RMS_EPS) * g


def _sigmoid(x):
    return 1.0 / (1.0 + jnp.exp(-x))


def _rows_call(body, n_rows, tm, row_ins, full_ins, outs, name):
    grid = (n_rows // tm,)
    in_specs = [pl.BlockSpec((tm, a.shape[1]), lambda i: (i, 0)) for a in row_ins]
    in_specs += [pl.BlockSpec(a.shape, lambda i, nd=a.ndim: (0,) * nd) for a in full_ins]
    out_specs = [pl.BlockSpec((tm, n), lambda i: (i, 0)) for n, _ in outs]
    out_shape = [jax.ShapeDtypeStruct((n_rows, n), dt) for n, dt in outs]
    return pl.pallas_call(
        body, grid=grid, in_specs=in_specs, out_specs=out_specs, out_shape=out_shape,
        compiler_params=_cparams(("parallel",)), name=name)(*row_ins, *full_ins)


def _mm_res_ln_body(alpha, x_ref, h_ref, w_ref, bias_ref, g_ref, b_ref, ho_ref, hb_ref):
    acc = jnp.dot(x_ref[...], w_ref[...], preferred_element_type=F32)
    y = alpha * h_ref[...] + (acc + bias_ref[...])
    o = _layer_norm(y, g_ref[...], b_ref[...])
    ho_ref[...] = o
    hb_ref[...] = o.astype(BF16)


def mm_res_ln(x_bf, h, w_bf, bias, g, b, alpha):
    t, d = h.shape
    tm = _pick(t, (256, 192, 176, 128, 96, 64, 48, 32, 16))
    return _rows_call(functools.partial(_mm_res_ln_body, alpha), t, tm, [x_bf, h],
                      [w_bf, bias.reshape(1, d), g.reshape(1, d), b.reshape(1, d)],
                      [(d, F32), (d, BF16)], "mm_res_ln")


def _glu_body(x_ref, wa_ref, wg_ref, ba_ref, bg_ref, u_ref):
    x = x_ref[...]
    a = jnp.dot(x, wa_ref[...], preferred_element_type=F32) + ba_ref[...]
    g = jnp.dot(x, wg_ref[...], preferred_element_type=F32) + bg_ref[...]
    u_ref[...] = a * _sigmoid(g)


def mm_glu(x_bf, w_bf, bias):
    t, k = x_bf.shape
    d = w_bf.shape[1] // 2
    tm = _pick(t, _ROW_TILES)
    tn = min(d, 1024)
    nj = d // tn
    bias2 = bias.reshape(1, 2 * d)
    return pl.pallas_call(
        _glu_body, grid=(nj, t // tm),
        in_specs=[pl.BlockSpec((tm, k), lambda j, i: (i, 0)),
                  pl.BlockSpec((k, tn), lambda j, i: (0, j)),
                  pl.BlockSpec((k, tn), lambda j, i: (0, j + nj)),
                  pl.BlockSpec((1, tn), lambda j, i: (0, j)),
                  pl.BlockSpec((1, tn), lambda j, i: (0, j + nj))],
        out_specs=pl.BlockSpec((tm, tn), lambda j, i: (i, j)),
        out_shape=jax.ShapeDtypeStruct((t, d), F32),
        compiler_params=_cparams(("parallel", "parallel")), name="mm_glu")(x_bf, w_bf, w_bf, bias2, bias2)


def _mm_plain_body(x_ref, w_ref, o_ref):
    o_ref[...] = jnp.dot(x_ref[...], w_ref[...], preferred_element_type=F32).astype(o_ref.dtype)


def mm_plain(x_bf, w_bf, out_dtype=F32):
    t, k = x_bf.shape
    n = w_bf.shape[1]
    tm = _pick(t, _ROW_TILES)
    tn = min(n, 1024)
    return pl.pallas_call(
        _mm_plain_body, grid=(n // tn, t // tm),
        in_specs=[pl.BlockSpec((tm, k), lambda j, i: (i, 0)),
                  pl.BlockSpec((k, tn), lambda j, i: (0, j))],
        out_specs=pl.BlockSpec((tm, tn), lambda j, i: (i, j)),
        out_shape=jax.ShapeDtypeStruct((t, n), out_dtype),
        compiler_params=_cparams(("parallel", "parallel")), name="mm_plain")(x_bf, w_bf)


def _conv_rows(ext_ref, w_ref, y_ref, sh_ref, rows, rc, lc):
    d = y_ref.shape[1]
    shift = CONV_HALO - (CONV_WIDTH - 1)
    sub = 8
    n_sh = sh_ref.shape[1]

    def lane_body(c, carry):
        l0 = pl.multiple_of(c * lc, lc)
        lanes = pl.ds(l0, lc)
        wv = w_ref[:, lanes]
        for b in range(1, sub):
            sh_ref[b - 1] = ext_ref[pl.ds(b, n_sh), lanes]
        for r in range(rows // rc):
            acc = None
            for k in range(CONV_WIDTH):
                b = (shift + k) % sub
                a = r * rc + (shift + k) - b
                src = ext_ref[pl.ds(a, rc), lanes] if b == 0 else sh_ref[b - 1, pl.ds(a, rc), :]
                term = src * wv[k:k + 1, :]
                acc = term if acc is None else acc + term
            y_ref[pl.ds(r * rc, rc), lanes] = acc
        return carry

    lax.fori_loop(0, d // lc, lane_body, 0)


def _conv_epilogue(y_ref, dwb_ref, g_ref, b_ref, o_ref):
    rows = y_ref.shape[0]
    step = 16

    def body(c, carry):
        sl = pl.ds(pl.multiple_of(c * step, step), step)
        z = _layer_norm(y_ref[sl, :] + dwb_ref[...], g_ref[...], b_ref[...])
        o_ref[sl, :] = (z * _sigmoid(z)).astype(BF16)
        return carry

    lax.fori_loop(0, rows // step, body, 0)


def _conv_main_body(tr, cur_ref, prev_ref, meta_ref, w_ref, dwb_ref, g_ref, b_ref, alias_ref, o_ref,
                    ext_ref, y_ref, sh_ref):
    del alias_ref
    s = pl.program_id(1)
    nm = meta_ref.shape[0]

    @pl.when(s == 0)
    def _():
        ext_ref[0:CONV_HALO - nm, :] = jnp.zeros((CONV_HALO - nm, ext_ref.shape[1]), F32)
        ext_ref[CONV_HALO - nm:CONV_HALO, :] = meta_ref[...]

    @pl.when(s > 0)
    def _():
        ext_ref[0:CONV_HALO, :] = prev_ref[...]

    ext_ref[CONV_HALO:CONV_HALO + tr, :] = cur_ref[...]
    _conv_rows(ext_ref, w_ref, y_ref, sh_ref, tr, 64, CONV_LANES)
    _conv_epilogue(y_ref, dwb_ref, g_ref, b_ref, o_ref)


def _conv_meta_body(meta_ref, w_ref, dwb_ref, g_ref, b_ref, o_ref, ext_ref, y_ref, sh_ref):
    nm = meta_ref.shape[0]
    ext_ref[0:CONV_HALO, :] = jnp.zeros((CONV_HALO, ext_ref.shape[1]), F32)
    ext_ref[CONV_HALO:CONV_HALO + nm, :] = meta_ref[...]
    _conv_rows(ext_ref, w_ref, y_ref, sh_ref, nm, nm, CONV_LANES)
    _conv_epilogue(y_ref, dwb_ref, g_ref, b_ref, o_ref)


def conv_ln_swish(u, dw_w, dw_b, ln_g, ln_b, nb, s_len, nm):
    t, d = u.shape
    tr_rows = nb * s_len
    tr = _pick(s_len, (512, 256))
    w_pad = jnp.concatenate([dw_w, jnp.zeros((CONV_HALO - CONV_WIDTH, d), F32)], axis=0)
    vecs = [dw_b.reshape(1, d), ln_g.reshape(1, d), ln_b.reshape(1, d)]
    meta_blk = tr_rows // nm
    out_meta = pl.pallas_call(
        _conv_meta_body, grid=(nb,),
        in_specs=[pl.BlockSpec((nm, d), lambda b: (meta_blk + b, 0)),
                  pl.BlockSpec((CONV_HALO, d), lambda b: (0, 0))]
        + [pl.BlockSpec((1, d), lambda b: (0, 0))] * 3,
        out_specs=pl.BlockSpec((nm, d), lambda b: (meta_blk + b, 0)),
        out_shape=jax.ShapeDtypeStruct((t, d), BF16),
        scratch_shapes=[pltpu.VMEM((CONV_HALO + nm, d), F32), pltpu.VMEM((nm, d), F32),
                        pltpu.VMEM((7, CONV_HALO + nm - 8, CONV_LANES), F32)],
        compiler_params=_cparams(("parallel",)), name="conv_meta")(u, w_pad, *vecs)
    nst = s_len // tr
    halo_per_tile = tr // CONV_HALO
    return pl.pallas_call(
        functools.partial(_conv_main_body, tr), grid=(nb, nst),
        in_specs=[pl.BlockSpec((tr, d), lambda b, s: (b * nst + s, 0)),
                  pl.BlockSpec((CONV_HALO, d),
                               lambda b, s: (jnp.maximum((b * nst + s) * halo_per_tile - 1, 0), 0)),
                  pl.BlockSpec((nm, d), lambda b, s: (meta_blk + b, 0)),
                  pl.BlockSpec((CONV_HALO, d), lambda b, s: (0, 0))]
        + [pl.BlockSpec((1, d), lambda b, s: (0, 0))] * 3
        + [pl.BlockSpec(memory_space=pl.ANY)],
        out_specs=pl.BlockSpec((tr, d), lambda b, s: (b * nst + s, 0)),
        out_shape=jax.ShapeDtypeStruct((t, d), BF16),
        scratch_shapes=[pltpu.VMEM((CONV_HALO + tr, d), F32), pltpu.VMEM((tr, d), F32),
                        pltpu.VMEM((7, CONV_HALO + tr - 8, CONV_LANES), F32)],
        input_output_aliases={7: 0},
        compiler_params=_cparams(("parallel", "arbitrary")), name="conv_main")(u, u, u, w_pad, *vecs, out_meta)


def _mla_proj_body(ql, kvl, hd, scale, x_ref, cos_ref, sin_ref, w1_ref, qg_ref, kvg_ref, wq_ref, wkv_ref,
                   qn_ref, qr_ref, kn_ref, v_ref, kr_ref):
    a = jnp.dot(x_ref[...], w1_ref[...], preferred_element_type=F32)
    cos = cos_ref[...]
    sin = sin_ref[...]
    cq = _rms_norm(a[:, :ql], qg_ref[...]).astype(BF16)
    ckv = _rms_norm(a[:, ql:ql + kvl], kvg_ref[...]).astype(BF16)
    r0 = ql + kvl
    kr_ref[...] = (a[:, r0:r0 + LANES] * cos + a[:, r0 + LANES:r0 + 2 * LANES] * sin).astype(BF16)
    qa = jnp.dot(cq, wq_ref[...], preferred_element_type=F32)
    nh = hd // LANES
    cos_t = jnp.tile(cos, (1, nh))
    sin_t = jnp.tile(sin, (1, nh))
    qn_ref[...] = (qa[:, :hd] * scale).astype(BF16)
    qr_ref[...] = ((qa[:, hd:2 * hd] * cos_t + qa[:, 2 * hd:] * sin_t) * scale).astype(BF16)
    kv = jnp.dot(ckv, wkv_ref[...], preferred_element_type=F32)
    kn_ref[...] = kv[:, :hd].astype(BF16)
    v_ref[...] = kv[:, hd:].astype(BF16)


def _attn_body(tq, qn_ref, qr_ref, kn_ref, kr_ref, v_ref, knm_ref, krm_ref, vm_ref, alias_ref, o_ref, kf_ref):
    del alias_ref
    s_len = qn_ref.shape[0]
    nt = (((1,), (1,)), ((), ()))
    kf_ref[:, :LANES] = kn_ref[...]
    kf_ref[:, LANES:] = kr_ref[...]
    km = jnp.concatenate([knm_ref[...], krm_ref[...]], axis=-1)
    vm = vm_ref[...]
    row_c = lax.broadcasted_iota(I32, (tq, tq), 0) // CHUNK
    col_c = lax.broadcasted_iota(I32, (tq, tq), 1) // CHUNK
    visible = col_c <= row_c
    for i in range(s_len // tq):
        r0 = i * tq
        q = jnp.concatenate([qn_ref[r0:r0 + tq, :], qr_ref[r0:r0 + tq, :]], axis=-1)
        s_m = lax.dot_general(q, km, nt, preferred_element_type=F32)
        s_d = lax.dot_general(q, kf_ref[r0:r0 + tq, :], nt, preferred_element_type=F32)
        s_d = jnp.where(visible, s_d, -jnp.inf)
        m = jnp.maximum(jnp.max(s_m, axis=-1, keepdims=True), jnp.max(s_d, axis=-1, keepdims=True))
        if i > 0:
            s_p = lax.dot_general(q, kf_ref[0:r0, :], nt, preferred_element_type=F32)
            m = jnp.maximum(m, jnp.max(s_p, axis=-1, keepdims=True))
        p_m = jnp.exp(s_m - m)
        p_d = jnp.exp(s_d - m)
        l = jnp.sum(p_m, axis=-1, keepdims=True) + jnp.sum(p_d, axis=-1, keepdims=True)
        acc = jnp.dot(p_m.astype(BF16), vm, preferred_element_type=F32)
        acc = acc + jnp.dot(p_d.astype(BF16), v_ref[r0:r0 + tq, :], preferred_element_type=F32)
        if i > 0:
            p_p = jnp.exp(s_p - m)
            l = l + jnp.sum(p_p, axis=-1, keepdims=True)
            acc = acc + jnp.dot(p_p.astype(BF16), v_ref[0:r0, :], preferred_element_type=F32)
        o_ref[r0:r0 + tq, :] = (acc / l).astype(BF16)


def _attn_meta_body(qn_ref, qr_ref, kn_ref, kr_ref, v_ref, o_ref):
    nt = (((1,), (1,)), ((), ()))
    q = jnp.concatenate([qn_ref[...], qr_ref[...]], axis=-1)
    k = jnp.concatenate([kn_ref[...], kr_ref[...]], axis=-1)
    s = lax.dot_general(q, k, nt, preferred_element_type=F32)
    p = jnp.exp(s - jnp.max(s, axis=-1, keepdims=True))
    l = jnp.sum(p, axis=-1, keepdims=True)
    o_ref[...] = (jnp.dot(p.astype(BF16), v_ref[...], preferred_element_type=F32) / l).astype(BF16)


def _rope_rows(n_pos):
    inv = ROPE_THETA ** (-jnp.arange(0, MLA_ROPE, 2, dtype=F32) / MLA_ROPE)
    ang = jnp.arange(n_pos, dtype=F32)[:, None] * inv[None, :]
    pad = jnp.zeros((n_pos, LANES - MLA_ROPE), F32)
    cos = jnp.concatenate([jnp.cos(ang), jnp.cos(ang), pad], axis=1)
    sin = jnp.concatenate([jnp.sin(ang), jnp.sin(ang), pad], axis=1)
    return cos, sin


def _pad_rope_cols(w):
    half = MLA_ROPE // 2
    z = jnp.zeros((w.shape[0], LANES - MLA_ROPE), w.dtype)
    rot = jnp.concatenate([-w[:, half:], w[:, :half]], axis=1)
    return jnp.concatenate([w, z], axis=1), jnp.concatenate([rot, z], axis=1)


def mla_mixer(hb, wdq, q_norm_g, wuq, wdkv, kv_norm_g, wukv, nb, s_len, nm):
    t, d = hb.shape
    ql = wdq.shape[1]
    kvl = kv_norm_g.shape[0]
    nh = wuq.shape[1] // (HEAD_DIM + MLA_ROPE)
    hd = nh * HEAD_DIM
    tr_rows = nb * s_len
    scale = float((HEAD_DIM + MLA_ROPE) ** -0.5)
    kr_w, kr_rot = _pad_rope_cols(wdkv[:, kvl:])
    w1 = jnp.concatenate([wdq, wdkv[:, :kvl], kr_w, kr_rot], axis=1).astype(BF16)
    wuq3 = wuq.reshape(ql, nh, HEAD_DIM + MLA_ROPE)
    q_rope = wuq3[:, :, HEAD_DIM:]
    half = MLA_ROPE // 2
    zq = jnp.zeros((ql, nh, LANES - MLA_ROPE), F32)
    q_rope_p = jnp.concatenate([q_rope, zq], axis=2).reshape(ql, hd)
    q_rot_p = jnp.concatenate([-q_rope[:, :, half:], q_rope[:, :, :half], zq], axis=2).reshape(ql, hd)
    wq = jnp.concatenate([wuq3[:, :, :HEAD_DIM].reshape(ql, hd), q_rope_p, q_rot_p], axis=1).astype(BF16)
    wukv3 = wukv.reshape(kvl, nh, 2 * HEAD_DIM)
    wkv = jnp.concatenate([wukv3[:, :, :HEAD_DIM].reshape(kvl, hd),
                           wukv3[:, :, HEAD_DIM:].reshape(kvl, hd)], axis=1).astype(BF16)
    cos_p, sin_p = _rope_rows(nm + s_len)
    cos_rows = jnp.concatenate([jnp.tile(cos_p[nm:], (nb, 1)), jnp.tile(cos_p[:nm], (nb, 1))], axis=0)
    sin_rows = jnp.concatenate([jnp.tile(sin_p[nm:], (nb, 1)), jnp.tile(sin_p[:nm], (nb, 1))], axis=0)

    tm = _pick(t, (128, 96, 64, 48, 32, 16))
    qn, qr, kn, v, kr = _rows_call(
        functools.partial(_mla_proj_body, ql, kvl, hd, scale), t, tm, [hb, cos_rows, sin_rows],
        [w1, q_norm_g.reshape(1, ql), kv_norm_g.reshape(1, kvl), wq, wkv],
        [(hd, BF16), (hd, BF16), (hd, BF16), (hd, BF16), (LANES, BF16)], "mla_proj")

    meta_blk = tr_rows // nm
    mspec = lambda: pl.BlockSpec((nm, LANES), lambda b, h: (meta_blk + b, h))
    mspec0 = lambda: pl.BlockSpec((nm, LANES), lambda b, h: (meta_blk + b, 0))
    o_meta = pl.pallas_call(
        _attn_meta_body, grid=(nb, nh),
        in_specs=[mspec(), mspec(), mspec(), mspec0(), mspec()],
        out_specs=mspec(), out_shape=jax.ShapeDtypeStruct((t, hd), BF16),
        compiler_params=_cparams(("parallel", "parallel")), name="attn_meta")(qn, qr, kn, kr, v)

    tq = 256
    kspec = lambda: pl.BlockSpec((s_len, LANES), lambda b, h: (b, h))
    kspec0 = lambda: pl.BlockSpec((s_len, LANES), lambda b, h: (b, 0))
    return pl.pallas_call(
        functools.partial(_attn_body, tq), grid=(nb, nh),
        in_specs=[kspec(), kspec(), kspec(), kspec0(), kspec(), mspec(), mspec0(), mspec(),
                  pl.BlockSpec(memory_space=pl.ANY)],
        out_specs=kspec(), out_shape=jax.ShapeDtypeStruct((t, hd), BF16),
        scratch_shapes=[pltpu.VMEM((s_len, 2 * LANES), BF16)],
        input_output_aliases={8: 0},
        compiler_params=_cparams(("parallel", "parallel")), name="attn_main")(
            qn, qr, kn, kr, v, kn, kr, v, o_meta)


def _hgrn_chunk(q, fz, iv, lb, st):
    c = HGRN_BLOCK
    f = lb + (1.0 - lb) * _sigmoid(fz)
    k = (1.0 - lb) * _sigmoid(-fz)
    row = lax.broadcasted_iota(I32, (c, HEAD_DIM), 0)
    lane = lax.broadcasted_iota(I32, (c, HEAD_DIM), 1)
    khat = jnp.zeros((c, HEAD_DIM), F32)
    sc_t = jnp.zeros((c, HEAD_DIM), F32)
    p = jnp.ones((1, HEAD_DIM), F32)
    pm = jnp.zeros((c, HEAD_DIM), F32)
    for t in range(c):
        ft = f[t:t + 1, :]
        khat = khat * ft + jnp.where(row == t, k, 0.0)
        col = jnp.sum(khat * q[t:t + 1, :], axis=-1, keepdims=True)
        sc_t = jnp.where(lane == t, col, sc_t)
        p = p * ft
        pm = jnp.where(row == t, p, pm)
    nt = (((1,), (1,)), ((), ()))
    tn = (((0,), (0,)), ((), ()))
    ib = iv.astype(BF16)
    o_inter = lax.dot_general((q * pm).astype(BF16), st.astype(BF16), nt, preferred_element_type=F32)
    o_intra = lax.dot_general(sc_t.astype(BF16), ib, tn, preferred_element_type=F32)[:c]
    st_new = st * p + lax.dot_general(ib, khat.astype(BF16), tn, preferred_element_type=F32)
    return o_inter + o_intra, st_new


def _hgrn_out(o, gate, ng):
    o = o * lax.rsqrt(jnp.mean(o * o, axis=-1, keepdims=True) + RMS_EPS) * ng
    return (o * (gate * _sigmoid(gate))).astype(BF16)


def _hgrn_meta_body(gh, q_ref, fz_ref, i_ref, g_ref, lb_ref, ng_ref, o_ref, st_ref):
    for g in range(gh):
        sl = slice(g * HEAD_DIM, (g + 1) * HEAD_DIM)
        o, st = _hgrn_chunk(q_ref[:, sl], fz_ref[:, sl], i_ref[:, sl], lb_ref[:, sl],
                            jnp.zeros((HEAD_DIM, HEAD_DIM), F32))
        st_ref[0, g] = st
        o_ref[:, sl] = _hgrn_out(o, g_ref[:, sl], ng_ref[...])


def _hgrn_main_body(gh, ts, q_ref, fz_ref, i_ref, g_ref, lb_ref, ng_ref, st0_ref, alias_ref, o_ref, st_ref):
    del alias_ref

    @pl.when(pl.program_id(2) == 0)
    def _():
        st_ref[...] = st0_ref[0]

    def chunk_body(cidx, carry):
        r0 = pl.multiple_of(cidx * HGRN_BLOCK, HGRN_BLOCK)
        rows = pl.ds(r0, HGRN_BLOCK)
        for g in range(gh):
            sl = slice(g * HEAD_DIM, (g + 1) * HEAD_DIM)
            o, st = _hgrn_chunk(q_ref[rows, sl], fz_ref[rows, sl], i_ref[rows, sl], lb_ref[:, sl], st_ref[g])
            st_ref[g] = st
            o_ref[rows, sl] = _hgrn_out(o, g_ref[rows, sl], ng_ref[...])
        return carry

    lax.fori_loop(0, ts // HGRN_BLOCK, chunk_body, 0)


def hgrn_mixer(qfig, lb, norm_g, nb, s_len, nm):
    t, d4 = qfig.shape
    d = d4 // 4
    nh = d // HEAD_DIM
    gh = 4 if nh % 4 == 0 else 1
    gw = gh * HEAD_DIM
    ng_blocks = d // gw
    tr_rows = nb * s_len
    meta_blk = tr_rows // nm
    lb2 = lb.reshape(1, d)
    ng2 = norm_g.reshape(1, HEAD_DIM)

    def mspec(sec):
        return pl.BlockSpec((nm, gw), lambda b, h: (meta_blk + b, sec * ng_blocks + h))

    o_meta, st0 = pl.pallas_call(
        functools.partial(_hgrn_meta_body, gh), grid=(nb, ng_blocks),
        in_specs=[mspec(0), mspec(1), mspec(2), mspec(3),
                  pl.BlockSpec((1, gw), lambda b, h: (0, h)),
                  pl.BlockSpec((1, HEAD_DIM), lambda b, h: (0, 0))],
        out_specs=[pl.BlockSpec((nm, gw), lambda b, h: (meta_blk + b, h)),
                   pl.BlockSpec((1, gh, HEAD_DIM, HEAD_DIM), lambda b, h: (b, h, 0, 0))],
        out_shape=[jax.ShapeDtypeStruct((t, d), BF16),
                   jax.ShapeDtypeStruct((nb, nh, HEAD_DIM, HEAD_DIM), F32)],
        compiler_params=_cparams(("parallel", "parallel")), name="hgrn_meta")(qfig, qfig, qfig, qfig, lb2, ng2)

    ts = _pick(s_len, (512, 256))
    nst = s_len // ts

    def rspec(sec):
        return pl.BlockSpec((ts, gw), lambda b, h, s: (b * nst + s, sec * ng_blocks + h))

    return pl.pallas_call(
        functools.partial(_hgrn_main_body, gh, ts), grid=(nb, ng_blocks, nst),
        in_specs=[rspec(0), rspec(1), rspec(2), rspec(3),
                  pl.BlockSpec((1, gw), lambda b, h, s: (0, h)),
                  pl.BlockSpec((1, HEAD_DIM), lambda b, h, s: (0, 0)),
                  pl.BlockSpec((1, gh, HEAD_DIM, HEAD_DIM), lambda b, h, s: (b, h, 0, 0)),
                  pl.BlockSpec(memory_space=pl.ANY)],
        out_specs=pl.BlockSpec((ts, gw), lambda b, h, s: (b * nst + s, h)),
        out_shape=jax.ShapeDtypeStruct((t, d), BF16),
        scratch_shapes=[pltpu.VMEM((gh, HEAD_DIM, HEAD_DIM), F32)],
        input_output_aliases={7: 0},
        compiler_params=_cparams(("parallel", "parallel", "arbitrary")), name="hgrn_main")(
            qfig, qfig, qfig, qfig, lb2, ng2, st0, o_meta)


def _first_index_of_max(vals, idx, n, axis):
    mx = jnp.max(vals, axis=axis, keepdims=True)
    first = jnp.min(jnp.where(vals == mx, idx, n), axis=axis, keepdims=True)
    return mx, first


def _router_body(tm, h_ref, rw_ref, rb_ref, tri_ref, e_ref, gate_ref, rank_ref, cnt_ref, base_ref):
    @pl.when(pl.program_id(0) == 0)
    def _():
        base_ref[...] = jnp.zeros(base_ref.shape, F32)

    nt = (((1,), (1,)), ((), ()))
    logits = lax.dot_general(rw_ref[...], h_ref[...], nt, precision=lax.Precision.HIGHEST,
                             preferred_element_type=F32)
    scores = _sigmoid(logits)
    sel = scores + rb_ref[...]
    g, epg = N_GROUPS, EXPERTS_PER_GROUP
    sel3 = sel.reshape(g, epg, tm)
    sc3 = scores.reshape(g, epg, tm)
    idx3 = lax.broadcasted_iota(I32, (g, epg, tm), 1)
    m1, i1 = _first_index_of_max(sel3, idx3, epg, 1)
    rest = jnp.where(idx3 == i1, -jnp.inf, sel3)
    m2, i2 = _first_index_of_max(rest, idx3, epg, 1)
    gidx = lax.broadcasted_iota(I32, (g, 1, tm), 0)
    _, gtop3 = _first_index_of_max(m1 + m2, gidx, g, 0)
    pick = gidx == gtop3
    gtop = gtop3[0]
    l1 = jnp.sum(jnp.where(pick, i1, 0), axis=0)
    l2 = jnp.sum(jnp.where(pick, i2, 0), axis=0)
    sc_in = jnp.sum(jnp.where(pick, sc3, 0.0), axis=0)
    idx2 = lax.broadcasted_iota(I32, (epg, tm), 0)
    s1 = jnp.sum(jnp.where(idx2 == l1, sc_in, 0.0), axis=0, keepdims=True)
    s2 = jnp.sum(jnp.where(idx2 == l2, sc_in, 0.0), axis=0, keepdims=True)
    e1 = gtop * epg + l1
    e2 = gtop * epg + l2
    e_ref[0:1, :] = e1
    e_ref[1:2, :] = e2
    den = s1 + s2
    gate_ref[0:1, :] = s1 / den
    gate_ref[1:2, :] = s2 / den
    eidx = lax.broadcasted_iota(I32, (N_EXPERTS, tm), 0)
    oh1 = (eidx == e1).astype(F32)
    oh2 = (eidx == e2).astype(F32)
    oh = jnp.concatenate([oh1, oh2], axis=0).astype(BF16)
    pre = jnp.dot(oh, tri_ref[...], preferred_element_type=F32)
    base = base_ref[:, 0:1]
    tot1 = jnp.sum(oh1, axis=1, keepdims=True)
    tot2 = jnp.sum(oh2, axis=1, keepdims=True)
    r1 = jnp.sum(oh1 * (base + pre[:N_EXPERTS]), axis=0, keepdims=True)
    r2 = jnp.sum(oh2 * (base + tot1 + pre[N_EXPERTS:]), axis=0, keepdims=True)
    rank_ref[0:1, :] = r1.astype(I32)
    rank_ref[1:2, :] = r2.astype(I32)
    new_base = jnp.broadcast_to(base + tot1 + tot2, base_ref.shape)
    base_ref[...] = new_base
    cnt_ref[...] = new_base


def moe_route(h, router_w, router_b):
    t, d = h.shape
    tm = _pick(t, (256, 128, 96, 64, 32))
    tri = (lax.broadcasted_iota(I32, (tm, tm), 0) < lax.broadcasted_iota(I32, (tm, tm), 1)).astype(BF16)
    e, gate, rank, cnt = pl.pallas_call(
        functools.partial(_router_body, tm), grid=(t // tm,),
        in_specs=[pl.BlockSpec((tm, d), lambda i: (i, 0)),
                  pl.BlockSpec((N_EXPERTS, d), lambda i: (0, 0)),
                  pl.BlockSpec((N_EXPERTS, 1), lambda i: (0, 0)),
                  pl.BlockSpec((tm, tm), lambda i: (0, 0))],
        out_specs=[pl.BlockSpec((2, tm), lambda i: (0, i)),
                   pl.BlockSpec((2, tm), lambda i: (0, i)),
                   pl.BlockSpec((2, tm), lambda i: (0, i)),
                   pl.BlockSpec((N_EXPERTS, LANES), lambda i: (0, 0))],
        out_shape=[jax.ShapeDtypeStruct((2, t), I32), jax.ShapeDtypeStruct((2, t), F32),
                   jax.ShapeDtypeStruct((2, t), I32), jax.ShapeDtypeStruct((N_EXPERTS, LANES), F32)],
        scratch_shapes=[pltpu.VMEM((N_EXPERTS, LANES), F32)],
        compiler_params=_cparams(("arbitrary",)), name="moe_route")(
            h, router_w.T, router_b.reshape(N_EXPERTS, 1), tri)
    return e, gate, rank, cnt[:, 0].astype(I32)


def _dispatch_body(tt, zero_flag_ref, dest_ref, h_ref, xb_hbm, zero_ref, sem):
    @pl.when(pl.program_id(0) == 0)
    def _():
        zero_ref[...] = jnp.zeros(zero_ref.shape, F32)

        def zfill(b, carry):
            @pl.when(zero_flag_ref[b] > 0)
            def _():
                start = pl.multiple_of(b * MOE_BLOCK, MOE_BLOCK)
                cp = pltpu.make_async_copy(zero_ref, xb_hbm.at[pl.ds(start, MOE_BLOCK), :], sem)
                cp.start()
                cp.wait()
            return carry

        lax.fori_loop(0, zero_flag_ref.shape[0], zfill, 0)

    def issue(j, carry):
        for k in range(2):
            pltpu.make_async_copy(h_ref.at[pl.ds(j, 1), :],
                                  xb_hbm.at[pl.ds(dest_ref[k, j], 1), :], sem).start()
        return carry

    lax.fori_loop(0, tt, issue, 0, unroll=8)
    for k in range(2):
        pltpu.make_async_copy(h_ref, xb_hbm.at[pl.ds(0, tt), :], sem).wait()


def moe_dispatch(h, dest, zero_flag, n_slots):
    t, d = h.shape
    tt = _pick(t, (256, 128, 96, 64, 32))
    return pl.pallas_call(
        functools.partial(_dispatch_body, tt),
        grid_spec=pltpu.PrefetchScalarGridSpec(
            num_scalar_prefetch=1, grid=(t // tt,),
            in_specs=[pl.BlockSpec((2, tt), lambda i, zf: (0, i), memory_space=pltpu.SMEM),
                      pl.BlockSpec((tt, d), lambda i, zf: (i, 0))],
            out_specs=pl.BlockSpec(memory_space=pl.ANY),
            scratch_shapes=[pltpu.VMEM((MOE_BLOCK, d), F32), pltpu.SemaphoreType.DMA(())]),
        out_shape=jax.ShapeDtypeStruct((n_slots, d), F32),
        compiler_params=_cparams(("arbitrary",)), name="moe_dispatch")(zero_flag, dest, h)


def _expert_body(be_ref, nu_ref, x_ref, wg_ref, wu_ref, wd_ref, y_ref):
    used = pl.program_id(0) < nu_ref[0]

    @pl.when(used)
    def _():
        x = x_ref[...].astype(BF16)
        g = jnp.dot(x, wg_ref[0], preferred_element_type=F32)
        u = jnp.dot(x, wu_ref[0], preferred_element_type=F32)
        a = (g * _sigmoid(g) * u).astype(BF16)
        y_ref[...] = jnp.dot(a, wd_ref[0], preferred_element_type=F32)

    @pl.when(jnp.logical_not(used))
    def _():
        y_ref[...] = jnp.zeros(y_ref.shape, F32)


def moe_experts(xb, blk_expert, n_used, wg, wu, wd):
    n_slots, d = xb.shape
    ff = wg.shape[2]
    nblk = n_slots // MOE_BLOCK
    return pl.pallas_call(
        _expert_body,
        grid_spec=pltpu.PrefetchScalarGridSpec(
            num_scalar_prefetch=2, grid=(nblk,),
            in_specs=[pl.BlockSpec((MOE_BLOCK, d), lambda i, be, nu: (i, 0)),
                      pl.BlockSpec((1, d, ff), lambda i, be, nu: (be[i], 0, 0)),
                      pl.BlockSpec((1, d, ff), lambda i, be, nu: (be[i], 0, 0)),
                      pl.BlockSpec((1, ff, d), lambda i, be, nu: (be[i], 0, 0))],
            out_specs=pl.BlockSpec((MOE_BLOCK, d), lambda i, be, nu: (i, 0))),
        out_shape=jax.ShapeDtypeStruct((n_slots, d), F32),
        compiler_params=_cparams(("arbitrary",)), name="moe_experts")(
            blk_expert, n_used, xb, wg, wu, wd)


def _combine_body(tt, alpha, dest_ref, gate_ref, h_ref, g_ref, b_ref, y_hbm, ho_ref, hb_ref, buf_ref, sem):
    def issue(j, carry):
        for k in range(2):
            pltpu.make_async_copy(y_hbm.at[pl.ds(dest_ref[k, j], 1), :],
                                  buf_ref.at[k, pl.ds(j, 1), :], sem).start()
        return carry

    lax.fori_loop(0, tt, issue, 0, unroll=8)
    for k in range(2):
        pltpu.make_async_copy(y_hbm.at[pl.ds(0, tt), :], buf_ref.at[k], sem).wait()
    gate = gate_ref[...]
    ffn = gate[:, 0:1] * buf_ref[0] + gate[:, 1:2] * buf_ref[1]
    o = _layer_norm(alpha * h_ref[...] + ffn, g_ref[...], b_ref[...])
    ho_ref[...] = o
    hb_ref[...] = o.astype(BF16)


def moe_combine(yb, dest, gate_t, h, ln_g, ln_b, alpha):
    t, d = h.shape
    tt = _pick(t, (256, 128, 96, 64, 32))
    return pl.pallas_call(
        functools.partial(_combine_body, tt, alpha), grid=(t // tt,),
        in_specs=[pl.BlockSpec((2, tt), lambda i: (0, i), memory_space=pltpu.SMEM),
                  pl.BlockSpec((tt, 2), lambda i: (i, 0)),
                  pl.BlockSpec((tt, d), lambda i: (i, 0)),
                  pl.BlockSpec((1, d), lambda i: (0, 0)),
                  pl.BlockSpec((1, d), lambda i: (0, 0)),
                  pl.BlockSpec(memory_space=pl.ANY)],
        out_specs=[pl.BlockSpec((tt, d), lambda i: (i, 0)), pl.BlockSpec((tt, d), lambda i: (i, 0))],
        out_shape=[jax.ShapeDtypeStruct((t, d), F32), jax.ShapeDtypeStruct((t, d), BF16)],
        scratch_shapes=[pltpu.VMEM((2, tt, d), F32), pltpu.SemaphoreType.DMA(())],
        compiler_params=_cparams(("arbitrary",)), name="moe_combine")(
            dest, gate_t, h, ln_g.reshape(1, d), ln_b.reshape(1, d), yb)


def moe_layer(h, router_w, router_b, w_gate, w_up, w_down, ln_g, ln_b, alpha):
    t, d = h.shape
    e_idx, gate, rank, counts = moe_route(h, router_w, router_b)
    padded = (counts + MOE_BLOCK - 1) // MOE_BLOCK * MOE_BLOCK
    pends = jnp.cumsum(padded)
    pstart = pends - padded
    experts = jnp.arange(N_EXPERTS, dtype=I32)[:, None, None]
    dest = jnp.sum(jnp.where(e_idx[None] == experts, pstart[:, None, None], 0), axis=0) + rank
    nblk = -(-(2 * t) // MOE_BLOCK) + N_EXPERTS
    n_used = (pends[-1] // MOE_BLOCK).astype(I32)
    blk = jnp.arange(nblk, dtype=I32)
    blk_first_row = jnp.minimum(blk, n_used - 1) * MOE_BLOCK
    blk_expert = jnp.minimum(jnp.sum((pends[None, :] <= blk_first_row[:, None]).astype(I32), axis=1),
                             N_EXPERTS - 1)
    zero_flag = ((blk >= n_used) | (blk == pends[blk_expert] // MOE_BLOCK - 1)).astype(I32)
    xb = moe_dispatch(h, dest, zero_flag, nblk * MOE_BLOCK)
    yb = moe_experts(xb, blk_expert, n_used.reshape(1), w_gate.astype(BF16), w_up.astype(BF16),
                     w_down.astype(BF16))
    return moe_combine(yb, dest, gate.T, h, ln_g, ln_b, alpha)


def kernel(x, meta_tokens, ln_mix_g, ln_mix_b, ln_ffn_g, ln_ffn_b, conv_pw1_w, conv_pw1_b, conv_dw_w, conv_dw_b, conv_ln_g, conv_ln_b, conv_pw2_w, conv_pw2_b, mla_wdq, mla_q_norm_g, mla_wuq, mla_wdkv, mla_kv_norm_g, mla_wukv, mla_wo, hgrn_w_in, hgrn_lb_logits, hgrn_norm_g, hgrn_wo, router_w, router_b, moe_w_gate, moe_w_up, moe_w_down):
    nb, s_len, d = x.shape
    nm = meta_tokens.shape[0]
    depth = ln_mix_g.shape[0]
    alpha = float((2 * depth) ** 0.25)
    zero_bias = jnp.zeros((d,), F32)

    meta = jnp.broadcast_to(meta_tokens[None].astype(x.dtype), (nb, nm, d)).reshape(nb * nm, d)
    h = jnp.concatenate([x.reshape(nb * s_len, d), meta], axis=0)
    hb = h.astype(BF16)
    p_lb = jax.nn.softmax(hgrn_lb_logits.astype(F32), axis=0)
    lower_bounds = jnp.cumsum(p_lb, axis=0) - p_lb[0]

    for i in range(depth):
        j = i // N_MIXERS
        kind = i % N_MIXERS
        if kind == 0:
            u = mm_glu(hb, conv_pw1_w[j].astype(BF16), conv_pw1_b[j])
            mix_in = conv_ln_swish(u, conv_dw_w[j], conv_dw_b[j], conv_ln_g[j], conv_ln_b[j], nb, s_len, nm)
            w_out, b_out = conv_pw2_w[j], conv_pw2_b[j]
        elif kind == 1:
            mix_in = mla_mixer(hb, mla_wdq[j], mla_q_norm_g[j], mla_wuq[j], mla_wdkv[j], mla_kv_norm_g[j],
                               mla_wukv[j], nb, s_len, nm)
            w_out, b_out = mla_wo[j], zero_bias
        else:
            qfig = mm_plain(hb, hgrn_w_in[j].astype(BF16))
            mix_in = hgrn_mixer(qfig, lower_bounds[i], hgrn_norm_g[j], nb, s_len, nm)
            w_out, b_out = hgrn_wo[j], zero_bias
        h, hb = mm_res_ln(mix_in, h, w_out.astype(BF16), b_out, ln_mix_g[i], ln_mix_b[i], alpha)
        h, hb = moe_layer(h, router_w, router_b, moe_w_gate[i], moe_w_up[i], moe_w_down[i],
                          ln_ffn_g[i], ln_ffn_b[i], alpha)
    return h[:nb * s_len].reshape(nb, s_len, d)
```

```python
import functools

import jax
import jax.numpy as jnp
from jax import lax
from jax.experimental import pallas as pl
from jax.experimental.pallas import tpu as pltpu

F32 = jnp.float32
BF16 = jnp.bfloat16
I32 = jnp.int32

LANES = 128
CHUNK = 64
N_MIXERS = 3
CONV_WIDTH = 31
CONV_HALO = 32
CONV_LANES = 256
HEAD_DIM = 128
MLA_ROPE = 64
ROPE_THETA = 10000.0
HGRN_CHUNK = 128
N_EXPERTS = 32
N_GROUPS = 4
EXPERTS_PER_GROUP = N_EXPERTS // N_GROUPS
MOE_BLOCK = 256
COMBINE_ROWS = 16
LN_EPS = 1e-5
RMS_EPS = 1e-6
VMEM_LIMIT = 52 * 1024 * 1024


def _pick(n, cands):
    for c in cands:
        if n % c == 0:
            return c
    raise ValueError(f"no tile for {n} in {cands}")


_ROW_TILES = (768, 512, 384, 256, 192, 176, 128, 96, 64, 48, 32, 16)


def _cparams(sem, vmem=VMEM_LIMIT):
    return pltpu.CompilerParams(dimension_semantics=sem, vmem_limit_bytes=vmem)


def _layer_norm(x, g, b):
    mu = jnp.mean(x, axis=-1, keepdims=True)
    xc = x - mu
    var = jnp.mean(xc * xc, axis=-1, keepdims=True)
    return xc * lax.rsqrt(var + LN_EPS) * g + b


def _rms_norm(x, g):
    return x * lax.rsqrt(jnp.mean(x * x, axis=-1, keepdims=True) + RMS_EPS) * g


def _sigmoid(x):
    return 1.0 / (1.0 + jnp.exp(-x))


def _write_meta_rows(meta_ref, o_ref):
    o_ref[...] = jnp.zeros(o_ref.shape, o_ref.dtype)
    o_ref[0:meta_ref.shape[0], :] = meta_ref[...]


def _rows_call(body, n_rows, tm, row_ins, full_ins, outs, name):
    grid = (n_rows // tm,)
    in_specs = [pl.BlockSpec((tm, a.shape[1]), lambda i: (i, 0)) for a in row_ins]
    in_specs += [pl.BlockSpec(a.shape, lambda i, nd=a.ndim: (0,) * nd) for a in full_ins]
    out_specs = [pl.BlockSpec((tm, n), lambda i: (i, 0)) for n, _ in outs]
    out_shape = [jax.ShapeDtypeStruct((n_rows, n), dt) for n, dt in outs]
    return pl.pallas_call(
        body, grid=grid, in_specs=in_specs, out_specs=out_specs, out_shape=out_shape,
        compiler_params=_cparams(("parallel",)), name=name)(*row_ins, *full_ins)


def _mm_res_ln_body(alpha, x_ref, h_ref, w_ref, bias_ref, g_ref, b_ref, ho_ref, hb_ref):
    acc = jnp.dot(x_ref[...], w_ref[...], preferred_element_type=F32)
    y = alpha * h_ref[...] + (acc + bias_ref[...])
    o = _layer_norm(y, g_ref[...], b_ref[...])
    ho_ref[...] = o
    hb_ref[...] = o.astype(BF16)


def mm_res_ln(x_bf, h, w_bf, bias, g, b, alpha):
    t, d = h.shape
    tm = _pick(t, (256, 192, 176, 128, 96, 64, 48, 32, 16))
    return _rows_call(functools.partial(_mm_res_ln_body, alpha), t, tm, [x_bf, h],
                      [w_bf, bias.reshape(1, d), g.reshape(1, d), b.reshape(1, d)],
                      [(d, F32), (d, BF16)], "mm_res_ln")


def _glu_body(x_ref, wa_ref, wg_ref, ba_ref, bg_ref, u_ref):
    x = x_ref[...]
    a = jnp.dot(x, wa_ref[...], preferred_element_type=F32) + ba_ref[...]
    g = jnp.dot(x, wg_ref[...], preferred_element_type=F32) + bg_ref[...]
    u_ref[...] = a * _sigmoid(g)


def mm_glu(x_bf, w_bf, bias):
    t, k = x_bf.shape
    d = w_bf.shape[1] // 2
    tm = _pick(t, _ROW_TILES)
    tn = min(d, 1024)
    nj = d // tn
    bias2 = bias.reshape(1, 2 * d)
    return pl.pallas_call(
        _glu_body, grid=(nj, t // tm),
        in_specs=[pl.BlockSpec((tm, k), lambda j, i: (i, 0)),
                  pl.BlockSpec((k, tn), lambda j, i: (0, j)),
                  pl.BlockSpec((k, tn), lambda j, i: (0, j + nj)),
                  pl.BlockSpec((1, tn), lambda j, i: (0, j)),
                  pl.BlockSpec((1, tn), lambda j, i: (0, j + nj))],
        out_specs=pl.BlockSpec((tm, tn), lambda j, i: (i, j)),
        out_shape=jax.ShapeDtypeStruct((t, d), F32),
        compiler_params=_cparams(("parallel", "parallel")), name="mm_glu")(x_bf, w_bf, w_bf, bias2, bias2)


def _mm_plain_body(x_ref, w_ref, o_ref):
    o_ref[...] = jnp.dot(x_ref[...], w_ref[...], preferred_element_type=F32).astype(o_ref.dtype)


def mm_plain(x_bf, w_bf, out_dtype=F32):
    t, k = x_bf.shape
    n = w_bf.shape[1]
    tm = _pick(t, _ROW_TILES)
    tn = min(n, 1024)
    return pl.pallas_call(
        _mm_plain_body, grid=(n // tn, t // tm),
        in_specs=[pl.BlockSpec((tm, k), lambda j, i: (i, 0)),
                  pl.BlockSpec((k, tn), lambda j, i: (0, j))],
        out_specs=pl.BlockSpec((tm, tn), lambda j, i: (i, j)),
        out_shape=jax.ShapeDtypeStruct((t, n), out_dtype),
        compiler_params=_cparams(("parallel", "parallel")), name="mm_plain")(x_bf, w_bf)


def _conv_rows(ext_ref, w_ref, y_ref, sh_ref, rows, rc, lc):
    d = y_ref.shape[1]
    shift = CONV_HALO - (CONV_WIDTH - 1)
    sub = 8
    n_sh = sh_ref.shape[1]

    def lane_body(c, carry):
        l0 = pl.multiple_of(c * lc, lc)
        lanes = pl.ds(l0, lc)
        wv = w_ref[:, lanes]
        for b in range(1, sub):
            sh_ref[b - 1] = ext_ref[pl.ds(b, n_sh), lanes]
        for r in range(rows // rc):
            acc = None
            for k in range(CONV_WIDTH):
                b = (shift + k) % sub
                a = r * rc + (shift + k) - b
                src = ext_ref[pl.ds(a, rc), lanes] if b == 0 else sh_ref[b - 1, pl.ds(a, rc), :]
                term = src * wv[k:k + 1, :]
                acc = term if acc is None else acc + term
            y_ref[pl.ds(r * rc, rc), lanes] = acc
        return carry

    lax.fori_loop(0, d // lc, lane_body, 0)


def _conv_epilogue(y_ref, dwb_ref, g_ref, b_ref, o_ref):
    rows = y_ref.shape[0]
    step = 16

    def body(c, carry):
        sl = pl.ds(pl.multiple_of(c * step, step), step)
        z = _layer_norm(y_ref[sl, :] + dwb_ref[...], g_ref[...], b_ref[...])
        o_ref[sl, :] = (z * _sigmoid(z)).astype(BF16)
        return carry

    lax.fori_loop(0, rows // step, body, 0, unroll=min(4, rows // step))


def _conv_main_body(tr, nst, n_real, cur_ref, prev_ref, meta_ref, w_ref, dwb_ref, g_ref, b_ref, vmeta_ref, o_ref,
                    ext_ref, y_ref, sh_ref):
    step = pl.program_id(0)
    nm = meta_ref.shape[0]

    @pl.when(step < n_real)
    def _():
        @pl.when(step % nst == 0)
        def _():
            ext_ref[0:CONV_HALO - nm, :] = jnp.zeros((CONV_HALO - nm, ext_ref.shape[1]), F32)
            ext_ref[CONV_HALO - nm:CONV_HALO, :] = meta_ref[...]

        @pl.when(step % nst > 0)
        def _():
            ext_ref[0:CONV_HALO, :] = prev_ref[...]

        ext_ref[CONV_HALO:CONV_HALO + tr, :] = cur_ref[...]
        _conv_rows(ext_ref, w_ref, y_ref, sh_ref, tr, 64, CONV_LANES)
        _conv_epilogue(y_ref, dwb_ref, g_ref, b_ref, o_ref)

    @pl.when(step == n_real)
    def _():
        _write_meta_rows(vmeta_ref, o_ref)


def _conv_meta_body(meta_ref, w_ref, dwb_ref, g_ref, b_ref, o_ref, ext_ref, y_ref, sh_ref):
    nm = meta_ref.shape[0]
    ext_ref[0:CONV_HALO, :] = jnp.zeros((CONV_HALO, ext_ref.shape[1]), F32)
    ext_ref[CONV_HALO:CONV_HALO + nm, :] = meta_ref[...]
    _conv_rows(ext_ref, w_ref, y_ref, sh_ref, nm, nm, CONV_LANES)
    _conv_epilogue(y_ref, dwb_ref, g_ref, b_ref, o_ref)


def conv_ln_swish(u, dw_w, dw_b, ln_g, ln_b, nb, s_len, nm):
    t, d = u.shape
    tr_rows = nb * s_len
    tr = _pick(s_len, (512, 256))
    w_pad = jnp.concatenate([dw_w, jnp.zeros((CONV_HALO - CONV_WIDTH, d), F32)], axis=0)
    vecs = [dw_b.reshape(1, d), ln_g.reshape(1, d), ln_b.reshape(1, d)]
    meta_blk = tr_rows // nm
    out_meta = pl.pallas_call(
        _conv_meta_body, grid=(nb,),
        in_specs=[pl.BlockSpec((nm, d), lambda b: (meta_blk + b, 0)),
                  pl.BlockSpec((CONV_HALO, d), lambda b: (0, 0))]
        + [pl.BlockSpec((1, d), lambda b: (0, 0))] * 3,
        out_specs=pl.BlockSpec((nm, d), lambda b: (b, 0)),
        out_shape=jax.ShapeDtypeStruct((nb * nm, d), BF16),
        scratch_shapes=[pltpu.VMEM((CONV_HALO + nm, d), F32), pltpu.VMEM((nm, d), F32),
                        pltpu.VMEM((7, CONV_HALO + nm - 8, CONV_LANES), F32)],
        compiler_params=_cparams(("parallel",)), name="conv_meta")(u, w_pad, *vecs)
    nst = s_len // tr
    n_real = nb * nst
    halo_per_tile = tr // CONV_HALO
    assert nb * nm <= tr

    def real(g):
        return jnp.minimum(g, n_real - 1)

    return pl.pallas_call(
        functools.partial(_conv_main_body, tr, nst, n_real), grid=(n_real + 1,),
        in_specs=[pl.BlockSpec((tr, d), lambda g: (real(g), 0)),
                  pl.BlockSpec((CONV_HALO, d), lambda g: (jnp.maximum(real(g) * halo_per_tile - 1, 0), 0)),
                  pl.BlockSpec((nm, d), lambda g: (meta_blk + real(g) // nst, 0)),
                  pl.BlockSpec((CONV_HALO, d), lambda g: (0, 0))]
        + [pl.BlockSpec((1, d), lambda g: (0, 0))] * 3
        + [pl.BlockSpec((nb * nm, d), lambda g: (0, 0))],
        out_specs=pl.BlockSpec((tr, d), lambda g: (g, 0)),
        out_shape=jax.ShapeDtypeStruct((t, d), BF16),
        scratch_shapes=[pltpu.VMEM((CONV_HALO + tr, d), F32), pltpu.VMEM((tr, d), F32),
                        pltpu.VMEM((7, CONV_HALO + tr - 8, CONV_LANES), F32)],
        compiler_params=_cparams(("arbitrary",)), name="conv_main")(u, u, u, w_pad, *vecs, out_meta)


def _mla_proj_body(ql, kvl, hd, scale, x_ref, cos_ref, sin_ref, w1_ref, qg_ref, kvg_ref, wq_ref, wkv_ref,
                   qn_ref, qr_ref, kn_ref, v_ref, kr_ref):
    a = jnp.dot(x_ref[...], w1_ref[...], preferred_element_type=F32)
    cos = cos_ref[...]
    sin = sin_ref[...]
    cq = _rms_norm(a[:, :ql], qg_ref[...]).astype(BF16)
    ckv = _rms_norm(a[:, ql:ql + kvl], kvg_ref[...]).astype(BF16)
    r0 = ql + kvl
    kr_ref[...] = (a[:, r0:r0 + LANES] * cos + a[:, r0 + LANES:r0 + 2 * LANES] * sin).astype(BF16)
    qa = jnp.dot(cq, wq_ref[...], preferred_element_type=F32)
    nh = hd // LANES
    cos_t = jnp.tile(cos, (1, nh))
    sin_t = jnp.tile(sin, (1, nh))
    qn_ref[...] = (qa[:, :hd] * scale).astype(BF16)
    qr_ref[...] = ((qa[:, hd:2 * hd] * cos_t + qa[:, 2 * hd:] * sin_t) * scale).astype(BF16)
    kv = jnp.dot(ckv, wkv_ref[...], preferred_element_type=F32)
    kn_ref[...] = kv[:, :hd].astype(BF16)
    v_ref[...] = kv[:, hd:].astype(BF16)


def _attn_body(tq, nb, qn_ref, qr_ref, kn_ref, kr_ref, v_ref, knm_ref, krm_ref, vm_ref, ometa_ref, o_ref, kf_ref):
    @pl.when(pl.program_id(0) == nb)
    def _():
        _write_meta_rows(ometa_ref, o_ref)

    @pl.when(pl.program_id(0) < nb)
    def _():
        _attn_tiles(tq, qn_ref, qr_ref, kn_ref, kr_ref, v_ref, knm_ref, krm_ref, vm_ref, o_ref, kf_ref)


def _attn_tiles(tq, qn_ref, qr_ref, kn_ref, kr_ref, v_ref, knm_ref, krm_ref, vm_ref, o_ref, kf_ref):
    s_len = qn_ref.shape[0]
    nt = (((1,), (1,)), ((), ()))
    kf_ref[:, :LANES] = kn_ref[...]
    kf_ref[:, LANES:] = kr_ref[...]
    km = jnp.concatenate([knm_ref[...], krm_ref[...]], axis=-1)
    vm = vm_ref[...]
    row_c = lax.broadcasted_iota(I32, (tq, tq), 0) // CHUNK
    col_c = lax.broadcasted_iota(I32, (tq, tq), 1) // CHUNK
    visible = col_c <= row_c
    for i in range(s_len // tq):
        r0 = i * tq
        q = jnp.concatenate([qn_ref[r0:r0 + tq, :], qr_ref[r0:r0 + tq, :]], axis=-1)
        s_m = lax.dot_general(q, km, nt, preferred_element_type=F32)
        s_d = lax.dot_general(q, kf_ref[r0:r0 + tq, :], nt, preferred_element_type=F32)
        s_d = jnp.where(visible, s_d, -jnp.inf)
        m = jnp.maximum(jnp.max(s_m, axis=-1, keepdims=True), jnp.max(s_d, axis=-1, keepdims=True))
        if i > 0:
            s_p = lax.dot_general(q, kf_ref[0:r0, :], nt, preferred_element_type=F32)
            m = jnp.maximum(m, jnp.max(s_p, axis=-1, keepdims=True))
        p_m = jnp.exp(s_m - m)
        p_d = jnp.exp(s_d - m)
        l = jnp.sum(p_m, axis=-1, keepdims=True) + jnp.sum(p_d, axis=-1, keepdims=True)
        acc = jnp.dot(p_m.astype(BF16), vm, preferred_element_type=F32)
        acc = acc + jnp.dot(p_d.astype(BF16), v_ref[r0:r0 + tq, :], preferred_element_type=F32)
        if i > 0:
            p_p = jnp.exp(s_p - m)
            l = l + jnp.sum(p_p, axis=-1, keepdims=True)
            acc = acc + jnp.dot(p_p.astype(BF16), v_ref[0:r0, :], preferred_element_type=F32)
        o_ref[r0:r0 + tq, :] = (acc / l).astype(BF16)


def _attn_meta_body(qn_ref, qr_ref, kn_ref, kr_ref, v_ref, o_ref):
    nt = (((1,), (1,)), ((), ()))
    q = jnp.concatenate([qn_ref[...], qr_ref[...]], axis=-1)
    k = jnp.concatenate([kn_ref[...], kr_ref[...]], axis=-1)
    s = lax.dot_general(q, k, nt, preferred_element_type=F32)
    p = jnp.exp(s - jnp.max(s, axis=-1, keepdims=True))
    l = jnp.sum(p, axis=-1, keepdims=True)
    o_ref[...] = (jnp.dot(p.astype(BF16), v_ref[...], preferred_element_type=F32) / l).astype(BF16)


def _rope_rows(n_pos):
    inv = ROPE_THETA ** (-jnp.arange(0, MLA_ROPE, 2, dtype=F32) / MLA_ROPE)
    ang = jnp.arange(n_pos, dtype=F32)[:, None] * inv[None, :]
    pad = jnp.zeros((n_pos, LANES - MLA_ROPE), F32)
    cos = jnp.concatenate([jnp.cos(ang), jnp.cos(ang), pad], axis=1)
    sin = jnp.concatenate([jnp.sin(ang), jnp.sin(ang), pad], axis=1)
    return cos, sin


def _pad_rope_cols(w):
    half = MLA_ROPE // 2
    z = jnp.zeros((w.shape[0], LANES - MLA_ROPE), w.dtype)
    rot = jnp.concatenate([-w[:, half:], w[:, :half]], axis=1)
    return jnp.concatenate([w, z], axis=1), jnp.concatenate([rot, z], axis=1)


def mla_mixer(hb, wdq, q_norm_g, wuq, wdkv, kv_norm_g, wukv, nb, s_len, nm):
    t, d = hb.shape
    ql = wdq.shape[1]
    kvl = kv_norm_g.shape[0]
    nh = wuq.shape[1] // (HEAD_DIM + MLA_ROPE)
    hd = nh * HEAD_DIM
    tr_rows = nb * s_len
    scale = float((HEAD_DIM + MLA_ROPE) ** -0.5)
    kr_w, kr_rot = _pad_rope_cols(wdkv[:, kvl:])
    w1 = jnp.concatenate([wdq, wdkv[:, :kvl], kr_w, kr_rot], axis=1).astype(BF16)
    wuq3 = wuq.reshape(ql, nh, HEAD_DIM + MLA_ROPE)
    q_rope = wuq3[:, :, HEAD_DIM:]
    half = MLA_ROPE // 2
    zq = jnp.zeros((ql, nh, LANES - MLA_ROPE), F32)
    q_rope_p = jnp.concatenate([q_rope, zq], axis=2).reshape(ql, hd)
    q_rot_p = jnp.concatenate([-q_rope[:, :, half:], q_rope[:, :, :half], zq], axis=2).reshape(ql, hd)
    wq = jnp.concatenate([wuq3[:, :, :HEAD_DIM].reshape(ql, hd), q_rope_p, q_rot_p], axis=1).astype(BF16)
    wukv3 = wukv.reshape(kvl, nh, 2 * HEAD_DIM)
    wkv = jnp.concatenate([wukv3[:, :, :HEAD_DIM].reshape(kvl, hd),
                           wukv3[:, :, HEAD_DIM:].reshape(kvl, hd)], axis=1).astype(BF16)
    cos_p, sin_p = _rope_rows(nm + s_len)
    cos_rows = jnp.concatenate([jnp.tile(cos_p[nm:], (nb, 1)), jnp.tile(cos_p[:nm], (nb, 1))], axis=0)
    sin_rows = jnp.concatenate([jnp.tile(sin_p[nm:], (nb, 1)), jnp.tile(sin_p[:nm], (nb, 1))], axis=0)

    tm = _pick(t, (128, 96, 64, 48, 32, 16))
    qn, qr, kn, v, kr = _rows_call(
        functools.partial(_mla_proj_body, ql, kvl, hd, scale), t, tm, [hb, cos_rows, sin_rows],
        [w1, q_norm_g.reshape(1, ql), kv_norm_g.reshape(1, kvl), wq, wkv],
        [(hd, BF16), (hd, BF16), (hd, BF16), (hd, BF16), (LANES, BF16)], "mla_proj")

    meta_blk = tr_rows // nm
    mspec = lambda: pl.BlockSpec((nm, LANES), lambda b, h: (meta_blk + b, h))
    mspec0 = lambda: pl.BlockSpec((nm, LANES), lambda b, h: (meta_blk + b, 0))
    o_meta = pl.pallas_call(
        _attn_meta_body, grid=(nb, nh),
        in_specs=[mspec(), mspec(), mspec(), mspec0(), mspec()],
        out_specs=pl.BlockSpec((nm, LANES), lambda b, h: (b, h)),
        out_shape=jax.ShapeDtypeStruct((nb * nm, hd), BF16),
        compiler_params=_cparams(("parallel", "parallel")), name="attn_meta")(qn, qr, kn, kr, v)

    tq = 256
    assert nb * nm <= s_len

    def real(b):
        return jnp.minimum(b, nb - 1)

    kspec = lambda: pl.BlockSpec((s_len, LANES), lambda b, h: (real(b), h))
    kspec0 = lambda: pl.BlockSpec((s_len, LANES), lambda b, h: (real(b), 0))
    m2 = lambda: pl.BlockSpec((nm, LANES), lambda b, h: (meta_blk + real(b), h))
    m20 = lambda: pl.BlockSpec((nm, LANES), lambda b, h: (meta_blk + real(b), 0))
    return pl.pallas_call(
        functools.partial(_attn_body, tq, nb), grid=(nb + 1, nh),
        in_specs=[kspec(), kspec(), kspec(), kspec0(), kspec(), m2(), m20(), m2(),
                  pl.BlockSpec((nb * nm, LANES), lambda b, h: (0, h))],
        out_specs=pl.BlockSpec((s_len, LANES), lambda b, h: (b, h)),
        out_shape=jax.ShapeDtypeStruct((t, hd), BF16),
        scratch_shapes=[pltpu.VMEM((s_len, 2 * LANES), BF16)],
        compiler_params=_cparams(("arbitrary", "arbitrary")), name="attn_main")(
            qn, qr, kn, kr, v, kn, kr, v, o_meta)


def _hgrn_gates(fz, lb):
    return lb + (1.0 - lb) * _sigmoid(fz), (1.0 - lb) * _sigmoid(-fz)


def _hgrn_level_index(c):
    t = lax.broadcasted_iota(I32, (c, c), 0)
    s = lax.broadcasted_iota(I32, (c, c), 1)
    lvl = 31 - lax.clz(t ^ s)
    return jnp.where(s > t, -2, lvl)


def _hgrn_block(q, f, k, iv, st, lvl):
    c = q.shape[0]
    nt = (((1,), (1,)), ((), ()))
    tn = (((0,), (0,)), ((), ()))
    row = lax.broadcasted_iota(I32, (c, HEAD_DIM), 0)
    scores = jnp.where(lvl == -1, lax.dot_general(q.astype(BF16), k.astype(BF16), nt,
                                                  preferred_element_type=F32), 0.0)
    qa = q * f
    kb = k
    tot = f
    h, idx = 1, 0
    while h < c:
        prod = lax.dot_general(qa.astype(BF16), kb.astype(BF16), nt, preferred_element_type=F32)
        scores = jnp.where(lvl == idx, prod, scores)
        right = (row & h) != 0
        left_tot = pltpu.roll(tot, h, 0)
        right_tot = pltpu.roll(tot, c - h, 0)
        qa = qa * jnp.where(right, left_tot, 1.0)
        kb = kb * jnp.where(right, 1.0, right_tot)
        tot = tot * jnp.where(right, left_tot, right_tot)
        h, idx = 2 * h, idx + 1
    ib = iv.astype(BF16)
    o = jnp.dot(scores.astype(BF16), ib, preferred_element_type=F32)
    o = o + lax.dot_general(qa.astype(BF16), st.astype(BF16), nt, preferred_element_type=F32)
    st_new = st * tot[0:1, :] + lax.dot_general(ib, kb.astype(BF16), tn, preferred_element_type=F32)
    return o, st_new


def _hgrn_out(o, gate, ng):
    o = o * lax.rsqrt(jnp.mean(o * o, axis=-1, keepdims=True) + RMS_EPS) * ng
    return (o * (gate * _sigmoid(gate))).astype(BF16)


def _hgrn_meta_body(gh, q_ref, fz_ref, i_ref, g_ref, lb_ref, ng_ref, o_ref, st_ref):
    nm = q_ref.shape[0]
    c = HGRN_CHUNK
    lvl = _hgrn_level_index(c)
    zeros = jnp.zeros((c - nm, HEAD_DIM), F32)
    for g in range(gh):
        sl = slice(g * HEAD_DIM, (g + 1) * HEAD_DIM)
        f, k = _hgrn_gates(fz_ref[:, sl], lb_ref[:, sl])
        o, st = _hgrn_block(jnp.concatenate([zeros, q_ref[:, sl]], axis=0),
                            jnp.concatenate([zeros + 1.0, f], axis=0),
                            jnp.concatenate([zeros, k], axis=0),
                            jnp.concatenate([zeros, i_ref[:, sl]], axis=0),
                            jnp.zeros((HEAD_DIM, HEAD_DIM), F32), lvl)
        st_ref[0, g] = st
        o_ref[:, sl] = _hgrn_out(o[c - nm:], g_ref[:, sl], ng_ref[...])


def _hgrn_main_body(gh, ts, nst, n_real, q_ref, fz_ref, i_ref, g_ref, lb_ref, ng_ref, st0_ref, ometa_ref, o_ref,
                    st_ref):
    c = HGRN_CHUNK
    step = pl.program_id(1)

    @pl.when(step == n_real)
    def _():
        _write_meta_rows(ometa_ref, o_ref)

    @pl.when(step < n_real)
    def _():
        lvl = _hgrn_level_index(c)

        @pl.when(step % nst == 0)
        def _():
            st_ref[...] = st0_ref[0]

        def chunk_body(cidx, carry):
            rows = pl.ds(pl.multiple_of(cidx * c, c), c)
            for g in range(gh):
                sl = slice(g * HEAD_DIM, (g + 1) * HEAD_DIM)
                f, k = _hgrn_gates(fz_ref[rows, sl], lb_ref[:, sl])
                o, st = _hgrn_block(q_ref[rows, sl], f, k, i_ref[rows, sl], st_ref[g], lvl)
                st_ref[g] = st
                o_ref[rows, sl] = _hgrn_out(o, g_ref[rows, sl], ng_ref[...])
            return carry

        lax.fori_loop(0, ts // c, chunk_body, 0)


def hgrn_mixer(qfig, lb, norm_g, nb, s_len, nm):
    t, d4 = qfig.shape
    d = d4 // 4
    nh = d // HEAD_DIM
    gh = 4 if nh % 4 == 0 else 1
    gw = gh * HEAD_DIM
    ng_blocks = d // gw
    tr_rows = nb * s_len
    meta_blk = tr_rows // nm
    lb2 = lb.reshape(1, d)
    ng2 = norm_g.reshape(1, HEAD_DIM)

    def mspec(sec):
        return pl.BlockSpec((nm, gw), lambda b, h: (meta_blk + b, sec * ng_blocks + h))

    o_meta, st0 = pl.pallas_call(
        functools.partial(_hgrn_meta_body, gh), grid=(nb, ng_blocks),
        in_specs=[mspec(0), mspec(1), mspec(2), mspec(3),
                  pl.BlockSpec((1, gw), lambda b, h: (0, h)),
                  pl.BlockSpec((1, HEAD_DIM), lambda b, h: (0, 0))],
        out_specs=[pl.BlockSpec((nm, gw), lambda b, h: (b, h)),
                   pl.BlockSpec((1, gh, HEAD_DIM, HEAD_DIM), lambda b, h: (b, h, 0, 0))],
        out_shape=[jax.ShapeDtypeStruct((nb * nm, d), BF16),
                   jax.ShapeDtypeStruct((nb, nh, HEAD_DIM, HEAD_DIM), F32)],
        compiler_params=_cparams(("parallel", "parallel")), name="hgrn_meta")(qfig, qfig, qfig, qfig, lb2, ng2)

    ts = _pick(s_len, (512, 256))
    nst = s_len // ts
    n_real = nb * nst
    assert nb * nm <= ts

    def real(g):
        return jnp.minimum(g, n_real - 1)

    def rspec(sec):
        return pl.BlockSpec((ts, gw), lambda h, g: (real(g), sec * ng_blocks + h))

    return pl.pallas_call(
        functools.partial(_hgrn_main_body, gh, ts, nst, n_real), grid=(ng_blocks, n_real + 1),
        in_specs=[rspec(0), rspec(1), rspec(2), rspec(3),
                  pl.BlockSpec((1, gw), lambda h, g: (0, h)),
                  pl.BlockSpec((1, HEAD_DIM), lambda h, g: (0, 0)),
                  pl.BlockSpec((1, gh, HEAD_DIM, HEAD_DIM), lambda h, g: (real(g) // nst, h, 0, 0)),
                  pl.BlockSpec((nb * nm, gw), lambda h, g: (0, h))],
        out_specs=pl.BlockSpec((ts, gw), lambda h, g: (g, h)),
        out_shape=jax.ShapeDtypeStruct((t, d), BF16),
        scratch_shapes=[pltpu.VMEM((gh, HEAD_DIM, HEAD_DIM), F32)],
        compiler_params=_cparams(("parallel", "arbitrary")), name="hgrn_main")(
            qfig, qfig, qfig, qfig, lb2, ng2, st0, o_meta)


def _first_index_of_max(vals, idx, n, axis):
    mx = jnp.max(vals, axis=axis, keepdims=True)
    first = jnp.min(jnp.where(vals == mx, idx, n), axis=axis, keepdims=True)
    return mx, first


def _router_body(tm, h_ref, rw_ref, rb_ref, tri_ref, e_ref, gate_ref, rank_ref, cnt_ref, base_ref):
    @pl.when(pl.program_id(0) == 0)
    def _():
        base_ref[...] = jnp.zeros(base_ref.shape, F32)

    nt = (((1,), (1,)), ((), ()))
    logits = lax.dot_general(rw_ref[...], h_ref[...], nt, precision=lax.Precision.HIGHEST,
                             preferred_element_type=F32)
    scores = _sigmoid(logits)
    sel = scores + rb_ref[...]
    g, epg = N_GROUPS, EXPERTS_PER_GROUP
    sel3 = sel.reshape(g, epg, tm)
    sc3 = scores.reshape(g, epg, tm)
    idx3 = lax.broadcasted_iota(I32, (g, epg, tm), 1)
    m1, i1 = _first_index_of_max(sel3, idx3, epg, 1)
    rest = jnp.where(idx3 == i1, -jnp.inf, sel3)
    m2, i2 = _first_index_of_max(rest, idx3, epg, 1)
    gidx = lax.broadcasted_iota(I32, (g, 1, tm), 0)
    _, gtop3 = _first_index_of_max(m1 + m2, gidx, g, 0)
    pick = gidx == gtop3
    gtop = gtop3[0]
    l1 = jnp.sum(jnp.where(pick, i1, 0), axis=0)
    l2 = jnp.sum(jnp.where(pick, i2, 0), axis=0)
    sc_in = jnp.sum(jnp.where(pick, sc3, 0.0), axis=0)
    idx2 = lax.broadcasted_iota(I32, (epg, tm), 0)
    s1 = jnp.sum(jnp.where(idx2 == l1, sc_in, 0.0), axis=0, keepdims=True)
    s2 = jnp.sum(jnp.where(idx2 == l2, sc_in, 0.0), axis=0, keepdims=True)
    e1 = gtop * epg + l1
    e2 = gtop * epg + l2
    e_ref[0:1, :] = e1
    e_ref[1:2, :] = e2
    den = s1 + s2
    gate_ref[0:1, :] = s1 / den
    gate_ref[1:2, :] = s2 / den
    eidx = lax.broadcasted_iota(I32, (N_EXPERTS, tm), 0)
    oh1 = (eidx == e1).astype(F32)
    oh2 = (eidx == e2).astype(F32)
    oh = jnp.concatenate([oh1, oh2], axis=0).astype(BF16)
    pre = jnp.dot(oh, tri_ref[...], preferred_element_type=F32)
    base = base_ref[:, 0:1]
    tot1 = jnp.sum(oh1, axis=1, keepdims=True)
    tot2 = jnp.sum(oh2, axis=1, keepdims=True)
    r1 = jnp.sum(oh1 * (base + pre[:N_EXPERTS]), axis=0, keepdims=True)
    r2 = jnp.sum(oh2 * (base + tot1 + pre[N_EXPERTS:]), axis=0, keepdims=True)
    rank_ref[0:1, :] = r1.astype(I32)
    rank_ref[1:2, :] = r2.astype(I32)
    new_base = jnp.broadcast_to(base + tot1 + tot2, base_ref.shape)
    base_ref[...] = new_base
    cnt_ref[...] = new_base


def moe_route(h, router_w, router_b):
    t, d = h.shape
    tm = _pick(t, (256, 128, 96, 64, 32))
    tri = (lax.broadcasted_iota(I32, (tm, tm), 0) < lax.broadcasted_iota(I32, (tm, tm), 1)).astype(BF16)
    e, gate, rank, cnt = pl.pallas_call(
        functools.partial(_router_body, tm), grid=(t // tm,),
        in_specs=[pl.BlockSpec((tm, d), lambda i: (i, 0)),
                  pl.BlockSpec((N_EXPERTS, d), lambda i: (0, 0)),
                  pl.BlockSpec((N_EXPERTS, 1), lambda i: (0, 0)),
                  pl.BlockSpec((tm, tm), lambda i: (0, 0))],
        out_specs=[pl.BlockSpec((2, tm), lambda i: (0, i)),
                   pl.BlockSpec((2, tm), lambda i: (0, i)),
                   pl.BlockSpec((2, tm), lambda i: (0, i)),
                   pl.BlockSpec((N_EXPERTS, LANES), lambda i: (0, 0))],
        out_shape=[jax.ShapeDtypeStruct((2, t), I32), jax.ShapeDtypeStruct((2, t), F32),
                   jax.ShapeDtypeStruct((2, t), I32), jax.ShapeDtypeStruct((N_EXPERTS, LANES), F32)],
        scratch_shapes=[pltpu.VMEM((N_EXPERTS, LANES), F32)],
        compiler_params=_cparams(("arbitrary",)), name="moe_route")(
            h, router_w.T, router_b.reshape(N_EXPERTS, 1), tri)
    return e, gate, rank, cnt[:, 0].astype(I32)


def _dispatch_body(tt, zero_flag_ref, dest_ref, h_ref, xb_hbm, zero_ref, sem):
    @pl.when(pl.program_id(0) == 0)
    def _():
        zero_ref[...] = jnp.zeros(zero_ref.shape, F32)

        def zfill(b, carry):
            @pl.when(zero_flag_ref[b] > 0)
            def _():
                start = pl.multiple_of(b * MOE_BLOCK, MOE_BLOCK)
                cp = pltpu.make_async_copy(zero_ref, xb_hbm.at[pl.ds(start, MOE_BLOCK), :], sem)
                cp.start()
                cp.wait()
            return carry

        lax.fori_loop(0, zero_flag_ref.shape[0], zfill, 0)

    def issue(j, carry):
        for k in range(2):
            pltpu.make_async_copy(h_ref.at[pl.ds(j, 1), :],
                                  xb_hbm.at[pl.ds(dest_ref[k, j], 1), :], sem).start()
        return carry

    lax.fori_loop(0, tt, issue, 0, unroll=8)
    for k in range(2):
        pltpu.make_async_copy(h_ref, xb_hbm.at[pl.ds(0, tt), :], sem).wait()


def moe_dispatch(h, dest, zero_flag, n_slots):
    t, d = h.shape
    tt = _pick(t, (256, 128, 96, 64, 32))
    return pl.pallas_call(
        functools.partial(_dispatch_body, tt),
        grid_spec=pltpu.PrefetchScalarGridSpec(
            num_scalar_prefetch=1, grid=(t // tt,),
            in_specs=[pl.BlockSpec((2, tt), lambda i, zf: (0, i), memory_space=pltpu.SMEM),
                      pl.BlockSpec((tt, d), lambda i, zf: (i, 0))],
            out_specs=pl.BlockSpec(memory_space=pl.ANY),
            scratch_shapes=[pltpu.VMEM((MOE_BLOCK, d), F32), pltpu.SemaphoreType.DMA(())]),
        out_shape=jax.ShapeDtypeStruct((n_slots, d), F32),
        compiler_params=_cparams(("arbitrary",)), name="moe_dispatch")(zero_flag, dest, h)


def _expert_body(be_ref, nu_ref, x_ref, wg_ref, wu_ref, wd_ref, y_ref, wgb_ref, wub_ref, wdb_ref):
    i = pl.program_id(0)
    used = i < nu_ref[0]

    @pl.when(jnp.logical_or(i == 0, be_ref[i] != be_ref[jnp.maximum(i - 1, 0)]))
    def _():
        wgb_ref[...] = wg_ref[0].astype(BF16)
        wub_ref[...] = wu_ref[0].astype(BF16)
        wdb_ref[...] = wd_ref[0].astype(BF16)

    @pl.when(used)
    def _():
        x = x_ref[...].astype(BF16)
        g = jnp.dot(x, wgb_ref[...], preferred_element_type=F32)
        u = jnp.dot(x, wub_ref[...], preferred_element_type=F32)
        a = (g * _sigmoid(g) * u).astype(BF16)
        y_ref[...] = jnp.dot(a, wdb_ref[...], preferred_element_type=F32)

    @pl.when(jnp.logical_not(used))
    def _():
        y_ref[...] = jnp.zeros(y_ref.shape, F32)


def moe_experts(xb, blk_expert, n_used, wg, wu, wd):
    n_slots, d = xb.shape
    ff = wg.shape[2]
    nblk = n_slots // MOE_BLOCK
    return pl.pallas_call(
        _expert_body,
        grid_spec=pltpu.PrefetchScalarGridSpec(
            num_scalar_prefetch=2, grid=(nblk,),
            in_specs=[pl.BlockSpec((MOE_BLOCK, d), lambda i, be, nu: (i, 0)),
                      pl.BlockSpec((1, d, ff), lambda i, be, nu: (be[i], 0, 0)),
                      pl.BlockSpec((1, d, ff), lambda i, be, nu: (be[i], 0, 0)),
                      pl.BlockSpec((1, ff, d), lambda i, be, nu: (be[i], 0, 0))],
            out_specs=pl.BlockSpec((MOE_BLOCK, d), lambda i, be, nu: (i, 0)),
            scratch_shapes=[pltpu.VMEM((d, ff), BF16), pltpu.VMEM((d, ff), BF16), pltpu.VMEM((ff, d), BF16)]),
        out_shape=jax.ShapeDtypeStruct((n_slots, d), F32),
        compiler_params=_cparams(("arbitrary",)), name="moe_experts")(
            blk_expert, n_used, xb, wg, wu, wd)


def _combine_body(tt, alpha, n_tiles, dest_ref, dnext_ref, gate_ref, h_ref, g_ref, b_ref, y_hbm, ho_ref, hb_ref,
                  buf_ref, sem):
    i = pl.program_id(0)
    slot = i % 2
    nslot = 1 - slot

    def gather(idx_ref, j, to_slot):
        for k in range(2):
            pltpu.make_async_copy(y_hbm.at[pl.ds(idx_ref[k, j], 1), :],
                                  buf_ref.at[to_slot, k, pl.ds(j, 1), :], sem.at[to_slot]).start()

    def wait_slot(s):
        for k in range(2):
            pltpu.make_async_copy(y_hbm.at[pl.ds(0, tt), :], buf_ref.at[s, k], sem.at[s]).wait()

    @pl.when(i == 0)
    def _():
        def first(j, carry):
            gather(dest_ref, j, 0)
            return carry
        lax.fori_loop(0, tt, first, 0, unroll=8)

    wait_slot(slot)

    def ahead(j, carry):
        gather(dnext_ref, j, nslot)
        return carry

    lax.fori_loop(0, tt, ahead, 0, unroll=8)

    def rows_body(c, carry):
        r0 = pl.multiple_of(c * COMBINE_ROWS, COMBINE_ROWS)
        rows = pl.ds(r0, COMBINE_ROWS)
        gate = gate_ref[rows, :]
        ffn = gate[:, 0:1] * buf_ref[slot, 0, rows, :] + gate[:, 1:2] * buf_ref[slot, 1, rows, :]
        o = _layer_norm(alpha * h_ref[rows, :] + ffn, g_ref[...], b_ref[...])
        ho_ref[rows, :] = o
        hb_ref[rows, :] = o.astype(BF16)
        return carry

    lax.fori_loop(0, tt // COMBINE_ROWS, rows_body, 0, unroll=4)

    @pl.when(i == n_tiles - 1)
    def _():
        wait_slot(nslot)


def moe_combine(yb, dest, gate_t, h, ln_g, ln_b, alpha):
    t, d = h.shape
    tt = _pick(t, (256, 128, 96, 64, 32))
    n_tiles = t // tt
    return pl.pallas_call(
        functools.partial(_combine_body, tt, alpha, n_tiles), grid=(n_tiles,),
        in_specs=[pl.BlockSpec((2, tt), lambda i: (0, i), memory_space=pltpu.SMEM),
                  pl.BlockSpec((2, tt), lambda i: (0, jnp.minimum(i + 1, n_tiles - 1)), memory_space=pltpu.SMEM),
                  pl.BlockSpec((tt, 2), lambda i: (i, 0)),
                  pl.BlockSpec((tt, d), lambda i: (i, 0)),
                  pl.BlockSpec((1, d), lambda i: (0, 0)),
                  pl.BlockSpec((1, d), lambda i: (0, 0)),
                  pl.BlockSpec(memory_space=pl.ANY)],
        out_specs=[pl.BlockSpec((tt, d), lambda i: (i, 0)), pl.BlockSpec((tt, d), lambda i: (i, 0))],
        out_shape=[jax.ShapeDtypeStruct((t, d), F32), jax.ShapeDtypeStruct((t, d), BF16)],
        scratch_shapes=[pltpu.VMEM((2, 2, tt, d), F32), pltpu.SemaphoreType.DMA((2,))],
        compiler_params=_cparams(("arbitrary",)), name="moe_combine")(
            dest, dest, gate_t, h, ln_g.reshape(1, d), ln_b.reshape(1, d), yb)


def moe_layer(h, router_w, router_b, w_gate, w_up, w_down, ln_g, ln_b, alpha):
    t, d = h.shape
    e_idx, gate, rank, counts = moe_route(h, router_w, router_b)
    padded = (counts + MOE_BLOCK - 1) // MOE_BLOCK * MOE_BLOCK
    pends = jnp.cumsum(padded)
    pstart = pends - padded
    experts = jnp.arange(N_EXPERTS, dtype=I32)[:, None, None]
    dest = jnp.sum(jnp.where(e_idx[None] == experts, pstart[:, None, None], 0), axis=0) + rank
    nblk = -(-(2 * t) // MOE_BLOCK) + N_EXPERTS
    n_used = (pends[-1] // MOE_BLOCK).astype(I32)
    blk = jnp.arange(nblk, dtype=I32)
    blk_first_row = jnp.minimum(blk, n_used - 1) * MOE_BLOCK
    blk_expert = jnp.minimum(jnp.sum((pends[None, :] <= blk_first_row[:, None]).astype(I32), axis=1),
                             N_EXPERTS - 1)
    zero_flag = ((blk >= n_used) | (blk == pends[blk_expert] // MOE_BLOCK - 1)).astype(I32)
    xb = moe_dispatch(h, dest, zero_flag, nblk * MOE_BLOCK)
    yb = moe_experts(xb, blk_expert, n_used.reshape(1), w_gate, w_up, w_down)
    return moe_combine(yb, dest, gate.T, h, ln_g, ln_b, alpha)


def kernel(x, meta_tokens, ln_mix_g, ln_mix_b, ln_ffn_g, ln_ffn_b, conv_pw1_w, conv_pw1_b, conv_dw_w, conv_dw_b, conv_ln_g, conv_ln_b, conv_pw2_w, conv_pw2_b, mla_wdq, mla_q_norm_g, mla_wuq, mla_wdkv, mla_kv_norm_g, mla_wukv, mla_wo, hgrn_w_in, hgrn_lb_logits, hgrn_norm_g, hgrn_wo, router_w, router_b, moe_w_gate, moe_w_up, moe_w_down):
    nb, s_len, d = x.shape
    nm = meta_tokens.shape[0]
    depth = ln_mix_g.shape[0]
    alpha = float((2 * depth) ** 0.25)
    zero_bias = jnp.zeros((d,), F32)

    meta = jnp.broadcast_to(meta_tokens[None].astype(x.dtype), (nb, nm, d)).reshape(nb * nm, d)
    h = jnp.concatenate([x.reshape(nb * s_len, d), meta], axis=0)
    hb = h.astype(BF16)
    p_lb = jax.nn.softmax(hgrn_lb_logits.astype(F32), axis=0)
    lower_bounds = jnp.cumsum(p_lb, axis=0) - p_lb[0]

    for i in range(depth):
        j = i // N_MIXERS
        kind = i % N_MIXERS
        if kind == 0:
            u = mm_glu(hb, conv_pw1_w[j].astype(BF16), conv_pw1_b[j])
            mix_in = conv_ln_swish(u, conv_dw_w[j], conv_dw_b[j], conv_ln_g[j], conv_ln_b[j], nb, s_len, nm)
            w_out, b_out = conv_pw2_w[j], conv_pw2_b[j]
        elif kind == 1:
            mix_in = mla_mixer(hb, mla_wdq[j], mla_q_norm_g[j], mla_wuq[j], mla_wdkv[j], mla_kv_norm_g[j],
                               mla_wukv[j], nb, s_len, nm)
            w_out, b_out = mla_wo[j], zero_bias
        else:
            qfig = mm_plain(hb, hgrn_w_in[j].astype(BF16))
            mix_in = hgrn_mixer(qfig, lower_bounds[i], hgrn_norm_g[j], nb, s_len, nm)
            w_out, b_out = hgrn_wo[j], zero_bias
        h, hb = mm_res_ln(mix_in, h, w_out.astype(BF16), b_out, ln_mix_g[i], ln_mix_b[i], alpha)
        h, hb = moe_layer(h, router_w, router_b, moe_w_gate[i], moe_w_up[i], moe_w_down[i],
                          ln_ffn_g[i], ln_ffn_b[i], alpha)
    return h[:nb * s_len].reshape(nb, s_len, d)
```

```python
import functools

import jax
import jax.numpy as jnp
from jax import lax
from jax.experimental import pallas as pl
from jax.experimental.pallas import tpu as pltpu

F32 = jnp.float32
BF16 = jnp.bfloat16
I32 = jnp.int32
U32 = jnp.uint32

LANES = 128
CHUNK = 64
N_MIXERS = 3
CONV_WIDTH = 31
CONV_HALO = 32
CONV_LANES = 256
HEAD_DIM = 128
MLA_ROPE = 64
ROPE_THETA = 10000.0
HGRN_CHUNK = 128
N_EXPERTS = 32
N_GROUPS = 4
EXPERTS_PER_GROUP = N_EXPERTS // N_GROUPS
MOE_BLOCK = 256
COMBINE_ROWS = 16
LN_EPS = 1e-5
RMS_EPS = 1e-6
VMEM_LIMIT = 52 * 1024 * 1024


def _pick(n, cands):
    for c in cands:
        if n % c == 0:
            return c
    raise ValueError(f"no tile for {n} in {cands}")


_ROW_TILES = (768, 512, 384, 256, 192, 176, 128, 96, 64, 48, 32, 16)


def _cparams(sem, vmem=VMEM_LIMIT):
    return pltpu.CompilerParams(dimension_semantics=sem, vmem_limit_bytes=vmem)


def _layer_norm(x, g, b):
    mu = jnp.mean(x, axis=-1, keepdims=True)
    xc = x - mu
    var = jnp.mean(xc * xc, axis=-1, keepdims=True)
    return xc * lax.rsqrt(var + LN_EPS) * g + b


def _rms_norm(x, g):
    return x * lax.rsqrt(jnp.mean(x * x, axis=-1, keepdims=True) + RMS_EPS) * g


def _sigmoid(x):
    return 1.0 / (1.0 + jnp.exp(-x))


def _pack_bf16_pair(lo, hi):
    lo_b = lax.bitcast_convert_type(lo.astype(BF16).astype(F32), U32)
    hi_b = lax.bitcast_convert_type(hi.astype(BF16).astype(F32), U32)
    return (hi_b & jnp.uint32(0xFFFF0000)) | (lo_b >> 16)


def _unpack_bf16_pair(w):
    lo = lax.bitcast_convert_type(w << 16, F32)
    hi = lax.bitcast_convert_type(w & jnp.uint32(0xFFFF0000), F32)
    return lo, hi


def _write_meta_rows(meta_ref, o_ref):
    o_ref[...] = jnp.zeros(o_ref.shape, o_ref.dtype)
    o_ref[0:meta_ref.shape[0], :] = meta_ref[...]


def _rows_call(body, n_rows, tm, row_ins, full_ins, outs, name):
    grid = (n_rows // tm,)
    in_specs = [pl.BlockSpec((tm, a.shape[1]), lambda i: (i, 0)) for a in row_ins]
    in_specs += [pl.BlockSpec(a.shape, lambda i, nd=a.ndim: (0,) * nd) for a in full_ins]
    out_specs = [pl.BlockSpec((tm, n), lambda i: (i, 0)) for n, _ in outs]
    out_shape = [jax.ShapeDtypeStruct((n_rows, n), dt) for n, dt in outs]
    return pl.pallas_call(
        body, grid=grid, in_specs=in_specs, out_specs=out_specs, out_shape=out_shape,
        compiler_params=_cparams(("parallel",)), name=name)(*row_ins, *full_ins)


def _mm_res_ln_body(alpha, x_ref, h_ref, w_ref, bias_ref, g_ref, b_ref, ho_ref, hb_ref):
    acc = jnp.dot(x_ref[...], w_ref[...], preferred_element_type=F32)
    y = alpha * h_ref[...] + (acc + bias_ref[...])
    o = _layer_norm(y, g_ref[...], b_ref[...])
    ho_ref[...] = o
    hb_ref[...] = o.astype(BF16)


def mm_res_ln(x_bf, h, w_bf, bias, g, b, alpha):
    t, d = h.shape
    tm = _pick(t, (256, 192, 176, 128, 96, 64, 48, 32, 16))
    return _rows_call(functools.partial(_mm_res_ln_body, alpha), t, tm, [x_bf, h],
                      [w_bf, bias.reshape(1, d), g.reshape(1, d), b.reshape(1, d)],
                      [(d, F32), (d, BF16)], "mm_res_ln")


def _glu_body(x_ref, wa_ref, wg_ref, ba_ref, bg_ref, u_ref):
    x = x_ref[...]
    a = jnp.dot(x, wa_ref[...], preferred_element_type=F32) + ba_ref[...]
    g = jnp.dot(x, wg_ref[...], preferred_element_type=F32) + bg_ref[...]
    u_ref[...] = a * _sigmoid(g)


def mm_glu(x_bf, w_bf, bias):
    t, k = x_bf.shape
    d = w_bf.shape[1] // 2
    tm = _pick(t, _ROW_TILES)
    tn = min(d, 1024)
    nj = d // tn
    bias2 = bias.reshape(1, 2 * d)
    return pl.pallas_call(
        _glu_body, grid=(nj, t // tm),
        in_specs=[pl.BlockSpec((tm, k), lambda j, i: (i, 0)),
                  pl.BlockSpec((k, tn), lambda j, i: (0, j)),
                  pl.BlockSpec((k, tn), lambda j, i: (0, j + nj)),
                  pl.BlockSpec((1, tn), lambda j, i: (0, j)),
                  pl.BlockSpec((1, tn), lambda j, i: (0, j + nj))],
        out_specs=pl.BlockSpec((tm, tn), lambda j, i: (i, j)),
        out_shape=jax.ShapeDtypeStruct((t, d), F32),
        compiler_params=_cparams(("parallel", "parallel")), name="mm_glu")(x_bf, w_bf, w_bf, bias2, bias2)


def _mm_plain_body(x_ref, w_ref, o_ref):
    o_ref[...] = jnp.dot(x_ref[...], w_ref[...], preferred_element_type=F32).astype(o_ref.dtype)


def mm_plain(x_bf, w_bf, out_dtype=F32):
    t, k = x_bf.shape
    n = w_bf.shape[1]
    tm = _pick(t, _ROW_TILES)
    tn = min(n, 1024)
    return pl.pallas_call(
        _mm_plain_body, grid=(n // tn, t // tm),
        in_specs=[pl.BlockSpec((tm, k), lambda j, i: (i, 0)),
                  pl.BlockSpec((k, tn), lambda j, i: (0, j))],
        out_specs=pl.BlockSpec((tm, tn), lambda j, i: (i, j)),
        out_shape=jax.ShapeDtypeStruct((t, n), out_dtype),
        compiler_params=_cparams(("parallel", "parallel")), name="mm_plain")(x_bf, w_bf)


def _conv_rows(ext_ref, w_ref, y_ref, sh_ref, rows, rc, lc):
    d = y_ref.shape[1]
    shift = CONV_HALO - (CONV_WIDTH - 1)
    sub = 8
    n_sh = sh_ref.shape[1]

    def lane_body(c, carry):
        l0 = pl.multiple_of(c * lc, lc)
        lanes = pl.ds(l0, lc)
        wv = w_ref[:, lanes]
        for b in range(1, sub):
            sh_ref[b - 1] = ext_ref[pl.ds(b, n_sh), lanes]
        for r in range(rows // rc):
            acc = None
            for k in range(CONV_WIDTH):
                b = (shift + k) % sub
                a = r * rc + (shift + k) - b
                src = ext_ref[pl.ds(a, rc), lanes] if b == 0 else sh_ref[b - 1, pl.ds(a, rc), :]
                term = src * wv[k:k + 1, :]
                acc = term if acc is None else acc + term
            y_ref[pl.ds(r * rc, rc), lanes] = acc
        return carry

    lax.fori_loop(0, d // lc, lane_body, 0)


def _conv_epilogue(y_ref, dwb_ref, g_ref, b_ref, o_ref):
    rows = y_ref.shape[0]
    step = 16

    def body(c, carry):
        sl = pl.ds(pl.multiple_of(c * step, step), step)
        z = _layer_norm(y_ref[sl, :] + dwb_ref[...], g_ref[...], b_ref[...])
        o_ref[sl, :] = (z * _sigmoid(z)).astype(BF16)
        return carry

    lax.fori_loop(0, rows // step, body, 0, unroll=min(4, rows // step))


def _conv_main_body(tr, nst, n_real, cur_ref, prev_ref, meta_ref, w_ref, dwb_ref, g_ref, b_ref, vmeta_ref, o_ref,
                    ext_ref, y_ref, sh_ref):
    step = pl.program_id(0)
    nm = meta_ref.shape[0]

    @pl.when(step < n_real)
    def _():
        @pl.when(step % nst == 0)
        def _():
            ext_ref[0:CONV_HALO - nm, :] = jnp.zeros((CONV_HALO - nm, ext_ref.shape[1]), F32)
            ext_ref[CONV_HALO - nm:CONV_HALO, :] = meta_ref[...]

        @pl.when(step % nst > 0)
        def _():
            ext_ref[0:CONV_HALO, :] = prev_ref[...]

        ext_ref[CONV_HALO:CONV_HALO + tr, :] = cur_ref[...]
        _conv_rows(ext_ref, w_ref, y_ref, sh_ref, tr, 64, CONV_LANES)
        _conv_epilogue(y_ref, dwb_ref, g_ref, b_ref, o_ref)

    @pl.when(step == n_real)
    def _():
        _write_meta_rows(vmeta_ref, o_ref)


def _conv_meta_body(meta_ref, w_ref, dwb_ref, g_ref, b_ref, o_ref, ext_ref, y_ref, sh_ref):
    nm = meta_ref.shape[0]
    ext_ref[0:CONV_HALO, :] = jnp.zeros((CONV_HALO, ext_ref.shape[1]), F32)
    ext_ref[CONV_HALO:CONV_HALO + nm, :] = meta_ref[...]
    _conv_rows(ext_ref, w_ref, y_ref, sh_ref, nm, nm, CONV_LANES)
    _conv_epilogue(y_ref, dwb_ref, g_ref, b_ref, o_ref)


def conv_ln_swish(u, dw_w, dw_b, ln_g, ln_b, nb, s_len, nm):
    t, d = u.shape
    tr_rows = nb * s_len
    tr = _pick(s_len, (512, 256))
    w_pad = jnp.concatenate([dw_w, jnp.zeros((CONV_HALO - CONV_WIDTH, d), F32)], axis=0)
    vecs = [dw_b.reshape(1, d), ln_g.reshape(1, d), ln_b.reshape(1, d)]
    meta_blk = tr_rows // nm
    out_meta = pl.pallas_call(
        _conv_meta_body, grid=(nb,),
        in_specs=[pl.BlockSpec((nm, d), lambda b: (meta_blk + b, 0)),
                  pl.BlockSpec((CONV_HALO, d), lambda b: (0, 0))]
        + [pl.BlockSpec((1, d), lambda b: (0, 0))] * 3,
        out_specs=pl.BlockSpec((nm, d), lambda b: (b, 0)),
        out_shape=jax.ShapeDtypeStruct((nb * nm, d), BF16),
        scratch_shapes=[pltpu.VMEM((CONV_HALO + nm, d), F32), pltpu.VMEM((nm, d), F32),
                        pltpu.VMEM((7, CONV_HALO + nm - 8, CONV_LANES), F32)],
        compiler_params=_cparams(("parallel",)), name="conv_meta")(u, w_pad, *vecs)
    nst = s_len // tr
    n_real = nb * nst
    halo_per_tile = tr // CONV_HALO
    assert nb * nm <= tr

    def real(g):
        return jnp.minimum(g, n_real - 1)

    return pl.pallas_call(
        functools.partial(_conv_main_body, tr, nst, n_real), grid=(n_real + 1,),
        in_specs=[pl.BlockSpec((tr, d), lambda g: (real(g), 0)),
                  pl.BlockSpec((CONV_HALO, d), lambda g: (jnp.maximum(real(g) * halo_per_tile - 1, 0), 0)),
                  pl.BlockSpec((nm, d), lambda g: (meta_blk + real(g) // nst, 0)),
                  pl.BlockSpec((CONV_HALO, d), lambda g: (0, 0))]
        + [pl.BlockSpec((1, d), lambda g: (0, 0))] * 3
        + [pl.BlockSpec((nb * nm, d), lambda g: (0, 0))],
        out_specs=pl.BlockSpec((tr, d), lambda g: (g, 0)),
        out_shape=jax.ShapeDtypeStruct((t, d), BF16),
        scratch_shapes=[pltpu.VMEM((CONV_HALO + tr, d), F32), pltpu.VMEM((tr, d), F32),
                        pltpu.VMEM((7, CONV_HALO + tr - 8, CONV_LANES), F32)],
        compiler_params=_cparams(("arbitrary",)), name="conv_main")(u, u, u, w_pad, *vecs, out_meta)


def _mla_proj_body(ql, kvl, hd, scale, x_ref, cos_ref, sin_ref, w1_ref, qg_ref, kvg_ref, wq_ref, wkv_ref,
                   qn_ref, qr_ref, kn_ref, v_ref, kr_ref):
    a = jnp.dot(x_ref[...], w1_ref[...], preferred_element_type=F32)
    cos = cos_ref[...]
    sin = sin_ref[...]
    cq = _rms_norm(a[:, :ql], qg_ref[...]).astype(BF16)
    ckv = _rms_norm(a[:, ql:ql + kvl], kvg_ref[...]).astype(BF16)
    r0 = ql + kvl
    kr_ref[...] = (a[:, r0:r0 + LANES] * cos + a[:, r0 + LANES:r0 + 2 * LANES] * sin).astype(BF16)
    qa = jnp.dot(cq, wq_ref[...], preferred_element_type=F32)
    nh = hd // LANES
    cos_t = jnp.tile(cos, (1, nh))
    sin_t = jnp.tile(sin, (1, nh))
    qn_ref[...] = (qa[:, :hd] * scale).astype(BF16)
    qr_ref[...] = ((qa[:, hd:2 * hd] * cos_t + qa[:, 2 * hd:] * sin_t) * scale).astype(BF16)
    kv = jnp.dot(ckv, wkv_ref[...], preferred_element_type=F32)
    kn_ref[...] = kv[:, :hd].astype(BF16)
    v_ref[...] = kv[:, hd:].astype(BF16)


def _attn_body(tq, nb, qn_ref, qr_ref, kn_ref, kr_ref, v_ref, knm_ref, krm_ref, vm_ref, ometa_ref, o_ref, kf_ref):
    @pl.when(pl.program_id(0) == nb)
    def _():
        _write_meta_rows(ometa_ref, o_ref)

    @pl.when(pl.program_id(0) < nb)
    def _():
        _attn_tiles(tq, qn_ref, qr_ref, kn_ref, kr_ref, v_ref, knm_ref, krm_ref, vm_ref, o_ref, kf_ref)


def _attn_tiles(tq, qn_ref, qr_ref, kn_ref, kr_ref, v_ref, knm_ref, krm_ref, vm_ref, o_ref, kf_ref):
    s_len = qn_ref.shape[0]
    nt = (((1,), (1,)), ((), ()))
    kf_ref[:, :LANES] = kn_ref[...]
    kf_ref[:, LANES:] = kr_ref[...]
    km = jnp.concatenate([knm_ref[...], krm_ref[...]], axis=-1)
    vm = vm_ref[...]
    row_c = lax.broadcasted_iota(I32, (tq, tq), 0) // CHUNK
    col_c = lax.broadcasted_iota(I32, (tq, tq), 1) // CHUNK
    visible = col_c <= row_c
    for i in range(s_len // tq):
        r0 = i * tq
        q = jnp.concatenate([qn_ref[r0:r0 + tq, :], qr_ref[r0:r0 + tq, :]], axis=-1)
        s_m = lax.dot_general(q, km, nt, preferred_element_type=F32)
        s_d = lax.dot_general(q, kf_ref[r0:r0 + tq, :], nt, preferred_element_type=F32)
        s_d = jnp.where(visible, s_d, -jnp.inf)
        m = jnp.maximum(jnp.max(s_m, axis=-1, keepdims=True), jnp.max(s_d, axis=-1, keepdims=True))
        if i > 0:
            s_p = lax.dot_general(q, kf_ref[0:r0, :], nt, preferred_element_type=F32)
            m = jnp.maximum(m, jnp.max(s_p, axis=-1, keepdims=True))
        p_m = jnp.exp(s_m - m)
        p_d = jnp.exp(s_d - m)
        l = jnp.sum(p_m, axis=-1, keepdims=True) + jnp.sum(p_d, axis=-1, keepdims=True)
        acc = jnp.dot(p_m.astype(BF16), vm, preferred_element_type=F32)
        acc = acc + jnp.dot(p_d.astype(BF16), v_ref[r0:r0 + tq, :], preferred_element_type=F32)
        if i > 0:
            p_p = jnp.exp(s_p - m)
            l = l + jnp.sum(p_p, axis=-1, keepdims=True)
            acc = acc + jnp.dot(p_p.astype(BF16), v_ref[0:r0, :], preferred_element_type=F32)
        o_ref[r0:r0 + tq, :] = (acc / l).astype(BF16)


def _attn_meta_body(qn_ref, qr_ref, kn_ref, kr_ref, v_ref, o_ref):
    nt = (((1,), (1,)), ((), ()))
    q = jnp.concatenate([qn_ref[...], qr_ref[...]], axis=-1)
    k = jnp.concatenate([kn_ref[...], kr_ref[...]], axis=-1)
    s = lax.dot_general(q, k, nt, preferred_element_type=F32)
    p = jnp.exp(s - jnp.max(s, axis=-1, keepdims=True))
    l = jnp.sum(p, axis=-1, keepdims=True)
    o_ref[...] = (jnp.dot(p.astype(BF16), v_ref[...], preferred_element_type=F32) / l).astype(BF16)


def _rope_rows(n_pos):
    inv = ROPE_THETA ** (-jnp.arange(0, MLA_ROPE, 2, dtype=F32) / MLA_ROPE)
    ang = jnp.arange(n_pos, dtype=F32)[:, None] * inv[None, :]
    pad = jnp.zeros((n_pos, LANES - MLA_ROPE), F32)
    cos = jnp.concatenate([jnp.cos(ang), jnp.cos(ang), pad], axis=1)
    sin = jnp.concatenate([jnp.sin(ang), jnp.sin(ang), pad], axis=1)
    return cos, sin


def _pad_rope_cols(w):
    half = MLA_ROPE // 2
    z = jnp.zeros((w.shape[0], LANES - MLA_ROPE), w.dtype)
    rot = jnp.concatenate([-w[:, half:], w[:, :half]], axis=1)
    return jnp.concatenate([w, z], axis=1), jnp.concatenate([rot, z], axis=1)


def mla_mixer(hb, wdq, q_norm_g, wuq, wdkv, kv_norm_g, wukv, nb, s_len, nm):
    t, d = hb.shape
    ql = wdq.shape[1]
    kvl = kv_norm_g.shape[0]
    nh = wuq.shape[1] // (HEAD_DIM + MLA_ROPE)
    hd = nh * HEAD_DIM
    tr_rows = nb * s_len
    scale = float((HEAD_DIM + MLA_ROPE) ** -0.5)
    kr_w, kr_rot = _pad_rope_cols(wdkv[:, kvl:])
    w1 = jnp.concatenate([wdq, wdkv[:, :kvl], kr_w, kr_rot], axis=1).astype(BF16)
    wuq3 = wuq.reshape(ql, nh, HEAD_DIM + MLA_ROPE)
    q_rope = wuq3[:, :, HEAD_DIM:]
    half = MLA_ROPE // 2
    zq = jnp.zeros((ql, nh, LANES - MLA_ROPE), F32)
    q_rope_p = jnp.concatenate([q_rope, zq], axis=2).reshape(ql, hd)
    q_rot_p = jnp.concatenate([-q_rope[:, :, half:], q_rope[:, :, :half], zq], axis=2).reshape(ql, hd)
    wq = jnp.concatenate([wuq3[:, :, :HEAD_DIM].reshape(ql, hd), q_rope_p, q_rot_p], axis=1).astype(BF16)
    wukv3 = wukv.reshape(kvl, nh, 2 * HEAD_DIM)
    wkv = jnp.concatenate([wukv3[:, :, :HEAD_DIM].reshape(kvl, hd),
                           wukv3[:, :, HEAD_DIM:].reshape(kvl, hd)], axis=1).astype(BF16)
    cos_p, sin_p = _rope_rows(nm + s_len)
    cos_rows = jnp.concatenate([jnp.tile(cos_p[nm:], (nb, 1)), jnp.tile(cos_p[:nm], (nb, 1))], axis=0)
    sin_rows = jnp.concatenate([jnp.tile(sin_p[nm:], (nb, 1)), jnp.tile(sin_p[:nm], (nb, 1))], axis=0)

    tm = _pick(t, (128, 96, 64, 48, 32, 16))
    qn, qr, kn, v, kr = _rows_call(
        functools.partial(_mla_proj_body, ql, kvl, hd, scale), t, tm, [hb, cos_rows, sin_rows],
        [w1, q_norm_g.reshape(1, ql), kv_norm_g.reshape(1, kvl), wq, wkv],
        [(hd, BF16), (hd, BF16), (hd, BF16), (hd, BF16), (LANES, BF16)], "mla_proj")

    meta_blk = tr_rows // nm
    mspec = lambda: pl.BlockSpec((nm, LANES), lambda b, h: (meta_blk + b, h))
    mspec0 = lambda: pl.BlockSpec((nm, LANES), lambda b, h: (meta_blk + b, 0))
    o_meta = pl.pallas_call(
        _attn_meta_body, grid=(nb, nh),
        in_specs=[mspec(), mspec(), mspec(), mspec0(), mspec()],
        out_specs=pl.BlockSpec((nm, LANES), lambda b, h: (b, h)),
        out_shape=jax.ShapeDtypeStruct((nb * nm, hd), BF16),
        compiler_params=_cparams(("parallel", "parallel")), name="attn_meta")(qn, qr, kn, kr, v)

    tq = 256
    assert nb * nm <= s_len

    def real(b):
        return jnp.minimum(b, nb - 1)

    kspec = lambda: pl.BlockSpec((s_len, LANES), lambda b, h: (real(b), h))
    kspec0 = lambda: pl.BlockSpec((s_len, LANES), lambda b, h: (real(b), 0))
    m2 = lambda: pl.BlockSpec((nm, LANES), lambda b, h: (meta_blk + real(b), h))
    m20 = lambda: pl.BlockSpec((nm, LANES), lambda b, h: (meta_blk + real(b), 0))
    return pl.pallas_call(
        functools.partial(_attn_body, tq, nb), grid=(nb + 1, nh),
        in_specs=[kspec(), kspec(), kspec(), kspec0(), kspec(), m2(), m20(), m2(),
                  pl.BlockSpec((nb * nm, LANES), lambda b, h: (0, h))],
        out_specs=pl.BlockSpec((s_len, LANES), lambda b, h: (b, h)),
        out_shape=jax.ShapeDtypeStruct((t, hd), BF16),
        scratch_shapes=[pltpu.VMEM((s_len, 2 * LANES), BF16)],
        compiler_params=_cparams(("arbitrary", "arbitrary")), name="attn_main")(
            qn, qr, kn, kr, v, kn, kr, v, o_meta)


def _hgrn_gates(fz, lb):
    return lb + (1.0 - lb) * _sigmoid(fz), (1.0 - lb) * _sigmoid(-fz)


def _hgrn_level_index(c):
    t = lax.broadcasted_iota(I32, (c, c), 0)
    s = lax.broadcasted_iota(I32, (c, c), 1)
    lvl = 31 - lax.clz(t ^ s)
    return jnp.where(s > t, -2, lvl)


def _hgrn_block(q, f, k, iv, st, lvl):
    c = q.shape[0]
    nt = (((1,), (1,)), ((), ()))
    tn = (((0,), (0,)), ((), ()))
    row = lax.broadcasted_iota(I32, (c, HEAD_DIM), 0)
    scores = jnp.where(lvl == -1, lax.dot_general(q.astype(BF16), k.astype(BF16), nt,
                                                  preferred_element_type=F32), 0.0)
    qa = q * f
    kb = k
    tot = f
    h, idx = 1, 0
    while h < c:
        prod = lax.dot_general(qa.astype(BF16), kb.astype(BF16), nt, preferred_element_type=F32)
        scores = jnp.where(lvl == idx, prod, scores)
        right = (row & h) != 0
        left_tot = pltpu.roll(tot, h, 0)
        right_tot = pltpu.roll(tot, c - h, 0)
        qa = qa * jnp.where(right, left_tot, 1.0)
        kb = kb * jnp.where(right, 1.0, right_tot)
        tot = tot * jnp.where(right, left_tot, right_tot)
        h, idx = 2 * h, idx + 1
    ib = iv.astype(BF16)
    o = jnp.dot(scores.astype(BF16), ib, preferred_element_type=F32)
    o = o + lax.dot_general(qa.astype(BF16), st.astype(BF16), nt, preferred_element_type=F32)
    st_new = st * tot[0:1, :] + lax.dot_general(ib, kb.astype(BF16), tn, preferred_element_type=F32)
    return o, st_new


def _hgrn_out(o, gate, ng):
    o = o * lax.rsqrt(jnp.mean(o * o, axis=-1, keepdims=True) + RMS_EPS) * ng
    return (o * (gate * _sigmoid(gate))).astype(BF16)


def _hgrn_meta_body(gh, q_ref, fz_ref, i_ref, g_ref, lb_ref, ng_ref, o_ref, st_ref):
    nm = q_ref.shape[0]
    c = HGRN_CHUNK
    lvl = _hgrn_level_index(c)
    zeros = jnp.zeros((c - nm, HEAD_DIM), F32)
    for g in range(gh):
        sl = slice(g * HEAD_DIM, (g + 1) * HEAD_DIM)
        f, k = _hgrn_gates(fz_ref[:, sl], lb_ref[:, sl])
        o, st = _hgrn_block(jnp.concatenate([zeros, q_ref[:, sl]], axis=0),
                            jnp.concatenate([zeros + 1.0, f], axis=0),
                            jnp.concatenate([zeros, k], axis=0),
                            jnp.concatenate([zeros, i_ref[:, sl]], axis=0),
                            jnp.zeros((HEAD_DIM, HEAD_DIM), F32), lvl)
        st_ref[0, g] = st
        o_ref[:, sl] = _hgrn_out(o[c - nm:], g_ref[:, sl], ng_ref[...])


def _hgrn_main_body(gh, ts, nst, n_real, q_ref, fz_ref, i_ref, g_ref, lb_ref, ng_ref, st0_ref, ometa_ref, o_ref,
                    st_ref):
    c = HGRN_CHUNK
    step = pl.program_id(1)

    @pl.when(step == n_real)
    def _():
        _write_meta_rows(ometa_ref, o_ref)

    @pl.when(step < n_real)
    def _():
        lvl = _hgrn_level_index(c)

        @pl.when(step % nst == 0)
        def _():
            st_ref[...] = st0_ref[0]

        def chunk_body(cidx, carry):
            rows = pl.ds(pl.multiple_of(cidx * c, c), c)
            for g in range(gh):
                sl = slice(g * HEAD_DIM, (g + 1) * HEAD_DIM)
                f, k = _hgrn_gates(fz_ref[rows, sl], lb_ref[:, sl])
                o, st = _hgrn_block(q_ref[rows, sl], f, k, i_ref[rows, sl], st_ref[g], lvl)
                st_ref[g] = st
                o_ref[rows, sl] = _hgrn_out(o, g_ref[rows, sl], ng_ref[...])
            return carry

        lax.fori_loop(0, ts // c, chunk_body, 0)


def hgrn_mixer(qfig, lb, norm_g, nb, s_len, nm):
    t, d4 = qfig.shape
    d = d4 // 4
    nh = d // HEAD_DIM
    gh = 4 if nh % 4 == 0 else 1
    gw = gh * HEAD_DIM
    ng_blocks = d // gw
    tr_rows = nb * s_len
    meta_blk = tr_rows // nm
    lb2 = lb.reshape(1, d)
    ng2 = norm_g.reshape(1, HEAD_DIM)

    def mspec(sec):
        return pl.BlockSpec((nm, gw), lambda b, h: (meta_blk + b, sec * ng_blocks + h))

    o_meta, st0 = pl.pallas_call(
        functools.partial(_hgrn_meta_body, gh), grid=(nb, ng_blocks),
        in_specs=[mspec(0), mspec(1), mspec(2), mspec(3),
                  pl.BlockSpec((1, gw), lambda b, h: (0, h)),
                  pl.BlockSpec((1, HEAD_DIM), lambda b, h: (0, 0))],
        out_specs=[pl.BlockSpec((nm, gw), lambda b, h: (b, h)),
                   pl.BlockSpec((1, gh, HEAD_DIM, HEAD_DIM), lambda b, h: (b, h, 0, 0))],
        out_shape=[jax.ShapeDtypeStruct((nb * nm, d), BF16),
                   jax.ShapeDtypeStruct((nb, nh, HEAD_DIM, HEAD_DIM), F32)],
        compiler_params=_cparams(("parallel", "parallel")), name="hgrn_meta")(qfig, qfig, qfig, qfig, lb2, ng2)

    ts = _pick(s_len, (512, 256))
    nst = s_len // ts
    n_real = nb * nst
    assert nb * nm <= ts

    def real(g):
        return jnp.minimum(g, n_real - 1)

    def rspec(sec):
        return pl.BlockSpec((ts, gw), lambda h, g: (real(g), sec * ng_blocks + h))

    return pl.pallas_call(
        functools.partial(_hgrn_main_body, gh, ts, nst, n_real), grid=(ng_blocks, n_real + 1),
        in_specs=[rspec(0), rspec(1), rspec(2), rspec(3),
                  pl.BlockSpec((1, gw), lambda h, g: (0, h)),
                  pl.BlockSpec((1, HEAD_DIM), lambda h, g: (0, 0)),
                  pl.BlockSpec((1, gh, HEAD_DIM, HEAD_DIM), lambda h, g: (real(g) // nst, h, 0, 0)),
                  pl.BlockSpec((nb * nm, gw), lambda h, g: (0, h))],
        out_specs=pl.BlockSpec((ts, gw), lambda h, g: (g, h)),
        out_shape=jax.ShapeDtypeStruct((t, d), BF16),
        scratch_shapes=[pltpu.VMEM((gh, HEAD_DIM, HEAD_DIM), F32)],
        compiler_params=_cparams(("parallel", "arbitrary")), name="hgrn_main")(
            qfig, qfig, qfig, qfig, lb2, ng2, st0, o_meta)


def _first_index_of_max(vals, idx, n, axis):
    mx = jnp.max(vals, axis=axis, keepdims=True)
    first = jnp.min(jnp.where(vals == mx, idx, n), axis=axis, keepdims=True)
    return mx, first


def _router_body(tm, h_ref, rw_ref, rb_ref, tri_ref, e_ref, gate_ref, rank_ref, cnt_ref, base_ref):
    @pl.when(pl.program_id(0) == 0)
    def _():
        base_ref[...] = jnp.zeros(base_ref.shape, F32)

    nt = (((1,), (1,)), ((), ()))
    logits = lax.dot_general(rw_ref[...], h_ref[...], nt, precision=lax.Precision.HIGHEST,
                             preferred_element_type=F32)
    scores = _sigmoid(logits)
    sel = scores + rb_ref[...]
    g, epg = N_GROUPS, EXPERTS_PER_GROUP
    sel3 = sel.reshape(g, epg, tm)
    sc3 = scores.reshape(g, epg, tm)
    idx3 = lax.broadcasted_iota(I32, (g, epg, tm), 1)
    m1, i1 = _first_index_of_max(sel3, idx3, epg, 1)
    rest = jnp.where(idx3 == i1, -jnp.inf, sel3)
    m2, i2 = _first_index_of_max(rest, idx3, epg, 1)
    gidx = lax.broadcasted_iota(I32, (g, 1, tm), 0)
    _, gtop3 = _first_index_of_max(m1 + m2, gidx, g, 0)
    pick = gidx == gtop3
    gtop = gtop3[0]
    l1 = jnp.sum(jnp.where(pick, i1, 0), axis=0)
    l2 = jnp.sum(jnp.where(pick, i2, 0), axis=0)
    sc_in = jnp.sum(jnp.where(pick, sc3, 0.0), axis=0)
    idx2 = lax.broadcasted_iota(I32, (epg, tm), 0)
    s1 = jnp.sum(jnp.where(idx2 == l1, sc_in, 0.0), axis=0, keepdims=True)
    s2 = jnp.sum(jnp.where(idx2 == l2, sc_in, 0.0), axis=0, keepdims=True)
    e1 = gtop * epg + l1
    e2 = gtop * epg + l2
    e_ref[0:1, :] = e1
    e_ref[1:2, :] = e2
    den = s1 + s2
    gate_ref[0:1, :] = s1 / den
    gate_ref[1:2, :] = s2 / den
    eidx = lax.broadcasted_iota(I32, (N_EXPERTS, tm), 0)
    oh1 = (eidx == e1).astype(F32)
    oh2 = (eidx == e2).astype(F32)
    oh = jnp.concatenate([oh1, oh2], axis=0).astype(BF16)
    pre = jnp.dot(oh, tri_ref[...], preferred_element_type=F32)
    base = base_ref[:, 0:1]
    tot1 = jnp.sum(oh1, axis=1, keepdims=True)
    tot2 = jnp.sum(oh2, axis=1, keepdims=True)
    r1 = jnp.sum(oh1 * (base + pre[:N_EXPERTS]), axis=0, keepdims=True)
    r2 = jnp.sum(oh2 * (base + tot1 + pre[N_EXPERTS:]), axis=0, keepdims=True)
    rank_ref[0:1, :] = r1.astype(I32)
    rank_ref[1:2, :] = r2.astype(I32)
    new_base = jnp.broadcast_to(base + tot1 + tot2, base_ref.shape)
    base_ref[...] = new_base
    cnt_ref[...] = new_base


def moe_route(h, router_w, router_b):
    t, d = h.shape
    tm = _pick(t, (256, 128, 96, 64, 32))
    tri = (lax.broadcasted_iota(I32, (tm, tm), 0) < lax.broadcasted_iota(I32, (tm, tm), 1)).astype(BF16)
    e, gate, rank, cnt = pl.pallas_call(
        functools.partial(_router_body, tm), grid=(t // tm,),
        in_specs=[pl.BlockSpec((tm, d), lambda i: (i, 0)),
                  pl.BlockSpec((N_EXPERTS, d), lambda i: (0, 0)),
                  pl.BlockSpec((N_EXPERTS, 1), lambda i: (0, 0)),
                  pl.BlockSpec((tm, tm), lambda i: (0, 0))],
        out_specs=[pl.BlockSpec((2, tm), lambda i: (0, i)),
                   pl.BlockSpec((2, tm), lambda i: (0, i)),
                   pl.BlockSpec((2, tm), lambda i: (0, i)),
                   pl.BlockSpec((N_EXPERTS, LANES), lambda i: (0, 0))],
        out_shape=[jax.ShapeDtypeStruct((2, t), I32), jax.ShapeDtypeStruct((2, t), F32),
                   jax.ShapeDtypeStruct((2, t), I32), jax.ShapeDtypeStruct((N_EXPERTS, LANES), F32)],
        scratch_shapes=[pltpu.VMEM((N_EXPERTS, LANES), F32)],
        compiler_params=_cparams(("arbitrary",)), name="moe_route")(
            h, router_w.T, router_b.reshape(N_EXPERTS, 1), tri)
    return e, gate, rank, cnt[:, 0].astype(I32)


def _dispatch_body(tt, zero_flag_ref, dest_ref, hb_ref, xb_hbm, stage_ref, zero_ref, sem):
    half = stage_ref.shape[1]

    @pl.when(pl.program_id(0) == 0)
    def _():
        zero_ref[...] = jnp.zeros(zero_ref.shape, U32)

        def zfill(b, carry):
            @pl.when(zero_flag_ref[b] > 0)
            def _():
                start = pl.multiple_of(b * MOE_BLOCK, MOE_BLOCK)
                cp = pltpu.make_async_copy(zero_ref, xb_hbm.at[pl.ds(start, MOE_BLOCK), :], sem)
                cp.start()
                cp.wait()
            return carry

        lax.fori_loop(0, zero_flag_ref.shape[0], zfill, 0)

    stage_ref[...] = _pack_bf16_pair(hb_ref[:, :half].astype(F32), hb_ref[:, half:].astype(F32))

    def issue(j, carry):
        for k in range(2):
            pltpu.make_async_copy(stage_ref.at[pl.ds(j, 1), :],
                                  xb_hbm.at[pl.ds(dest_ref[k, j], 1), :], sem).start()
        return carry

    lax.fori_loop(0, tt, issue, 0, unroll=8)
    for k in range(2):
        pltpu.make_async_copy(stage_ref, xb_hbm.at[pl.ds(0, tt), :], sem).wait()


def moe_dispatch(hb, dest, zero_flag, n_slots):
    t, d = hb.shape
    tt = _pick(t, (256, 128, 96, 64, 32))
    return pl.pallas_call(
        functools.partial(_dispatch_body, tt),
        grid_spec=pltpu.PrefetchScalarGridSpec(
            num_scalar_prefetch=1, grid=(t // tt,),
            in_specs=[pl.BlockSpec((2, tt), lambda i, zf: (0, i), memory_space=pltpu.SMEM),
                      pl.BlockSpec((tt, d), lambda i, zf: (i, 0))],
            out_specs=pl.BlockSpec(memory_space=pl.ANY),
            scratch_shapes=[pltpu.VMEM((tt, d // 2), U32), pltpu.VMEM((MOE_BLOCK, d // 2), U32),
                            pltpu.SemaphoreType.DMA(())]),
        out_shape=jax.ShapeDtypeStruct((n_slots, d // 2), U32),
        compiler_params=_cparams(("arbitrary",)), name="moe_dispatch")(zero_flag, dest, hb)


def _expert_body(be_ref, nu_ref, x_ref, wg_ref, wu_ref, wd_ref, y_ref, wgb_ref, wub_ref, wdb_ref):
    i = pl.program_id(0)
    used = i < nu_ref[0]
    half = x_ref.shape[1]

    @pl.when(jnp.logical_or(i == 0, be_ref[i] != be_ref[jnp.maximum(i - 1, 0)]))
    def _():
        wgb_ref[...] = wg_ref[0, 0].astype(BF16)
        wub_ref[...] = wu_ref[0, 0].astype(BF16)
        wdb_ref[...] = wd_ref[0, 0].astype(BF16)

    @pl.when(used)
    def _():
        lo, hi = _unpack_bf16_pair(x_ref[...])
        lo = lo.astype(BF16)
        hi = hi.astype(BF16)
        g = (jnp.dot(lo, wgb_ref[:half, :], preferred_element_type=F32)
             + jnp.dot(hi, wgb_ref[half:, :], preferred_element_type=F32))
        u = (jnp.dot(lo, wub_ref[:half, :], preferred_element_type=F32)
             + jnp.dot(hi, wub_ref[half:, :], preferred_element_type=F32))
        a = (g * _sigmoid(g) * u).astype(BF16)
        y = jnp.dot(a, wdb_ref[...], preferred_element_type=F32)
        y_ref[...] = _pack_bf16_pair(y[:, :half], y[:, half:])

    @pl.when(jnp.logical_not(used))
    def _():
        y_ref[...] = jnp.zeros(y_ref.shape, U32)


def moe_experts(xb, blk_expert, n_used, wg, wu, wd, layer):
    n_slots, half = xb.shape
    d = 2 * half
    ff = wg.shape[3]
    nblk = n_slots // MOE_BLOCK
    return pl.pallas_call(
        _expert_body,
        grid_spec=pltpu.PrefetchScalarGridSpec(
            num_scalar_prefetch=2, grid=(nblk,),
            in_specs=[pl.BlockSpec((MOE_BLOCK, half), lambda i, be, nu: (i, 0)),
                      pl.BlockSpec((1, 1, d, ff), lambda i, be, nu: (layer, be[i], 0, 0)),
                      pl.BlockSpec((1, 1, d, ff), lambda i, be, nu: (layer, be[i], 0, 0)),
                      pl.BlockSpec((1, 1, ff, d), lambda i, be, nu: (layer, be[i], 0, 0))],
            out_specs=pl.BlockSpec((MOE_BLOCK, half), lambda i, be, nu: (i, 0)),
            scratch_shapes=[pltpu.VMEM((d, ff), BF16), pltpu.VMEM((d, ff), BF16), pltpu.VMEM((ff, d), BF16)]),
        out_shape=jax.ShapeDtypeStruct((n_slots, half), U32),
        compiler_params=_cparams(("arbitrary",)), name="moe_experts")(
            blk_expert, n_used, xb, wg, wu, wd)


def _combine_body(tt, alpha, n_tiles, dest_ref, dnext_ref, gate_ref, h_ref, g_ref, b_ref, y_hbm, ho_ref, hb_ref,
                  buf_ref, sem):
    i = pl.program_id(0)
    slot = i % 2
    nslot = 1 - slot

    def gather(idx_ref, j, to_slot):
        for k in range(2):
            pltpu.make_async_copy(y_hbm.at[pl.ds(idx_ref[k, j], 1), :],
                                  buf_ref.at[to_slot, k, pl.ds(j, 1), :], sem.at[to_slot]).start()

    def wait_slot(s):
        for k in range(2):
            pltpu.make_async_copy(y_hbm.at[pl.ds(0, tt), :], buf_ref.at[s, k], sem.at[s]).wait()

    @pl.when(i == 0)
    def _():
        def first(j, carry):
            gather(dest_ref, j, 0)
            return carry
        lax.fori_loop(0, tt, first, 0, unroll=8)

    wait_slot(slot)

    def ahead(j, carry):
        gather(dnext_ref, j, nslot)
        return carry

    lax.fori_loop(0, tt, ahead, 0, unroll=8)

    def rows_body(c, carry):
        r0 = pl.multiple_of(c * COMBINE_ROWS, COMBINE_ROWS)
        rows = pl.ds(r0, COMBINE_ROWS)
        gate = gate_ref[rows, :]
        lo0, hi0 = _unpack_bf16_pair(buf_ref[slot, 0, rows, :])
        lo1, hi1 = _unpack_bf16_pair(buf_ref[slot, 1, rows, :])
        ffn = jnp.concatenate([gate[:, 0:1] * lo0 + gate[:, 1:2] * lo1,
                               gate[:, 0:1] * hi0 + gate[:, 1:2] * hi1], axis=-1)
        o = _layer_norm(alpha * h_ref[rows, :] + ffn, g_ref[...], b_ref[...])
        ho_ref[rows, :] = o
        hb_ref[rows, :] = o.astype(BF16)
        return carry

    lax.fori_loop(0, tt // COMBINE_ROWS, rows_body, 0, unroll=4)

    @pl.when(i == n_tiles - 1)
    def _():
        wait_slot(nslot)


def moe_combine(yb, dest, gate_t, h, ln_g, ln_b, alpha, n_rows):
    t, d = h.shape
    tt = next(c for c in (256, 128, 96, 64, 32) if t % c == 0 and n_rows % c == 0)
    n_tiles = n_rows // tt
    return pl.pallas_call(
        functools.partial(_combine_body, tt, alpha, n_tiles), grid=(n_tiles,),
        in_specs=[pl.BlockSpec((2, tt), lambda i: (0, i), memory_space=pltpu.SMEM),
                  pl.BlockSpec((2, tt), lambda i: (0, jnp.minimum(i + 1, n_tiles - 1)), memory_space=pltpu.SMEM),
                  pl.BlockSpec((tt, 2), lambda i: (i, 0)),
                  pl.BlockSpec((tt, d), lambda i: (i, 0)),
                  pl.BlockSpec((1, d), lambda i: (0, 0)),
                  pl.BlockSpec((1, d), lambda i: (0, 0)),
                  pl.BlockSpec(memory_space=pl.ANY)],
        out_specs=[pl.BlockSpec((tt, d), lambda i: (i, 0)), pl.BlockSpec((tt, d), lambda i: (i, 0))],
        out_shape=[jax.ShapeDtypeStruct((n_rows, d), F32), jax.ShapeDtypeStruct((n_rows, d), BF16)],
        scratch_shapes=[pltpu.VMEM((2, 2, tt, d // 2), U32), pltpu.SemaphoreType.DMA((2,))],
        compiler_params=_cparams(("arbitrary",)), name="moe_combine")(
            dest, dest, gate_t, h, ln_g.reshape(1, d), ln_b.reshape(1, d), yb)


def moe_layer(h, hb, router_w, router_b, w_gate, w_up, w_down, layer, ln_g, ln_b, alpha, n_rows_out):
    t, d = h.shape
    e_idx, gate, rank, counts = moe_route(h, router_w, router_b)
    padded = (counts + MOE_BLOCK - 1) // MOE_BLOCK * MOE_BLOCK
    pends = jnp.cumsum(padded)
    pstart = pends - padded
    experts = jnp.arange(N_EXPERTS, dtype=I32)[:, None, None]
    dest = jnp.sum(jnp.where(e_idx[None] == experts, pstart[:, None, None], 0), axis=0) + rank
    nblk = -(-(2 * t) // MOE_BLOCK) + N_EXPERTS
    n_used = (pends[-1] // MOE_BLOCK).astype(I32)
    blk = jnp.arange(nblk, dtype=I32)
    blk_first_row = jnp.minimum(blk, n_used - 1) * MOE_BLOCK
    blk_expert = jnp.minimum(jnp.sum((pends[None, :] <= blk_first_row[:, None]).astype(I32), axis=1),
                             N_EXPERTS - 1)
    zero_flag = ((blk >= n_used) | (blk == pends[blk_expert] // MOE_BLOCK - 1)).astype(I32)
    xb = moe_dispatch(hb, dest, zero_flag, nblk * MOE_BLOCK)
    yb = moe_experts(xb, blk_expert, n_used.reshape(1), w_gate, w_up, w_down, layer)
    return moe_combine(yb, dest, gate.T, h, ln_g, ln_b, alpha, n_rows_out)


def kernel(x, meta_tokens, ln_mix_g, ln_mix_b, ln_ffn_g, ln_ffn_b, conv_pw1_w, conv_pw1_b, conv_dw_w, conv_dw_b, conv_ln_g, conv_ln_b, conv_pw2_w, conv_pw2_b, mla_wdq, mla_q_norm_g, mla_wuq, mla_wdkv, mla_kv_norm_g, mla_wukv, mla_wo, hgrn_w_in, hgrn_lb_logits, hgrn_norm_g, hgrn_wo, router_w, router_b, moe_w_gate, moe_w_up, moe_w_down):
    nb, s_len, d = x.shape
    nm = meta_tokens.shape[0]
    depth = ln_mix_g.shape[0]
    alpha = float((2 * depth) ** 0.25)
    zero_bias = jnp.zeros((d,), F32)

    meta = jnp.broadcast_to(meta_tokens[None].astype(x.dtype), (nb, nm, d)).reshape(nb * nm, d)
    h = jnp.concatenate([x.reshape(nb * s_len, d), meta], axis=0)
    hb = h.astype(BF16)
    p_lb = jax.nn.softmax(hgrn_lb_logits.astype(F32), axis=0)
    lower_bounds = jnp.cumsum(p_lb, axis=0) - p_lb[0]

    for i in range(depth):
        j = i // N_MIXERS
        kind = i % N_MIXERS
        if kind == 0:
            u = mm_glu(hb, conv_pw1_w[j].astype(BF16), conv_pw1_b[j])
            mix_in = conv_ln_swish(u, conv_dw_w[j], conv_dw_b[j], conv_ln_g[j], conv_ln_b[j], nb, s_len, nm)
            w_out, b_out = conv_pw2_w[j], conv_pw2_b[j]
        elif kind == 1:
            mix_in = mla_mixer(hb, mla_wdq[j], mla_q_norm_g[j], mla_wuq[j], mla_wdkv[j], mla_kv_norm_g[j],
                               mla_wukv[j], nb, s_len, nm)
            w_out, b_out = mla_wo[j], zero_bias
        else:
            qfig = mm_plain(hb, hgrn_w_in[j].astype(BF16))
            mix_in = hgrn_mixer(qfig, lower_bounds[i], hgrn_norm_g[j], nb, s_len, nm)
            w_out, b_out = hgrn_wo[j], zero_bias
        h, hb = mm_res_ln(mix_in, h, w_out.astype(BF16), b_out, ln_mix_g[i], ln_mix_b[i], alpha)
        n_rows_out = nb * s_len if i == depth - 1 else h.shape[0]
        h, hb = moe_layer(h, hb, router_w, router_b, moe_w_gate, moe_w_up, moe_w_down, i,
                          ln_ffn_g[i], ln_ffn_b[i], alpha, n_rows_out)
    return h.reshape(nb, s_len, d)
```

```python
import functools

import jax
import jax.numpy as jnp
from jax import lax
from jax.experimental import pallas as pl
from jax.experimental.pallas import tpu as pltpu

F32 = jnp.float32
BF16 = jnp.bfloat16
I32 = jnp.int32
U32 = jnp.uint32

LANES = 128
CHUNK = 64
N_MIXERS = 3
CONV_WIDTH = 31
CONV_HALO = 32
CONV_LANES = 256
HEAD_DIM = 128
MLA_ROPE = 64
ROPE_THETA = 10000.0
HGRN_CHUNK = 128
N_EXPERTS = 32
N_GROUPS = 4
EXPERTS_PER_GROUP = N_EXPERTS // N_GROUPS
MOE_BLOCK = 256
COMBINE_ROWS = 16
LN_EPS = 1e-5
RMS_EPS = 1e-6
VMEM_LIMIT = 52 * 1024 * 1024


def _pick(n, cands):
    for c in cands:
        if n % c == 0:
            return c
    raise ValueError(f"no tile for {n} in {cands}")


_ROW_TILES = (768, 512, 384, 256, 192, 176, 128, 96, 64, 48, 32, 16)


def _cparams(sem, vmem=VMEM_LIMIT):
    return pltpu.CompilerParams(dimension_semantics=sem, vmem_limit_bytes=vmem)


def _layer_norm(x, g, b):
    mu = jnp.mean(x, axis=-1, keepdims=True)
    xc = x - mu
    var = jnp.mean(xc * xc, axis=-1, keepdims=True)
    return xc * lax.rsqrt(var + LN_EPS) * g + b


def _rms_norm(x, g):
    return x * lax.rsqrt(jnp.mean(x * x, axis=-1, keepdims=True) + RMS_EPS) * g


def _sigmoid(x):
    return 1.0 / (1.0 + jnp.exp(-x))


def _pack_bf16_pair(lo, hi):
    lo_b = lax.bitcast_convert_type(lo.astype(BF16).astype(F32), U32)
    hi_b = lax.bitcast_convert_type(hi.astype(BF16).astype(F32), U32)
    return (hi_b & jnp.uint32(0xFFFF0000)) | (lo_b >> 16)


def _unpack_bf16_pair(w):
    lo = lax.bitcast_convert_type(w << 16, F32)
    hi = lax.bitcast_convert_type(w & jnp.uint32(0xFFFF0000), F32)
    return lo, hi


def _write_meta_rows(meta_ref, o_ref):
    o_ref[...] = jnp.zeros(o_ref.shape, o_ref.dtype)
    o_ref[0:meta_ref.shape[0], :] = meta_ref[...]


def _rows_call(body, n_rows, tm, row_ins, full_ins, outs, name):
    grid = (n_rows // tm,)
    in_specs = [pl.BlockSpec((tm, a.shape[1]), lambda i: (i, 0)) for a in row_ins]
    in_specs += [pl.BlockSpec(a.shape, lambda i, nd=a.ndim: (0,) * nd) for a in full_ins]
    out_specs = [pl.BlockSpec((tm, n), lambda i: (i, 0)) for n, _ in outs]
    out_shape = [jax.ShapeDtypeStruct((n_rows, n), dt) for n, dt in outs]
    return pl.pallas_call(
        body, grid=grid, in_specs=in_specs, out_specs=out_specs, out_shape=out_shape,
        compiler_params=_cparams(("parallel",)), name=name)(*row_ins, *full_ins)


def _mm_res_ln_body(alpha, x_ref, h_ref, w_ref, bias_ref, g_ref, b_ref, ho_ref, hb_ref):
    acc = jnp.dot(x_ref[...], w_ref[...], preferred_element_type=F32)
    y = alpha * h_ref[...] + (acc + bias_ref[...])
    o = _layer_norm(y, g_ref[...], b_ref[...])
    ho_ref[...] = o
    hb_ref[...] = o.astype(BF16)


def mm_res_ln(x_bf, h, w_bf, bias, g, b, alpha):
    t, d = h.shape
    tm = _pick(t, (256, 192, 176, 128, 96, 64, 48, 32, 16))
    return _rows_call(functools.partial(_mm_res_ln_body, alpha), t, tm, [x_bf, h],
                      [w_bf, bias.reshape(1, d), g.reshape(1, d), b.reshape(1, d)],
                      [(d, F32), (d, BF16)], "mm_res_ln")


def _glu_body(x_ref, wa_ref, wg_ref, ba_ref, bg_ref, u_ref):
    x = x_ref[...]
    a = jnp.dot(x, wa_ref[...], preferred_element_type=F32) + ba_ref[...]
    g = jnp.dot(x, wg_ref[...], preferred_element_type=F32) + bg_ref[...]
    u_ref[...] = a * _sigmoid(g)


def mm_glu(x_bf, w_bf, bias):
    t, k = x_bf.shape
    d = w_bf.shape[1] // 2
    tm = _pick(t, _ROW_TILES)
    tn = min(d, 1024)
    nj = d // tn
    bias2 = bias.reshape(1, 2 * d)
    return pl.pallas_call(
        _glu_body, grid=(nj, t // tm),
        in_specs=[pl.BlockSpec((tm, k), lambda j, i: (i, 0)),
                  pl.BlockSpec((k, tn), lambda j, i: (0, j)),
                  pl.BlockSpec((k, tn), lambda j, i: (0, j + nj)),
                  pl.BlockSpec((1, tn), lambda j, i: (0, j)),
                  pl.BlockSpec((1, tn), lambda j, i: (0, j + nj))],
        out_specs=pl.BlockSpec((tm, tn), lambda j, i: (i, j)),
        out_shape=jax.ShapeDtypeStruct((t, d), F32),
        compiler_params=_cparams(("parallel", "parallel")), name="mm_glu")(x_bf, w_bf, w_bf, bias2, bias2)


def _mm_plain_body(x_ref, w_ref, o_ref):
    o_ref[...] = jnp.dot(x_ref[...], w_ref[...], preferred_element_type=F32).astype(o_ref.dtype)


def mm_plain(x_bf, w_bf, out_dtype=F32):
    t, k = x_bf.shape
    n = w_bf.shape[1]
    tm = _pick(t, _ROW_TILES)
    tn = min(n, 2048)
    return pl.pallas_call(
        _mm_plain_body, grid=(n // tn, t // tm),
        in_specs=[pl.BlockSpec((tm, k), lambda j, i: (i, 0)),
                  pl.BlockSpec((k, tn), lambda j, i: (0, j))],
        out_specs=pl.BlockSpec((tm, tn), lambda j, i: (i, j)),
        out_shape=jax.ShapeDtypeStruct((t, n), out_dtype),
        compiler_params=_cparams(("parallel", "parallel")), name="mm_plain")(x_bf, w_bf)


def _conv_rows(ext_ref, w_ref, y_ref, sh_ref, rows, rc, lc):
    d = y_ref.shape[1]
    shift = CONV_HALO - (CONV_WIDTH - 1)
    sub = 8
    n_sh = sh_ref.shape[1]

    def lane_body(c, carry):
        l0 = pl.multiple_of(c * lc, lc)
        lanes = pl.ds(l0, lc)
        wv = w_ref[:, lanes]
        for b in range(1, sub):
            sh_ref[b - 1] = ext_ref[pl.ds(b, n_sh), lanes]
        for r in range(rows // rc):
            acc = None
            for k in range(CONV_WIDTH):
                b = (shift + k) % sub
                a = r * rc + (shift + k) - b
                src = ext_ref[pl.ds(a, rc), lanes] if b == 0 else sh_ref[b - 1, pl.ds(a, rc), :]
                term = src * wv[k:k + 1, :]
                acc = term if acc is None else acc + term
            y_ref[pl.ds(r * rc, rc), lanes] = acc
        return carry

    lax.fori_loop(0, d // lc, lane_body, 0)


def _conv_epilogue(y_ref, dwb_ref, g_ref, b_ref, o_ref):
    rows = y_ref.shape[0]
    step = 16

    def body(c, carry):
        sl = pl.ds(pl.multiple_of(c * step, step), step)
        z = _layer_norm(y_ref[sl, :] + dwb_ref[...], g_ref[...], b_ref[...])
        o_ref[sl, :] = (z * _sigmoid(z)).astype(BF16)
        return carry

    lax.fori_loop(0, rows // step, body, 0, unroll=min(4, rows // step))


def _conv_main_body(tr, nst, n_real, cur_ref, prev_ref, meta_ref, w_ref, dwb_ref, g_ref, b_ref, vmeta_ref, o_ref,
                    ext_ref, y_ref, sh_ref):
    step = pl.program_id(0)
    nm = meta_ref.shape[0]

    @pl.when(step < n_real)
    def _():
        @pl.when(step % nst == 0)
        def _():
            ext_ref[0:CONV_HALO - nm, :] = jnp.zeros((CONV_HALO - nm, ext_ref.shape[1]), F32)
            ext_ref[CONV_HALO - nm:CONV_HALO, :] = meta_ref[...]

        @pl.when(step % nst > 0)
        def _():
            ext_ref[0:CONV_HALO, :] = prev_ref[...]

        ext_ref[CONV_HALO:CONV_HALO + tr, :] = cur_ref[...]
        _conv_rows(ext_ref, w_ref, y_ref, sh_ref, tr, 64, CONV_LANES)
        _conv_epilogue(y_ref, dwb_ref, g_ref, b_ref, o_ref)

    @pl.when(step == n_real)
    def _():
        _write_meta_rows(vmeta_ref, o_ref)


def _conv_meta_body(meta_ref, w_ref, dwb_ref, g_ref, b_ref, o_ref, ext_ref, y_ref, sh_ref):
    nm = meta_ref.shape[0]
    ext_ref[0:CONV_HALO, :] = jnp.zeros((CONV_HALO, ext_ref.shape[1]), F32)
    ext_ref[CONV_HALO:CONV_HALO + nm, :] = meta_ref[...]
    _conv_rows(ext_ref, w_ref, y_ref, sh_ref, nm, nm, CONV_LANES)
    _conv_epilogue(y_ref, dwb_ref, g_ref, b_ref, o_ref)


def conv_ln_swish(u, dw_w, dw_b, ln_g, ln_b, nb, s_len, nm):
    t, d = u.shape
    tr_rows = nb * s_len
    tr = _pick(s_len, (512, 256))
    w_pad = jnp.concatenate([dw_w, jnp.zeros((CONV_HALO - CONV_WIDTH, d), F32)], axis=0)
    vecs = [dw_b.reshape(1, d), ln_g.reshape(1, d), ln_b.reshape(1, d)]
    meta_blk = tr_rows // nm
    out_meta = pl.pallas_call(
        _conv_meta_body, grid=(nb,),
        in_specs=[pl.BlockSpec((nm, d), lambda b: (meta_blk + b, 0)),
                  pl.BlockSpec((CONV_HALO, d), lambda b: (0, 0))]
        + [pl.BlockSpec((1, d), lambda b: (0, 0))] * 3,
        out_specs=pl.BlockSpec((nm, d), lambda b: (b, 0)),
        out_shape=jax.ShapeDtypeStruct((nb * nm, d), BF16),
        scratch_shapes=[pltpu.VMEM((CONV_HALO + nm, d), F32), pltpu.VMEM((nm, d), F32),
                        pltpu.VMEM((7, CONV_HALO + nm - 8, CONV_LANES), F32)],
        compiler_params=_cparams(("parallel",)), name="conv_meta")(u, w_pad, *vecs)
    nst = s_len // tr
    n_real = nb * nst
    halo_per_tile = tr // CONV_HALO
    assert nb * nm <= tr

    def real(g):
        return jnp.minimum(g, n_real - 1)

    return pl.pallas_call(
        functools.partial(_conv_main_body, tr, nst, n_real), grid=(n_real + 1,),
        in_specs=[pl.BlockSpec((tr, d), lambda g: (real(g), 0)),
                  pl.BlockSpec((CONV_HALO, d), lambda g: (jnp.maximum(real(g) * halo_per_tile - 1, 0), 0)),
                  pl.BlockSpec((nm, d), lambda g: (meta_blk + real(g) // nst, 0)),
                  pl.BlockSpec((CONV_HALO, d), lambda g: (0, 0))]
        + [pl.BlockSpec((1, d), lambda g: (0, 0))] * 3
        + [pl.BlockSpec((nb * nm, d), lambda g: (0, 0))],
        out_specs=pl.BlockSpec((tr, d), lambda g: (g, 0)),
        out_shape=jax.ShapeDtypeStruct((t, d), BF16),
        scratch_shapes=[pltpu.VMEM((CONV_HALO + tr, d), F32), pltpu.VMEM((tr, d), F32),
                        pltpu.VMEM((7, CONV_HALO + tr - 8, CONV_LANES), F32)],
        compiler_params=_cparams(("arbitrary",)), name="conv_main")(u, u, u, w_pad, *vecs, out_meta)


def _mla_proj_body(ql, kvl, hd, scale, x_ref, cos_ref, sin_ref, w1_ref, qg_ref, kvg_ref, wq_ref, wkv_ref,
                   qn_ref, qr_ref, kn_ref, v_ref, kr_ref):
    a = jnp.dot(x_ref[...], w1_ref[...], preferred_element_type=F32)
    cos = cos_ref[...]
    sin = sin_ref[...]
    cq = _rms_norm(a[:, :ql], qg_ref[...]).astype(BF16)
    ckv = _rms_norm(a[:, ql:ql + kvl], kvg_ref[...]).astype(BF16)
    r0 = ql + kvl
    kr_ref[...] = (a[:, r0:r0 + LANES] * cos + a[:, r0 + LANES:r0 + 2 * LANES] * sin).astype(BF16)
    qa = jnp.dot(cq, wq_ref[...], preferred_element_type=F32)
    nh = hd // LANES
    cos_t = jnp.tile(cos, (1, nh))
    sin_t = jnp.tile(sin, (1, nh))
    qn_ref[...] = (qa[:, :hd] * scale).astype(BF16)
    qr_ref[...] = ((qa[:, hd:2 * hd] * cos_t + qa[:, 2 * hd:] * sin_t) * scale).astype(BF16)
    kv = jnp.dot(ckv, wkv_ref[...], preferred_element_type=F32)
    kn_ref[...] = kv[:, :hd].astype(BF16)
    v_ref[...] = kv[:, hd:].astype(BF16)


def _attn_body(tq, nb, qn_ref, qr_ref, kn_ref, kr_ref, v_ref, knm_ref, krm_ref, vm_ref, ometa_ref, o_ref, kf_ref):
    @pl.when(pl.program_id(0) == nb)
    def _():
        _write_meta_rows(ometa_ref, o_ref)

    @pl.when(pl.program_id(0) < nb)
    def _():
        _attn_tiles(tq, qn_ref, qr_ref, kn_ref, kr_ref, v_ref, knm_ref, krm_ref, vm_ref, o_ref, kf_ref)


def _attn_tiles(tq, qn_ref, qr_ref, kn_ref, kr_ref, v_ref, knm_ref, krm_ref, vm_ref, o_ref, kf_ref):
    s_len = qn_ref.shape[0]
    nt = (((1,), (1,)), ((), ()))
    kf_ref[:, :LANES] = kn_ref[...]
    kf_ref[:, LANES:] = kr_ref[...]
    km = jnp.concatenate([knm_ref[...], krm_ref[...]], axis=-1)
    vm = vm_ref[...]
    row_c = lax.broadcasted_iota(I32, (tq, tq), 0) // CHUNK
    col_c = lax.broadcasted_iota(I32, (tq, tq), 1) // CHUNK
    visible = col_c <= row_c
    for i in range(s_len // tq):
        r0 = i * tq
        q = jnp.concatenate([qn_ref[r0:r0 + tq, :], qr_ref[r0:r0 + tq, :]], axis=-1)
        s_m = lax.dot_general(q, km, nt, preferred_element_type=F32)
        s_d = lax.dot_general(q, kf_ref[r0:r0 + tq, :], nt, preferred_element_type=F32)
        s_d = jnp.where(visible, s_d, -jnp.inf)
        m = jnp.maximum(jnp.max(s_m, axis=-1, keepdims=True), jnp.max(s_d, axis=-1, keepdims=True))
        if i > 0:
            s_p = lax.dot_general(q, kf_ref[0:r0, :], nt, preferred_element_type=F32)
            m = jnp.maximum(m, jnp.max(s_p, axis=-1, keepdims=True))
        p_m = jnp.exp(s_m - m)
        p_d = jnp.exp(s_d - m)
        l = jnp.sum(p_m, axis=-1, keepdims=True) + jnp.sum(p_d, axis=-1, keepdims=True)
        acc = jnp.dot(p_m.astype(BF16), vm, preferred_element_type=F32)
        acc = acc + jnp.dot(p_d.astype(BF16), v_ref[r0:r0 + tq, :], preferred_element_type=F32)
        if i > 0:
            p_p = jnp.exp(s_p - m)
            l = l + jnp.sum(p_p, axis=-1, keepdims=True)
            acc = acc + jnp.dot(p_p.astype(BF16), v_ref[0:r0, :], preferred_element_type=F32)
        o_ref[r0:r0 + tq, :] = (acc / l).astype(BF16)


def _attn_meta_body(qn_ref, qr_ref, kn_ref, kr_ref, v_ref, o_ref):
    nt = (((1,), (1,)), ((), ()))
    q = jnp.concatenate([qn_ref[...], qr_ref[...]], axis=-1)
    k = jnp.concatenate([kn_ref[...], kr_ref[...]], axis=-1)
    s = lax.dot_general(q, k, nt, preferred_element_type=F32)
    p = jnp.exp(s - jnp.max(s, axis=-1, keepdims=True))
    l = jnp.sum(p, axis=-1, keepdims=True)
    o_ref[...] = (jnp.dot(p.astype(BF16), v_ref[...], preferred_element_type=F32) / l).astype(BF16)


def _rope_rows(n_pos):
    inv = ROPE_THETA ** (-jnp.arange(0, MLA_ROPE, 2, dtype=F32) / MLA_ROPE)
    ang = jnp.arange(n_pos, dtype=F32)[:, None] * inv[None, :]
    pad = jnp.zeros((n_pos, LANES - MLA_ROPE), F32)
    cos = jnp.concatenate([jnp.cos(ang), jnp.cos(ang), pad], axis=1)
    sin = jnp.concatenate([jnp.sin(ang), jnp.sin(ang), pad], axis=1)
    return cos, sin


def _pad_rope_cols(w):
    half = MLA_ROPE // 2
    z = jnp.zeros((w.shape[0], LANES - MLA_ROPE), w.dtype)
    rot = jnp.concatenate([-w[:, half:], w[:, :half]], axis=1)
    return jnp.concatenate([w, z], axis=1), jnp.concatenate([rot, z], axis=1)


def mla_mixer(hb, wdq, q_norm_g, wuq, wdkv, kv_norm_g, wukv, nb, s_len, nm):
    t, d = hb.shape
    ql = wdq.shape[1]
    kvl = kv_norm_g.shape[0]
    nh = wuq.shape[1] // (HEAD_DIM + MLA_ROPE)
    hd = nh * HEAD_DIM
    tr_rows = nb * s_len
    scale = float((HEAD_DIM + MLA_ROPE) ** -0.5)
    kr_w, kr_rot = _pad_rope_cols(wdkv[:, kvl:])
    w1 = jnp.concatenate([wdq, wdkv[:, :kvl], kr_w, kr_rot], axis=1).astype(BF16)
    wuq3 = wuq.reshape(ql, nh, HEAD_DIM + MLA_ROPE)
    q_rope = wuq3[:, :, HEAD_DIM:]
    half = MLA_ROPE // 2
    zq = jnp.zeros((ql, nh, LANES - MLA_ROPE), F32)
    q_rope_p = jnp.concatenate([q_rope, zq], axis=2).reshape(ql, hd)
    q_rot_p = jnp.concatenate([-q_rope[:, :, half:], q_rope[:, :, :half], zq], axis=2).reshape(ql, hd)
    wq = jnp.concatenate([wuq3[:, :, :HEAD_DIM].reshape(ql, hd), q_rope_p, q_rot_p], axis=1).astype(BF16)
    wukv3 = wukv.reshape(kvl, nh, 2 * HEAD_DIM)
    wkv = jnp.concatenate([wukv3[:, :, :HEAD_DIM].reshape(kvl, hd),
                           wukv3[:, :, HEAD_DIM:].reshape(kvl, hd)], axis=1).astype(BF16)
    cos_p, sin_p = _rope_rows(nm + s_len)
    cos_rows = jnp.concatenate([jnp.tile(cos_p[nm:], (nb, 1)), jnp.tile(cos_p[:nm], (nb, 1))], axis=0)
    sin_rows = jnp.concatenate([jnp.tile(sin_p[nm:], (nb, 1)), jnp.tile(sin_p[:nm], (nb, 1))], axis=0)

    tm = _pick(t, (128, 96, 64, 48, 32, 16))
    qn, qr, kn, v, kr = _rows_call(
        functools.partial(_mla_proj_body, ql, kvl, hd, scale), t, tm, [hb, cos_rows, sin_rows],
        [w1, q_norm_g.reshape(1, ql), kv_norm_g.reshape(1, kvl), wq, wkv],
        [(hd, BF16), (hd, BF16), (hd, BF16), (hd, BF16), (LANES, BF16)], "mla_proj")

    meta_blk = tr_rows // nm
    mspec = lambda: pl.BlockSpec((nm, LANES), lambda b, h: (meta_blk + b, h))
    mspec0 = lambda: pl.BlockSpec((nm, LANES), lambda b, h: (meta_blk + b, 0))
    o_meta = pl.pallas_call(
        _attn_meta_body, grid=(nb, nh),
        in_specs=[mspec(), mspec(), mspec(), mspec0(), mspec()],
        out_specs=pl.BlockSpec((nm, LANES), lambda b, h: (b, h)),
        out_shape=jax.ShapeDtypeStruct((nb * nm, hd), BF16),
        compiler_params=_cparams(("parallel", "parallel")), name="attn_meta")(qn, qr, kn, kr, v)

    tq = 256
    assert nb * nm <= s_len

    def real(b):
        return jnp.minimum(b, nb - 1)

    kspec = lambda: pl.BlockSpec((s_len, LANES), lambda b, h: (real(b), h))
    kspec0 = lambda: pl.BlockSpec((s_len, LANES), lambda b, h: (real(b), 0))
    m2 = lambda: pl.BlockSpec((nm, LANES), lambda b, h: (meta_blk + real(b), h))
    m20 = lambda: pl.BlockSpec((nm, LANES), lambda b, h: (meta_blk + real(b), 0))
    return pl.pallas_call(
        functools.partial(_attn_body, tq, nb), grid=(nb + 1, nh),
        in_specs=[kspec(), kspec(), kspec(), kspec0(), kspec(), m2(), m20(), m2(),
                  pl.BlockSpec((nb * nm, LANES), lambda b, h: (0, h))],
        out_specs=pl.BlockSpec((s_len, LANES), lambda b, h: (b, h)),
        out_shape=jax.ShapeDtypeStruct((t, hd), BF16),
        scratch_shapes=[pltpu.VMEM((s_len, 2 * LANES), BF16)],
        compiler_params=_cparams(("arbitrary", "arbitrary")), name="attn_main")(
            qn, qr, kn, kr, v, kn, kr, v, o_meta)


def _hgrn_gates(fz, lb):
    return lb + (1.0 - lb) * _sigmoid(fz), (1.0 - lb) * _sigmoid(-fz)


def _hgrn_level_index(c):
    t = lax.broadcasted_iota(I32, (c, c), 0)
    s = lax.broadcasted_iota(I32, (c, c), 1)
    lvl = 31 - lax.clz(t ^ s)
    return jnp.where(s > t, -2, lvl)


def _hgrn_block(q, f, k, iv, st, lvl):
    c = q.shape[0]
    nt = (((1,), (1,)), ((), ()))
    tn = (((0,), (0,)), ((), ()))
    row = lax.broadcasted_iota(I32, (c, HEAD_DIM), 0)
    scores = jnp.where(lvl == -1, lax.dot_general(q.astype(BF16), k.astype(BF16), nt,
                                                  preferred_element_type=F32), 0.0)
    qa = q * f
    kb = k
    tot = f
    h, idx = 1, 0
    while h < c:
        prod = lax.dot_general(qa.astype(BF16), kb.astype(BF16), nt, preferred_element_type=F32)
        scores = jnp.where(lvl == idx, prod, scores)
        right = (row & h) != 0
        left_tot = pltpu.roll(tot, h, 0)
        right_tot = pltpu.roll(tot, c - h, 0)
        qa = qa * jnp.where(right, left_tot, 1.0)
        kb = kb * jnp.where(right, 1.0, right_tot)
        tot = tot * jnp.where(right, left_tot, right_tot)
        h, idx = 2 * h, idx + 1
    ib = iv.astype(BF16)
    o = jnp.dot(scores.astype(BF16), ib, preferred_element_type=F32)
    o = o + lax.dot_general(qa.astype(BF16), st.astype(BF16), nt, preferred_element_type=F32)
    st_new = st * tot[0:1, :] + lax.dot_general(ib, kb.astype(BF16), tn, preferred_element_type=F32)
    return o, st_new


def _hgrn_out(o, gate, ng):
    o = o * lax.rsqrt(jnp.mean(o * o, axis=-1, keepdims=True) + RMS_EPS) * ng
    return (o * (gate * _sigmoid(gate))).astype(BF16)


def _hgrn_meta_body(gh, q_ref, fz_ref, i_ref, g_ref, lb_ref, ng_ref, o_ref, st_ref):
    nm = q_ref.shape[0]
    c = HGRN_CHUNK
    lvl = _hgrn_level_index(c)
    zeros = jnp.zeros((c - nm, HEAD_DIM), F32)
    for g in range(gh):
        sl = slice(g * HEAD_DIM, (g + 1) * HEAD_DIM)
        f, k = _hgrn_gates(fz_ref[:, sl], lb_ref[:, sl])
        o, st = _hgrn_block(jnp.concatenate([zeros, q_ref[:, sl]], axis=0),
                            jnp.concatenate([zeros + 1.0, f], axis=0),
                            jnp.concatenate([zeros, k], axis=0),
                            jnp.concatenate([zeros, i_ref[:, sl]], axis=0),
                            jnp.zeros((HEAD_DIM, HEAD_DIM), F32), lvl)
        st_ref[0, g] = st
        o_ref[:, sl] = _hgrn_out(o[c - nm:], g_ref[:, sl], ng_ref[...])


def _hgrn_main_body(gh, ts, nst, n_real, q_ref, fz_ref, i_ref, g_ref, lb_ref, ng_ref, st0_ref, ometa_ref, o_ref,
                    st_ref):
    c = HGRN_CHUNK
    step = pl.program_id(1)

    @pl.when(step == n_real)
    def _():
        _write_meta_rows(ometa_ref, o_ref)

    @pl.when(step < n_real)
    def _():
        lvl = _hgrn_level_index(c)

        @pl.when(step % nst == 0)
        def _():
            st_ref[...] = st0_ref[0]

        def chunk_body(cidx, carry):
            rows = pl.ds(pl.multiple_of(cidx * c, c), c)
            for g in range(gh):
                sl = slice(g * HEAD_DIM, (g + 1) * HEAD_DIM)
                f, k = _hgrn_gates(fz_ref[rows, sl], lb_ref[:, sl])
                o, st = _hgrn_block(q_ref[rows, sl], f, k, i_ref[rows, sl], st_ref[g], lvl)
                st_ref[g] = st
                o_ref[rows, sl] = _hgrn_out(o, g_ref[rows, sl], ng_ref[...])
            return carry

        lax.fori_loop(0, ts // c, chunk_body, 0)


def hgrn_mixer(qfig, lb, norm_g, nb, s_len, nm):
    t, d4 = qfig.shape
    d = d4 // 4
    nh = d // HEAD_DIM
    gh = 4 if nh % 4 == 0 else 1
    gw = gh * HEAD_DIM
    ng_blocks = d // gw
    tr_rows = nb * s_len
    meta_blk = tr_rows // nm
    lb2 = lb.reshape(1, d)
    ng2 = norm_g.reshape(1, HEAD_DIM)

    def mspec(sec):
        return pl.BlockSpec((nm, gw), lambda b, h: (meta_blk + b, sec * ng_blocks + h))

    o_meta, st0 = pl.pallas_call(
        functools.partial(_hgrn_meta_body, gh), grid=(nb, ng_blocks),
        in_specs=[mspec(0), mspec(1), mspec(2), mspec(3),
                  pl.BlockSpec((1, gw), lambda b, h: (0, h)),
                  pl.BlockSpec((1, HEAD_DIM), lambda b, h: (0, 0))],
        out_specs=[pl.BlockSpec((nm, gw), lambda b, h: (b, h)),
                   pl.BlockSpec((1, gh, HEAD_DIM, HEAD_DIM), lambda b, h: (b, h, 0, 0))],
        out_shape=[jax.ShapeDtypeStruct((nb * nm, d), BF16),
                   jax.ShapeDtypeStruct((nb, nh, HEAD_DIM, HEAD_DIM), F32)],
        compiler_params=_cparams(("parallel", "parallel")), name="hgrn_meta")(qfig, qfig, qfig, qfig, lb2, ng2)

    ts = _pick(s_len, (512, 256))
    nst = s_len // ts
    n_real = nb * nst
    assert nb * nm <= ts

    def real(g):
        return jnp.minimum(g, n_real - 1)

    def rspec(sec):
        return pl.BlockSpec((ts, gw), lambda h, g: (real(g), sec * ng_blocks + h))

    return pl.pallas_call(
        functools.partial(_hgrn_main_body, gh, ts, nst, n_real), grid=(ng_blocks, n_real + 1),
        in_specs=[rspec(0), rspec(1), rspec(2), rspec(3),
                  pl.BlockSpec((1, gw), lambda h, g: (0, h)),
                  pl.BlockSpec((1, HEAD_DIM), lambda h, g: (0, 0)),
                  pl.BlockSpec((1, gh, HEAD_DIM, HEAD_DIM), lambda h, g: (real(g) // nst, h, 0, 0)),
                  pl.BlockSpec((nb * nm, gw), lambda h, g: (0, h))],
        out_specs=pl.BlockSpec((ts, gw), lambda h, g: (g, h)),
        out_shape=jax.ShapeDtypeStruct((t, d), BF16),
        scratch_shapes=[pltpu.VMEM((gh, HEAD_DIM, HEAD_DIM), F32)],
        compiler_params=_cparams(("parallel", "arbitrary")), name="hgrn_main")(
            qfig, qfig, qfig, qfig, lb2, ng2, st0, o_meta)


def _first_index_of_max(vals, idx, n, axis):
    mx = jnp.max(vals, axis=axis, keepdims=True)
    first = jnp.min(jnp.where(vals == mx, idx, n), axis=axis, keepdims=True)
    return mx, first


def _router_body(tm, h_ref, rw_ref, rb_ref, tri_ref, e_ref, gate_ref, rank_ref, cnt_ref, base_ref):
    @pl.when(pl.program_id(0) == 0)
    def _():
        base_ref[...] = jnp.zeros(base_ref.shape, F32)

    nt = (((1,), (1,)), ((), ()))
    logits = lax.dot_general(rw_ref[...], h_ref[...], nt, precision=lax.Precision.HIGHEST,
                             preferred_element_type=F32)
    scores = _sigmoid(logits)
    sel = scores + rb_ref[...]
    g, epg = N_GROUPS, EXPERTS_PER_GROUP
    sel3 = sel.reshape(g, epg, tm)
    sc3 = scores.reshape(g, epg, tm)
    idx3 = lax.broadcasted_iota(I32, (g, epg, tm), 1)
    m1, i1 = _first_index_of_max(sel3, idx3, epg, 1)
    rest = jnp.where(idx3 == i1, -jnp.inf, sel3)
    m2, i2 = _first_index_of_max(rest, idx3, epg, 1)
    gidx = lax.broadcasted_iota(I32, (g, 1, tm), 0)
    _, gtop3 = _first_index_of_max(m1 + m2, gidx, g, 0)
    pick = gidx == gtop3
    gtop = gtop3[0]
    l1 = jnp.sum(jnp.where(pick, i1, 0), axis=0)
    l2 = jnp.sum(jnp.where(pick, i2, 0), axis=0)
    sc_in = jnp.sum(jnp.where(pick, sc3, 0.0), axis=0)
    idx2 = lax.broadcasted_iota(I32, (epg, tm), 0)
    s1 = jnp.sum(jnp.where(idx2 == l1, sc_in, 0.0), axis=0, keepdims=True)
    s2 = jnp.sum(jnp.where(idx2 == l2, sc_in, 0.0), axis=0, keepdims=True)
    e1 = gtop * epg + l1
    e2 = gtop * epg + l2
    e_ref[0:1, :] = e1
    e_ref[1:2, :] = e2
    den = s1 + s2
    gate_ref[0:1, :] = s1 / den
    gate_ref[1:2, :] = s2 / den
    eidx = lax.broadcasted_iota(I32, (N_EXPERTS, tm), 0)
    oh1 = (eidx == e1).astype(F32)
    oh2 = (eidx == e2).astype(F32)
    oh = jnp.concatenate([oh1, oh2], axis=0).astype(BF16)
    pre = jnp.dot(oh, tri_ref[...], preferred_element_type=F32)
    base = base_ref[:, 0:1]
    tot1 = jnp.sum(oh1, axis=1, keepdims=True)
    tot2 = jnp.sum(oh2, axis=1, keepdims=True)
    r1 = jnp.sum(oh1 * (base + pre[:N_EXPERTS]), axis=0, keepdims=True)
    r2 = jnp.sum(oh2 * (base + tot1 + pre[N_EXPERTS:]), axis=0, keepdims=True)
    rank_ref[0:1, :] = r1.astype(I32)
    rank_ref[1:2, :] = r2.astype(I32)
    new_base = jnp.broadcast_to(base + tot1 + tot2, base_ref.shape)
    base_ref[...] = new_base
    cnt_ref[...] = new_base


def moe_route(h, router_w, router_b):
    t, d = h.shape
    tm = _pick(t, (256, 128, 96, 64, 32))
    tri = (lax.broadcasted_iota(I32, (tm, tm), 0) < lax.broadcasted_iota(I32, (tm, tm), 1)).astype(BF16)
    e, gate, rank, cnt = pl.pallas_call(
        functools.partial(_router_body, tm), grid=(t // tm,),
        in_specs=[pl.BlockSpec((tm, d), lambda i: (i, 0)),
                  pl.BlockSpec((N_EXPERTS, d), lambda i: (0, 0)),
                  pl.BlockSpec((N_EXPERTS, 1), lambda i: (0, 0)),
                  pl.BlockSpec((tm, tm), lambda i: (0, 0))],
        out_specs=[pl.BlockSpec((2, tm), lambda i: (0, i)),
                   pl.BlockSpec((2, tm), lambda i: (0, i)),
                   pl.BlockSpec((2, tm), lambda i: (0, i)),
                   pl.BlockSpec((N_EXPERTS, LANES), lambda i: (0, 0))],
        out_shape=[jax.ShapeDtypeStruct((2, t), I32), jax.ShapeDtypeStruct((2, t), F32),
                   jax.ShapeDtypeStruct((2, t), I32), jax.ShapeDtypeStruct((N_EXPERTS, LANES), F32)],
        scratch_shapes=[pltpu.VMEM((N_EXPERTS, LANES), F32)],
        compiler_params=_cparams(("arbitrary",)), name="moe_route")(
            h, router_w.T, router_b.reshape(N_EXPERTS, 1), tri)
    return e, gate, rank, cnt[:, 0].astype(I32)


def _dispatch_body(tt, zero_flag_ref, dest_ref, hb_ref, xb_hbm, stage_ref, zero_ref, sem):
    half = stage_ref.shape[1]

    @pl.when(pl.program_id(0) == 0)
    def _():
        zero_ref[...] = jnp.zeros(zero_ref.shape, U32)

        def zfill(b, carry):
            @pl.when(zero_flag_ref[b] > 0)
            def _():
                start = pl.multiple_of(b * MOE_BLOCK, MOE_BLOCK)
                cp = pltpu.make_async_copy(zero_ref, xb_hbm.at[pl.ds(start, MOE_BLOCK), :], sem)
                cp.start()
                cp.wait()
            return carry

        lax.fori_loop(0, zero_flag_ref.shape[0], zfill, 0)

    stage_ref[...] = _pack_bf16_pair(hb_ref[:, :half].astype(F32), hb_ref[:, half:].astype(F32))

    def issue(j, carry):
        for k in range(2):
            pltpu.make_async_copy(stage_ref.at[pl.ds(j, 1), :],
                                  xb_hbm.at[pl.ds(dest_ref[k, j], 1), :], sem).start(priority=k)
        return carry

    lax.fori_loop(0, tt, issue, 0, unroll=8)
    for k in range(2):
        pltpu.make_async_copy(stage_ref, xb_hbm.at[pl.ds(0, tt), :], sem).wait()


def moe_dispatch(hb, dest, zero_flag, n_slots):
    t, d = hb.shape
    tt = _pick(t, (256, 128, 96, 64, 32))
    return pl.pallas_call(
        functools.partial(_dispatch_body, tt),
        grid_spec=pltpu.PrefetchScalarGridSpec(
            num_scalar_prefetch=1, grid=(t // tt,),
            in_specs=[pl.BlockSpec((2, tt), lambda i, zf: (0, i), memory_space=pltpu.SMEM),
                      pl.BlockSpec((tt, d), lambda i, zf: (i, 0))],
            out_specs=pl.BlockSpec(memory_space=pl.ANY),
            scratch_shapes=[pltpu.VMEM((tt, d // 2), U32), pltpu.VMEM((MOE_BLOCK, d // 2), U32),
                            pltpu.SemaphoreType.DMA(())]),
        out_shape=jax.ShapeDtypeStruct((n_slots, d // 2), U32),
        compiler_params=_cparams(("arbitrary",)), name="moe_dispatch")(zero_flag, dest, hb)


def _expert_body(be_ref, nu_ref, x_ref, wg_ref, wu_ref, wd_ref, y_ref, wgb_ref, wub_ref, wdb_ref):
    i = pl.program_id(0)
    used = i < nu_ref[0]
    half = x_ref.shape[1]

    @pl.when(jnp.logical_or(i == 0, be_ref[i] != be_ref[jnp.maximum(i - 1, 0)]))
    def _():
        wgb_ref[...] = wg_ref[0, 0].astype(BF16)
        wub_ref[...] = wu_ref[0, 0].astype(BF16)
        wdb_ref[...] = wd_ref[0, 0].astype(BF16)

    @pl.when(used)
    def _():
        lo, hi = _unpack_bf16_pair(x_ref[...])
        lo = lo.astype(BF16)
        hi = hi.astype(BF16)
        g = (jnp.dot(lo, wgb_ref[:half, :], preferred_element_type=F32)
             + jnp.dot(hi, wgb_ref[half:, :], preferred_element_type=F32))
        u = (jnp.dot(lo, wub_ref[:half, :], preferred_element_type=F32)
             + jnp.dot(hi, wub_ref[half:, :], preferred_element_type=F32))
        a = (g * _sigmoid(g) * u).astype(BF16)
        y = jnp.dot(a, wdb_ref[...], preferred_element_type=F32)
        y_ref[...] = _pack_bf16_pair(y[:, :half], y[:, half:])

    @pl.when(jnp.logical_not(used))
    def _():
        y_ref[...] = jnp.zeros(y_ref.shape, U32)


def moe_experts(xb, blk_expert, n_used, wg, wu, wd, layer):
    n_slots, half = xb.shape
    d = 2 * half
    ff = wg.shape[3]
    nblk = n_slots // MOE_BLOCK
    return pl.pallas_call(
        _expert_body,
        grid_spec=pltpu.PrefetchScalarGridSpec(
            num_scalar_prefetch=2, grid=(nblk,),
            in_specs=[pl.BlockSpec((MOE_BLOCK, half), lambda i, be, nu: (i, 0)),
                      pl.BlockSpec((1, 1, d, ff), lambda i, be, nu: (layer, be[i], 0, 0)),
                      pl.BlockSpec((1, 1, d, ff), lambda i, be, nu: (layer, be[i], 0, 0)),
                      pl.BlockSpec((1, 1, ff, d), lambda i, be, nu: (layer, be[i], 0, 0))],
            out_specs=pl.BlockSpec((MOE_BLOCK, half), lambda i, be, nu: (i, 0)),
            scratch_shapes=[pltpu.VMEM((d, ff), BF16), pltpu.VMEM((d, ff), BF16), pltpu.VMEM((ff, d), BF16)]),
        out_shape=jax.ShapeDtypeStruct((n_slots, half), U32),
        compiler_params=_cparams(("arbitrary",)), name="moe_experts")(
            blk_expert, n_used, xb, wg, wu, wd)


def _combine_body(tt, alpha, n_tiles, dest_ref, dnext_ref, gate_ref, h_ref, g_ref, b_ref, y_hbm, ho_ref, hb_ref,
                  buf_ref, sem):
    i = pl.program_id(0)
    slot = i % 2
    nslot = 1 - slot

    def gather(idx_ref, j, to_slot):
        for k in range(2):
            pltpu.make_async_copy(y_hbm.at[pl.ds(idx_ref[k, j], 1), :],
                                  buf_ref.at[to_slot, k, pl.ds(j, 1), :], sem.at[to_slot]).start(priority=k)

    def wait_slot(s):
        for k in range(2):
            pltpu.make_async_copy(y_hbm.at[pl.ds(0, tt), :], buf_ref.at[s, k], sem.at[s]).wait()

    @pl.when(i == 0)
    def _():
        def first(j, carry):
            gather(dest_ref, j, 0)
            return carry
        lax.fori_loop(0, tt, first, 0, unroll=8)

    wait_slot(slot)

    def ahead(j, carry):
        gather(dnext_ref, j, nslot)
        return carry

    lax.fori_loop(0, tt, ahead, 0, unroll=8)

    def rows_body(c, carry):
        r0 = pl.multiple_of(c * COMBINE_ROWS, COMBINE_ROWS)
        rows = pl.ds(r0, COMBINE_ROWS)
        gate = gate_ref[rows, :]
        lo0, hi0 = _unpack_bf16_pair(buf_ref[slot, 0, rows, :])
        lo1, hi1 = _unpack_bf16_pair(buf_ref[slot, 1, rows, :])
        ffn = jnp.concatenate([gate[:, 0:1] * lo0 + gate[:, 1:2] * lo1,
                               gate[:, 0:1] * hi0 + gate[:, 1:2] * hi1], axis=-1)
        o = _layer_norm(alpha * h_ref[rows, :] + ffn, g_ref[...], b_ref[...])
        ho_ref[rows, :] = o
        hb_ref[rows, :] = o.astype(BF16)
        return carry

    lax.fori_loop(0, tt // COMBINE_ROWS, rows_body, 0, unroll=4)

    @pl.when(i == n_tiles - 1)
    def _():
        wait_slot(nslot)


def moe_combine(yb, dest, gate_t, h, ln_g, ln_b, alpha, n_rows):
    t, d = h.shape
    tt = next(c for c in (256, 128, 96, 64, 32) if t % c == 0 and n_rows % c == 0)
    n_tiles = n_rows // tt
    return pl.pallas_call(
        functools.partial(_combine_body, tt, alpha, n_tiles), grid=(n_tiles,),
        in_specs=[pl.BlockSpec((2, tt), lambda i: (0, i), memory_space=pltpu.SMEM),
                  pl.BlockSpec((2, tt), lambda i: (0, jnp.minimum(i + 1, n_tiles - 1)), memory_space=pltpu.SMEM),
                  pl.BlockSpec((tt, 2), lambda i: (i, 0)),
                  pl.BlockSpec((tt, d), lambda i: (i, 0)),
                  pl.BlockSpec((1, d), lambda i: (0, 0)),
                  pl.BlockSpec((1, d), lambda i: (0, 0)),
                  pl.BlockSpec(memory_space=pl.ANY)],
        out_specs=[pl.BlockSpec((tt, d), lambda i: (i, 0)), pl.BlockSpec((tt, d), lambda i: (i, 0))],
        out_shape=[jax.ShapeDtypeStruct((n_rows, d), F32), jax.ShapeDtypeStruct((n_rows, d), BF16)],
        scratch_shapes=[pltpu.VMEM((2, 2, tt, d // 2), U32), pltpu.SemaphoreType.DMA((2,))],
        compiler_params=_cparams(("arbitrary",)), name="moe_combine")(
            dest, dest, gate_t, h, ln_g.reshape(1, d), ln_b.reshape(1, d), yb)


def moe_layer(h, hb, router_w, router_b, w_gate, w_up, w_down, layer, ln_g, ln_b, alpha, n_rows_out):
    t, d = h.shape
    e_idx, gate, rank, counts = moe_route(h, router_w, router_b)
    padded = (counts + MOE_BLOCK - 1) // MOE_BLOCK * MOE_BLOCK
    pends = jnp.cumsum(padded)
    pstart = pends - padded
    experts = jnp.arange(N_EXPERTS, dtype=I32)[:, None, None]
    dest = jnp.sum(jnp.where(e_idx[None] == experts, pstart[:, None, None], 0), axis=0) + rank
    nblk = -(-(2 * t) // MOE_BLOCK) + N_EXPERTS
    n_used = (pends[-1] // MOE_BLOCK).astype(I32)
    blk = jnp.arange(nblk, dtype=I32)
    blk_first_row = jnp.minimum(blk, n_used - 1) * MOE_BLOCK
    blk_expert = jnp.minimum(jnp.sum((pends[None, :] <= blk_first_row[:, None]).astype(I32), axis=1),
                             N_EXPERTS - 1)
    zero_flag = ((blk >= n_used) | (blk == pends[blk_expert] // MOE_BLOCK - 1)).astype(I32)
    xb = moe_dispatch(hb, dest, zero_flag, nblk * MOE_BLOCK)
    yb = moe_experts(xb, blk_expert, n_used.reshape(1), w_gate, w_up, w_down, layer)
    return moe_combine(yb, dest, gate.T, h, ln_g, ln_b, alpha, n_rows_out)


def kernel(x, meta_tokens, ln_mix_g, ln_mix_b, ln_ffn_g, ln_ffn_b, conv_pw1_w, conv_pw1_b, conv_dw_w, conv_dw_b, conv_ln_g, conv_ln_b, conv_pw2_w, conv_pw2_b, mla_wdq, mla_q_norm_g, mla_wuq, mla_wdkv, mla_kv_norm_g, mla_wukv, mla_wo, hgrn_w_in, hgrn_lb_logits, hgrn_norm_g, hgrn_wo, router_w, router_b, moe_w_gate, moe_w_up, moe_w_down):
    nb, s_len, d = x.shape
    nm = meta_tokens.shape[0]
    depth = ln_mix_g.shape[0]
    alpha = float((2 * depth) ** 0.25)
    zero_bias = jnp.zeros((d,), F32)

    meta = jnp.broadcast_to(meta_tokens[None].astype(x.dtype), (nb, nm, d)).reshape(nb * nm, d)
    h = jnp.concatenate([x.reshape(nb * s_len, d), meta], axis=0)
    hb = h.astype(BF16)
    p_lb = jax.nn.softmax(hgrn_lb_logits.astype(F32), axis=0)
    lower_bounds = jnp.cumsum(p_lb, axis=0) - p_lb[0]

    for i in range(depth):
        j = i // N_MIXERS
        kind = i % N_MIXERS
        if kind == 0:
            u = mm_glu(hb, conv_pw1_w[j].astype(BF16), conv_pw1_b[j])
            mix_in = conv_ln_swish(u, conv_dw_w[j], conv_dw_b[j], conv_ln_g[j], conv_ln_b[j], nb, s_len, nm)
            w_out, b_out = conv_pw2_w[j], conv_pw2_b[j]
        elif kind == 1:
            mix_in = mla_mixer(hb, mla_wdq[j], mla_q_norm_g[j], mla_wuq[j], mla_wdkv[j], mla_kv_norm_g[j],
                               mla_wukv[j], nb, s_len, nm)
            w_out, b_out = mla_wo[j], zero_bias
        else:
            qfig = mm_plain(hb, hgrn_w_in[j].astype(BF16))
            mix_in = hgrn_mixer(qfig, lower_bounds[i], hgrn_norm_g[j], nb, s_len, nm)
            w_out, b_out = hgrn_wo[j], zero_bias
        h, hb = mm_res_ln(mix_in, h, w_out.astype(BF16), b_out, ln_mix_g[i], ln_mix_b[i], alpha)
        n_rows_out = nb * s_len if i == depth - 1 else h.shape[0]
        h, hb = moe_layer(h, hb, router_w, router_b, moe_w_gate, moe_w_up, moe_w_down, i,
                          ln_ffn_g[i], ln_ffn_b[i], alpha, n_rows_out)
    return h.reshape(nb, s_len, d)
```

```python
import functools

import jax
import jax.numpy as jnp
from jax import lax
from jax.experimental import pallas as pl
from jax.experimental.pallas import tpu as pltpu

F32 = jnp.float32
BF16 = jnp.bfloat16
I32 = jnp.int32
U32 = jnp.uint32

LANES = 128
CHUNK = 64
N_MIXERS = 3
CONV_WIDTH = 31
CONV_HALO = 32
CONV_LANES = 256
HEAD_DIM = 128
MLA_ROPE = 64
ROPE_THETA = 10000.0
HGRN_CHUNK = 128
N_EXPERTS = 32
N_GROUPS = 4
EXPERTS_PER_GROUP = N_EXPERTS // N_GROUPS
MOE_BLOCK = 256
COMBINE_ROWS = 16
LN_EPS = 1e-5
RMS_EPS = 1e-6
VMEM_LIMIT = 52 * 1024 * 1024


def _pick(n, cands):
    for c in cands:
        if n % c == 0:
            return c
    raise ValueError(f"no tile for {n} in {cands}")


_ROW_TILES = (768, 512, 384, 256, 192, 176, 128, 96, 64, 48, 32, 16)


def _cparams(sem, vmem=VMEM_LIMIT):
    return pltpu.CompilerParams(dimension_semantics=sem, vmem_limit_bytes=vmem)


def _layer_norm(x, g, b):
    mu = jnp.mean(x, axis=-1, keepdims=True)
    xc = x - mu
    var = jnp.mean(xc * xc, axis=-1, keepdims=True)
    return xc * lax.rsqrt(var + LN_EPS) * g + b


def _rms_norm(x, g):
    return x * lax.rsqrt(jnp.mean(x * x, axis=-1, keepdims=True) + RMS_EPS) * g


def _sigmoid(x):
    return 1.0 / (1.0 + jnp.exp(-x))


def _pack_bf16_pair(lo, hi):
    lo_b = lax.bitcast_convert_type(lo.astype(BF16).astype(F32), U32)
    hi_b = lax.bitcast_convert_type(hi.astype(BF16).astype(F32), U32)
    return (hi_b & jnp.uint32(0xFFFF0000)) | (lo_b >> 16)


def _unpack_bf16_pair(w):
    lo = lax.bitcast_convert_type(w << 16, F32)
    hi = lax.bitcast_convert_type(w & jnp.uint32(0xFFFF0000), F32)
    return lo, hi


def _write_meta_rows(meta_ref, o_ref):
    o_ref[...] = jnp.zeros(o_ref.shape, o_ref.dtype)
    o_ref[0:meta_ref.shape[0], :] = meta_ref[...]


def _rows_call(body, n_rows, tm, row_ins, full_ins, outs, name):
    grid = (n_rows // tm,)
    in_specs = [pl.BlockSpec((tm, a.shape[1]), lambda i: (i, 0)) for a in row_ins]
    in_specs += [pl.BlockSpec(a.shape, lambda i, nd=a.ndim: (0,) * nd) for a in full_ins]
    out_specs = [pl.BlockSpec((tm, n), lambda i: (i, 0)) for n, _ in outs]
    out_shape = [jax.ShapeDtypeStruct((n_rows, n), dt) for n, dt in outs]
    return pl.pallas_call(
        body, grid=grid, in_specs=in_specs, out_specs=out_specs, out_shape=out_shape,
        compiler_params=_cparams(("parallel",)), name=name)(*row_ins, *full_ins)


def _mm_res_ln_body(alpha, x_ref, h_ref, w_ref, bias_ref, g_ref, b_ref, ho_ref, hb_ref):
    acc = jnp.dot(x_ref[...], w_ref[...], preferred_element_type=F32)
    y = alpha * h_ref[...] + (acc + bias_ref[...])
    o = _layer_norm(y, g_ref[...], b_ref[...])
    ho_ref[...] = o
    hb_ref[...] = o.astype(BF16)


def mm_res_ln(x_bf, h, w_bf, bias, g, b, alpha):
    t, d = h.shape
    tm = _pick(t, (256, 192, 176, 128, 96, 64, 48, 32, 16))
    return _rows_call(functools.partial(_mm_res_ln_body, alpha), t, tm, [x_bf, h],
                      [w_bf, bias.reshape(1, d), g.reshape(1, d), b.reshape(1, d)],
                      [(d, F32), (d, BF16)], "mm_res_ln")


def _glu_body(x_ref, wa_ref, wg_ref, ba_ref, bg_ref, u_ref):
    x = x_ref[...]
    a = jnp.dot(x, wa_ref[...], preferred_element_type=F32) + ba_ref[...]
    g = jnp.dot(x, wg_ref[...], preferred_element_type=F32) + bg_ref[...]
    u_ref[...] = a * _sigmoid(g)


def mm_glu(x_bf, w_bf, bias):
    t, k = x_bf.shape
    d = w_bf.shape[1] // 2
    tm = _pick(t, _ROW_TILES)
    tn = min(d, 1024)
    nj = d // tn
    bias2 = bias.reshape(1, 2 * d)
    return pl.pallas_call(
        _glu_body, grid=(nj, t // tm),
        in_specs=[pl.BlockSpec((tm, k), lambda j, i: (i, 0)),
                  pl.BlockSpec((k, tn), lambda j, i: (0, j)),
                  pl.BlockSpec((k, tn), lambda j, i: (0, j + nj)),
                  pl.BlockSpec((1, tn), lambda j, i: (0, j)),
                  pl.BlockSpec((1, tn), lambda j, i: (0, j + nj))],
        out_specs=pl.BlockSpec((tm, tn), lambda j, i: (i, j)),
        out_shape=jax.ShapeDtypeStruct((t, d), F32),
        compiler_params=_cparams(("parallel", "parallel")), name="mm_glu")(x_bf, w_bf, w_bf, bias2, bias2)


def _mm_plain_body(x_ref, w_ref, o_ref):
    o_ref[...] = jnp.dot(x_ref[...], w_ref[...], preferred_element_type=F32).astype(o_ref.dtype)


def mm_plain(x_bf, w_bf, out_dtype=F32):
    t, k = x_bf.shape
    n = w_bf.shape[1]
    tm = _pick(t, _ROW_TILES)
    tn = min(n, 2048)
    return pl.pallas_call(
        _mm_plain_body, grid=(n // tn, t // tm),
        in_specs=[pl.BlockSpec((tm, k), lambda j, i: (i, 0)),
                  pl.BlockSpec((k, tn), lambda j, i: (0, j))],
        out_specs=pl.BlockSpec((tm, tn), lambda j, i: (i, j)),
        out_shape=jax.ShapeDtypeStruct((t, n), out_dtype),
        compiler_params=_cparams(("parallel", "parallel")), name="mm_plain")(x_bf, w_bf)


def _conv_rows(ext_ref, w_ref, y_ref, sh_ref, rows, rc, lc):
    d = y_ref.shape[1]
    shift = CONV_HALO - (CONV_WIDTH - 1)
    sub = 8
    n_sh = sh_ref.shape[1]

    def lane_body(c, carry):
        l0 = pl.multiple_of(c * lc, lc)
        lanes = pl.ds(l0, lc)
        wv = w_ref[:, lanes]
        for b in range(1, sub):
            sh_ref[b - 1] = ext_ref[pl.ds(b, n_sh), lanes]
        for r in range(rows // rc):
            acc = None
            for k in range(CONV_WIDTH):
                b = (shift + k) % sub
                a = r * rc + (shift + k) - b
                src = ext_ref[pl.ds(a, rc), lanes] if b == 0 else sh_ref[b - 1, pl.ds(a, rc), :]
                term = src * wv[k:k + 1, :]
                acc = term if acc is None else acc + term
            y_ref[pl.ds(r * rc, rc), lanes] = acc
        return carry

    lax.fori_loop(0, d // lc, lane_body, 0)


def _conv_epilogue(y_ref, dwb_ref, g_ref, b_ref, o_ref):
    rows = y_ref.shape[0]
    step = 16

    def body(c, carry):
        sl = pl.ds(pl.multiple_of(c * step, step), step)
        z = _layer_norm(y_ref[sl, :] + dwb_ref[...], g_ref[...], b_ref[...])
        o_ref[sl, :] = (z * _sigmoid(z)).astype(BF16)
        return carry

    lax.fori_loop(0, rows // step, body, 0, unroll=min(4, rows // step))


def _conv_main_body(tr, nst, n_real, cur_ref, prev_ref, meta_ref, w_ref, dwb_ref, g_ref, b_ref, vmeta_ref, o_ref,
                    ext_ref, y_ref, sh_ref):
    step = pl.program_id(0)
    nm = meta_ref.shape[0]

    @pl.when(step < n_real)
    def _():
        @pl.when(step % nst == 0)
        def _():
            ext_ref[0:CONV_HALO - nm, :] = jnp.zeros((CONV_HALO - nm, ext_ref.shape[1]), F32)
            ext_ref[CONV_HALO - nm:CONV_HALO, :] = meta_ref[...]

        @pl.when(step % nst > 0)
        def _():
            ext_ref[0:CONV_HALO, :] = prev_ref[...]

        ext_ref[CONV_HALO:CONV_HALO + tr, :] = cur_ref[...]
        _conv_rows(ext_ref, w_ref, y_ref, sh_ref, tr, 64, CONV_LANES)
        _conv_epilogue(y_ref, dwb_ref, g_ref, b_ref, o_ref)

    @pl.when(step == n_real)
    def _():
        _write_meta_rows(vmeta_ref, o_ref)


def _conv_meta_body(meta_ref, w_ref, dwb_ref, g_ref, b_ref, o_ref, ext_ref, y_ref, sh_ref):
    nm = meta_ref.shape[0]
    ext_ref[0:CONV_HALO, :] = jnp.zeros((CONV_HALO, ext_ref.shape[1]), F32)
    ext_ref[CONV_HALO:CONV_HALO + nm, :] = meta_ref[...]
    _conv_rows(ext_ref, w_ref, y_ref, sh_ref, nm, nm, CONV_LANES)
    _conv_epilogue(y_ref, dwb_ref, g_ref, b_ref, o_ref)


def conv_ln_swish(u, dw_w, dw_b, ln_g, ln_b, nb, s_len, nm):
    t, d = u.shape
    tr_rows = nb * s_len
    tr = _pick(s_len, (512, 256))
    w_pad = jnp.concatenate([dw_w, jnp.zeros((CONV_HALO - CONV_WIDTH, d), F32)], axis=0)
    vecs = [dw_b.reshape(1, d), ln_g.reshape(1, d), ln_b.reshape(1, d)]
    meta_blk = tr_rows // nm
    out_meta = pl.pallas_call(
        _conv_meta_body, grid=(nb,),
        in_specs=[pl.BlockSpec((nm, d), lambda b: (meta_blk + b, 0)),
                  pl.BlockSpec((CONV_HALO, d), lambda b: (0, 0))]
        + [pl.BlockSpec((1, d), lambda b: (0, 0))] * 3,
        out_specs=pl.BlockSpec((nm, d), lambda b: (b, 0)),
        out_shape=jax.ShapeDtypeStruct((nb * nm, d), BF16),
        scratch_shapes=[pltpu.VMEM((CONV_HALO + nm, d), F32), pltpu.VMEM((nm, d), F32),
                        pltpu.VMEM((7, CONV_HALO + nm - 8, CONV_LANES), F32)],
        compiler_params=_cparams(("parallel",)), name="conv_meta")(u, w_pad, *vecs)
    nst = s_len // tr
    n_real = nb * nst
    halo_per_tile = tr // CONV_HALO
    assert nb * nm <= tr

    def real(g):
        return jnp.minimum(g, n_real - 1)

    return pl.pallas_call(
        functools.partial(_conv_main_body, tr, nst, n_real), grid=(n_real + 1,),
        in_specs=[pl.BlockSpec((tr, d), lambda g: (real(g), 0)),
                  pl.BlockSpec((CONV_HALO, d), lambda g: (jnp.maximum(real(g) * halo_per_tile - 1, 0), 0)),
                  pl.BlockSpec((nm, d), lambda g: (meta_blk + real(g) // nst, 0)),
                  pl.BlockSpec((CONV_HALO, d), lambda g: (0, 0))]
        + [pl.BlockSpec((1, d), lambda g: (0, 0))] * 3
        + [pl.BlockSpec((nb * nm, d), lambda g: (0, 0))],
        out_specs=pl.BlockSpec((tr, d), lambda g: (g, 0)),
        out_shape=jax.ShapeDtypeStruct((t, d), BF16),
        scratch_shapes=[pltpu.VMEM((CONV_HALO + tr, d), F32), pltpu.VMEM((tr, d), F32),
                        pltpu.VMEM((7, CONV_HALO + tr - 8, CONV_LANES), F32)],
        compiler_params=_cparams(("arbitrary",)), name="conv_main")(u, u, u, w_pad, *vecs, out_meta)


def _mla_proj_body(ql, kvl, hd, scale, x_ref, cos_ref, sin_ref, w1_ref, qg_ref, kvg_ref, wq_ref, wkv_ref,
                   qn_ref, qr_ref, kn_ref, v_ref, kr_ref):
    a = jnp.dot(x_ref[...], w1_ref[...], preferred_element_type=F32)
    cos = cos_ref[...]
    sin = sin_ref[...]
    cq = _rms_norm(a[:, :ql], qg_ref[...]).astype(BF16)
    ckv = _rms_norm(a[:, ql:ql + kvl], kvg_ref[...]).astype(BF16)
    r0 = ql + kvl
    kr_ref[...] = (a[:, r0:r0 + LANES] * cos + a[:, r0 + LANES:r0 + 2 * LANES] * sin).astype(BF16)
    qa = jnp.dot(cq, wq_ref[...], preferred_element_type=F32)
    nh = hd // LANES
    cos_t = jnp.tile(cos, (1, nh))
    sin_t = jnp.tile(sin, (1, nh))
    qn_ref[...] = (qa[:, :hd] * scale).astype(BF16)
    qr_ref[...] = ((qa[:, hd:2 * hd] * cos_t + qa[:, 2 * hd:] * sin_t) * scale).astype(BF16)
    kv = jnp.dot(ckv, wkv_ref[...], preferred_element_type=F32)
    kn_ref[...] = kv[:, :hd].astype(BF16)
    v_ref[...] = kv[:, hd:].astype(BF16)


def _attn_body(tq, nb, qn_ref, qr_ref, kn_ref, kr_ref, v_ref, knm_ref, krm_ref, vm_ref, ometa_ref, o_ref, kf_ref):
    @pl.when(pl.program_id(0) == nb)
    def _():
        _write_meta_rows(ometa_ref, o_ref)

    @pl.when(pl.program_id(0) < nb)
    def _():
        _attn_tiles(tq, qn_ref, qr_ref, kn_ref, kr_ref, v_ref, knm_ref, krm_ref, vm_ref, o_ref, kf_ref)


def _attn_tiles(tq, qn_ref, qr_ref, kn_ref, kr_ref, v_ref, knm_ref, krm_ref, vm_ref, o_ref, kf_ref):
    s_len = qn_ref.shape[0]
    nt = (((1,), (1,)), ((), ()))
    kf_ref[:, :LANES] = kn_ref[...]
    kf_ref[:, LANES:] = kr_ref[...]
    km = jnp.concatenate([knm_ref[...], krm_ref[...]], axis=-1)
    vm = vm_ref[...]
    row_c = lax.broadcasted_iota(I32, (tq, tq), 0) // CHUNK
    col_c = lax.broadcasted_iota(I32, (tq, tq), 1) // CHUNK
    visible = col_c <= row_c
    for i in range(s_len // tq):
        r0 = i * tq
        q = jnp.concatenate([qn_ref[r0:r0 + tq, :], qr_ref[r0:r0 + tq, :]], axis=-1)
        s_m = lax.dot_general(q, km, nt, preferred_element_type=F32)
        s_d = lax.dot_general(q, kf_ref[r0:r0 + tq, :], nt, preferred_element_type=F32)
        s_d = jnp.where(visible, s_d, -jnp.inf)
        m = jnp.maximum(jnp.max(s_m, axis=-1, keepdims=True), jnp.max(s_d, axis=-1, keepdims=True))
        if i > 0:
            s_p = lax.dot_general(q, kf_ref[0:r0, :], nt, preferred_element_type=F32)
            m = jnp.maximum(m, jnp.max(s_p, axis=-1, keepdims=True))
        p_m = jnp.exp(s_m - m)
        p_d = jnp.exp(s_d - m)
        l = jnp.sum(p_m, axis=-1, keepdims=True) + jnp.sum(p_d, axis=-1, keepdims=True)
        acc = jnp.dot(p_m.astype(BF16), vm, preferred_element_type=F32)
        acc = acc + jnp.dot(p_d.astype(BF16), v_ref[r0:r0 + tq, :], preferred_element_type=F32)
        if i > 0:
            p_p = jnp.exp(s_p - m)
            l = l + jnp.sum(p_p, axis=-1, keepdims=True)
            acc = acc + jnp.dot(p_p.astype(BF16), v_ref[0:r0, :], preferred_element_type=F32)
        o_ref[r0:r0 + tq, :] = (acc / l).astype(BF16)


def _attn_meta_body(qn_ref, qr_ref, kn_ref, kr_ref, v_ref, o_ref):
    nt = (((1,), (1,)), ((), ()))
    q = jnp.concatenate([qn_ref[...], qr_ref[...]], axis=-1)
    k = jnp.concatenate([kn_ref[...], kr_ref[...]], axis=-1)
    s = lax.dot_general(q, k, nt, preferred_element_type=F32)
    p = jnp.exp(s - jnp.max(s, axis=-1, keepdims=True))
    l = jnp.sum(p, axis=-1, keepdims=True)
    o_ref[...] = (jnp.dot(p.astype(BF16), v_ref[...], preferred_element_type=F32) / l).astype(BF16)


def _rope_rows(n_pos):
    inv = ROPE_THETA ** (-jnp.arange(0, MLA_ROPE, 2, dtype=F32) / MLA_ROPE)
    ang = jnp.arange(n_pos, dtype=F32)[:, None] * inv[None, :]
    pad = jnp.zeros((n_pos, LANES - MLA_ROPE), F32)
    cos = jnp.concatenate([jnp.cos(ang), jnp.cos(ang), pad], axis=1)
    sin = jnp.concatenate([jnp.sin(ang), jnp.sin(ang), pad], axis=1)
    return cos, sin


def _pad_rope_cols(w):
    half = MLA_ROPE // 2
    z = jnp.zeros((w.shape[0], LANES - MLA_ROPE), w.dtype)
    rot = jnp.concatenate([-w[:, half:], w[:, :half]], axis=1)
    return jnp.concatenate([w, z], axis=1), jnp.concatenate([rot, z], axis=1)


def mla_mixer(hb, wdq, q_norm_g, wuq, wdkv, kv_norm_g, wukv, nb, s_len, nm):
    t, d = hb.shape
    ql = wdq.shape[1]
    kvl = kv_norm_g.shape[0]
    nh = wuq.shape[1] // (HEAD_DIM + MLA_ROPE)
    hd = nh * HEAD_DIM
    tr_rows = nb * s_len
    scale = float((HEAD_DIM + MLA_ROPE) ** -0.5)
    kr_w, kr_rot = _pad_rope_cols(wdkv[:, kvl:])
    w1 = jnp.concatenate([wdq, wdkv[:, :kvl], kr_w, kr_rot], axis=1).astype(BF16)
    wuq3 = wuq.reshape(ql, nh, HEAD_DIM + MLA_ROPE)
    q_rope = wuq3[:, :, HEAD_DIM:]
    half = MLA_ROPE // 2
    zq = jnp.zeros((ql, nh, LANES - MLA_ROPE), F32)
    q_rope_p = jnp.concatenate([q_rope, zq], axis=2).reshape(ql, hd)
    q_rot_p = jnp.concatenate([-q_rope[:, :, half:], q_rope[:, :, :half], zq], axis=2).reshape(ql, hd)
    wq = jnp.concatenate([wuq3[:, :, :HEAD_DIM].reshape(ql, hd), q_rope_p, q_rot_p], axis=1).astype(BF16)
    wukv3 = wukv.reshape(kvl, nh, 2 * HEAD_DIM)
    wkv = jnp.concatenate([wukv3[:, :, :HEAD_DIM].reshape(kvl, hd),
                           wukv3[:, :, HEAD_DIM:].reshape(kvl, hd)], axis=1).astype(BF16)
    cos_p, sin_p = _rope_rows(nm + s_len)
    cos_rows = jnp.concatenate([jnp.tile(cos_p[nm:], (nb, 1)), jnp.tile(cos_p[:nm], (nb, 1))], axis=0)
    sin_rows = jnp.concatenate([jnp.tile(sin_p[nm:], (nb, 1)), jnp.tile(sin_p[:nm], (nb, 1))], axis=0)

    tm = _pick(t, (128, 96, 64, 48, 32, 16))
    qn, qr, kn, v, kr = _rows_call(
        functools.partial(_mla_proj_body, ql, kvl, hd, scale), t, tm, [hb, cos_rows, sin_rows],
        [w1, q_norm_g.reshape(1, ql), kv_norm_g.reshape(1, kvl), wq, wkv],
        [(hd, BF16), (hd, BF16), (hd, BF16), (hd, BF16), (LANES, BF16)], "mla_proj")

    meta_blk = tr_rows // nm
    mspec = lambda: pl.BlockSpec((nm, LANES), lambda b, h: (meta_blk + b, h))
    mspec0 = lambda: pl.BlockSpec((nm, LANES), lambda b, h: (meta_blk + b, 0))
    o_meta = pl.pallas_call(
        _attn_meta_body, grid=(nb, nh),
        in_specs=[mspec(), mspec(), mspec(), mspec0(), mspec()],
        out_specs=pl.BlockSpec((nm, LANES), lambda b, h: (b, h)),
        out_shape=jax.ShapeDtypeStruct((nb * nm, hd), BF16),
        compiler_params=_cparams(("parallel", "parallel")), name="attn_meta")(qn, qr, kn, kr, v)

    tq = 256
    assert nb * nm <= s_len

    def real(b):
        return jnp.minimum(b, nb - 1)

    kspec = lambda: pl.BlockSpec((s_len, LANES), lambda b, h: (real(b), h))
    kspec0 = lambda: pl.BlockSpec((s_len, LANES), lambda b, h: (real(b), 0))
    m2 = lambda: pl.BlockSpec((nm, LANES), lambda b, h: (meta_blk + real(b), h))
    m20 = lambda: pl.BlockSpec((nm, LANES), lambda b, h: (meta_blk + real(b), 0))
    return pl.pallas_call(
        functools.partial(_attn_body, tq, nb), grid=(nb + 1, nh),
        in_specs=[kspec(), kspec(), kspec(), kspec0(), kspec(), m2(), m20(), m2(),
                  pl.BlockSpec((nb * nm, LANES), lambda b, h: (0, h))],
        out_specs=pl.BlockSpec((s_len, LANES), lambda b, h: (b, h)),
        out_shape=jax.ShapeDtypeStruct((t, hd), BF16),
        scratch_shapes=[pltpu.VMEM((s_len, 2 * LANES), BF16)],
        compiler_params=_cparams(("arbitrary", "arbitrary")), name="attn_main")(
            qn, qr, kn, kr, v, kn, kr, v, o_meta)


def _hgrn_gates(fz, lb):
    return lb + (1.0 - lb) * _sigmoid(fz), (1.0 - lb) * _sigmoid(-fz)


def _hgrn_level_index(c):
    t = lax.broadcasted_iota(I32, (c, c), 0)
    s = lax.broadcasted_iota(I32, (c, c), 1)
    lvl = 31 - lax.clz(t ^ s)
    return jnp.where(s > t, -2, lvl)


def _hgrn_block(q, f, k, iv, st, lvl):
    c = q.shape[0]
    nt = (((1,), (1,)), ((), ()))
    tn = (((0,), (0,)), ((), ()))
    row = lax.broadcasted_iota(I32, (c, HEAD_DIM), 0)
    scores = jnp.where(lvl == -1, lax.dot_general(q.astype(BF16), k.astype(BF16), nt,
                                                  preferred_element_type=F32), 0.0)
    qa = q * f
    kb = k
    tot = f
    h, idx = 1, 0
    while h < c:
        prod = lax.dot_general(qa.astype(BF16), kb.astype(BF16), nt, preferred_element_type=F32)
        scores = jnp.where(lvl == idx, prod, scores)
        right = (row & h) != 0
        left_tot = pltpu.roll(tot, h, 0)
        right_tot = pltpu.roll(tot, c - h, 0)
        qa = qa * jnp.where(right, left_tot, 1.0)
        kb = kb * jnp.where(right, 1.0, right_tot)
        tot = tot * jnp.where(right, left_tot, right_tot)
        h, idx = 2 * h, idx + 1
    ib = iv.astype(BF16)
    o = jnp.dot(scores.astype(BF16), ib, preferred_element_type=F32)
    o = o + lax.dot_general(qa.astype(BF16), st.astype(BF16), nt, preferred_element_type=F32)
    st_new = st * tot[0:1, :] + lax.dot_general(ib, kb.astype(BF16), tn, preferred_element_type=F32)
    return o, st_new


def _hgrn_out(o, gate, ng):
    o = o * lax.rsqrt(jnp.mean(o * o, axis=-1, keepdims=True) + RMS_EPS) * ng
    return (o * (gate * _sigmoid(gate))).astype(BF16)


def _hgrn_meta_body(gh, q_ref, fz_ref, i_ref, g_ref, lb_ref, ng_ref, o_ref, st_ref):
    nm = q_ref.shape[0]
    c = HGRN_CHUNK
    lvl = _hgrn_level_index(c)
    zeros = jnp.zeros((c - nm, HEAD_DIM), F32)
    for g in range(gh):
        sl = slice(g * HEAD_DIM, (g + 1) * HEAD_DIM)
        f, k = _hgrn_gates(fz_ref[:, sl], lb_ref[:, sl])
        o, st = _hgrn_block(jnp.concatenate([zeros, q_ref[:, sl]], axis=0),
                            jnp.concatenate([zeros + 1.0, f], axis=0),
                            jnp.concatenate([zeros, k], axis=0),
                            jnp.concatenate([zeros, i_ref[:, sl]], axis=0),
                            jnp.zeros((HEAD_DIM, HEAD_DIM), F32), lvl)
        st_ref[0, g] = st
        o_ref[:, sl] = _hgrn_out(o[c - nm:], g_ref[:, sl], ng_ref[...])


def _hgrn_main_body(gh, ts, nst, n_real, q_ref, fz_ref, i_ref, g_ref, lb_ref, ng_ref, st0_ref, ometa_ref, o_ref,
                    st_ref):
    c = HGRN_CHUNK
    step = pl.program_id(1)

    @pl.when(step == n_real)
    def _():
        _write_meta_rows(ometa_ref, o_ref)

    @pl.when(step < n_real)
    def _():
        lvl = _hgrn_level_index(c)

        @pl.when(step % nst == 0)
        def _():
            st_ref[...] = st0_ref[0]

        def chunk_body(cidx, carry):
            rows = pl.ds(pl.multiple_of(cidx * c, c), c)
            for g in range(gh):
                sl = slice(g * HEAD_DIM, (g + 1) * HEAD_DIM)
                f, k = _hgrn_gates(fz_ref[rows, sl], lb_ref[:, sl])
                o, st = _hgrn_block(q_ref[rows, sl], f, k, i_ref[rows, sl], st_ref[g], lvl)
                st_ref[g] = st
                o_ref[rows, sl] = _hgrn_out(o, g_ref[rows, sl], ng_ref[...])
            return carry

        lax.fori_loop(0, ts // c, chunk_body, 0)


def hgrn_mixer(qfig, lb, norm_g, nb, s_len, nm):
    t, d4 = qfig.shape
    d = d4 // 4
    nh = d // HEAD_DIM
    gh = 4 if nh % 4 == 0 else 1
    gw = gh * HEAD_DIM
    ng_blocks = d // gw
    tr_rows = nb * s_len
    meta_blk = tr_rows // nm
    lb2 = lb.reshape(1, d)
    ng2 = norm_g.reshape(1, HEAD_DIM)

    def mspec(sec):
        return pl.BlockSpec((nm, gw), lambda b, h: (meta_blk + b, sec * ng_blocks + h))

    o_meta, st0 = pl.pallas_call(
        functools.partial(_hgrn_meta_body, gh), grid=(nb, ng_blocks),
        in_specs=[mspec(0), mspec(1), mspec(2), mspec(3),
                  pl.BlockSpec((1, gw), lambda b, h: (0, h)),
                  pl.BlockSpec((1, HEAD_DIM), lambda b, h: (0, 0))],
        out_specs=[pl.BlockSpec((nm, gw), lambda b, h: (b, h)),
                   pl.BlockSpec((1, gh, HEAD_DIM, HEAD_DIM), lambda b, h: (b, h, 0, 0))],
        out_shape=[jax.ShapeDtypeStruct((nb * nm, d), BF16),
                   jax.ShapeDtypeStruct((nb, nh, HEAD_DIM, HEAD_DIM), F32)],
        compiler_params=_cparams(("parallel", "parallel")), name="hgrn_meta")(qfig, qfig, qfig, qfig, lb2, ng2)

    ts = _pick(s_len, (512, 256))
    nst = s_len // ts
    n_real = nb * nst
    assert nb * nm <= ts

    def real(g):
        return jnp.minimum(g, n_real - 1)

    def rspec(sec):
        return pl.BlockSpec((ts, gw), lambda h, g: (real(g), sec * ng_blocks + h))

    return pl.pallas_call(
        functools.partial(_hgrn_main_body, gh, ts, nst, n_real), grid=(ng_blocks, n_real + 1),
        in_specs=[rspec(0), rspec(1), rspec(2), rspec(3),
                  pl.BlockSpec((1, gw), lambda h, g: (0, h)),
                  pl.BlockSpec((1, HEAD_DIM), lambda h, g: (0, 0)),
                  pl.BlockSpec((1, gh, HEAD_DIM, HEAD_DIM), lambda h, g: (real(g) // nst, h, 0, 0)),
                  pl.BlockSpec((nb * nm, gw), lambda h, g: (0, h))],
        out_specs=pl.BlockSpec((ts, gw), lambda h, g: (g, h)),
        out_shape=jax.ShapeDtypeStruct((t, d), BF16),
        scratch_shapes=[pltpu.VMEM((gh, HEAD_DIM, HEAD_DIM), F32)],
        compiler_params=_cparams(("parallel", "arbitrary")), name="hgrn_main")(
            qfig, qfig, qfig, qfig, lb2, ng2, st0, o_meta)


def _first_index_of_max(vals, idx, n, axis):
    mx = jnp.max(vals, axis=axis, keepdims=True)
    first = jnp.min(jnp.where(vals == mx, idx, n), axis=axis, keepdims=True)
    return mx, first


def _router_body(tm, h_ref, hb_ref, rw_ref, rb_ref, tri_ref, e_ref, gate_ref, rank_ref, cnt_ref, base_ref):
    @pl.when(pl.program_id(0) == 0)
    def _():
        base_ref[...] = jnp.zeros(base_ref.shape, F32)

    nt = (((1,), (1,)), ((), ()))
    hb = hb_ref[...]
    resid = (h_ref[...] - hb.astype(F32)).astype(BF16)
    main = lax.dot_general(rw_ref[...], hb, nt, preferred_element_type=F32)
    logits = (main[:N_EXPERTS] + main[N_EXPERTS:2 * N_EXPERTS] + main[2 * N_EXPERTS:]
              + lax.dot_general(rw_ref[:N_EXPERTS, :], resid, nt, preferred_element_type=F32))
    scores = _sigmoid(logits)
    sel = scores + rb_ref[...]
    g, epg = N_GROUPS, EXPERTS_PER_GROUP
    sel3 = sel.reshape(g, epg, tm)
    sc3 = scores.reshape(g, epg, tm)
    idx3 = lax.broadcasted_iota(I32, (g, epg, tm), 1)
    m1, i1 = _first_index_of_max(sel3, idx3, epg, 1)
    rest = jnp.where(idx3 == i1, -jnp.inf, sel3)
    m2, i2 = _first_index_of_max(rest, idx3, epg, 1)
    gidx = lax.broadcasted_iota(I32, (g, 1, tm), 0)
    _, gtop3 = _first_index_of_max(m1 + m2, gidx, g, 0)
    pick = gidx == gtop3
    gtop = gtop3[0]
    l1 = jnp.sum(jnp.where(pick, i1, 0), axis=0)
    l2 = jnp.sum(jnp.where(pick, i2, 0), axis=0)
    sc_in = jnp.sum(jnp.where(pick, sc3, 0.0), axis=0)
    idx2 = lax.broadcasted_iota(I32, (epg, tm), 0)
    s1 = jnp.sum(jnp.where(idx2 == l1, sc_in, 0.0), axis=0, keepdims=True)
    s2 = jnp.sum(jnp.where(idx2 == l2, sc_in, 0.0), axis=0, keepdims=True)
    e1 = gtop * epg + l1
    e2 = gtop * epg + l2
    e_ref[0:1, :] = e1
    e_ref[1:2, :] = e2
    den = s1 + s2
    gate_ref[0:1, :] = s1 / den
    gate_ref[1:2, :] = s2 / den
    eidx = lax.broadcasted_iota(I32, (N_EXPERTS, tm), 0)
    oh1 = (eidx == e1).astype(F32)
    oh2 = (eidx == e2).astype(F32)
    oh = jnp.concatenate([oh1, oh2], axis=0).astype(BF16)
    pre = jnp.dot(oh, tri_ref[...], preferred_element_type=F32)
    base = base_ref[:, 0:1]
    tot1 = jnp.sum(oh1, axis=1, keepdims=True)
    tot2 = jnp.sum(oh2, axis=1, keepdims=True)
    r1 = jnp.sum(oh1 * (base + pre[:N_EXPERTS]), axis=0, keepdims=True)
    r2 = jnp.sum(oh2 * (base + tot1 + pre[N_EXPERTS:]), axis=0, keepdims=True)
    rank_ref[0:1, :] = r1.astype(I32)
    rank_ref[1:2, :] = r2.astype(I32)
    new_base = jnp.broadcast_to(base + tot1 + tot2, base_ref.shape)
    base_ref[...] = new_base
    cnt_ref[...] = new_base


def moe_route(h, hb, router_w, router_b):
    t, d = h.shape
    tm = _pick(t, (256, 128, 96, 64, 32))
    tri = (lax.broadcasted_iota(I32, (tm, tm), 0) < lax.broadcasted_iota(I32, (tm, tm), 1)).astype(BF16)
    rw_t = router_w.T.astype(F32)
    rw1 = rw_t.astype(BF16)
    rw2 = (rw_t - rw1.astype(F32)).astype(BF16)
    rw3 = (rw_t - rw1.astype(F32) - rw2.astype(F32)).astype(BF16)
    rw_pieces = jnp.concatenate([rw1, rw2, rw3], axis=0)
    e, gate, rank, cnt = pl.pallas_call(
        functools.partial(_router_body, tm), grid=(t // tm,),
        in_specs=[pl.BlockSpec((tm, d), lambda i: (i, 0)),
                  pl.BlockSpec((tm, d), lambda i: (i, 0)),
                  pl.BlockSpec((3 * N_EXPERTS, d), lambda i: (0, 0)),
                  pl.BlockSpec((N_EXPERTS, 1), lambda i: (0, 0)),
                  pl.BlockSpec((tm, tm), lambda i: (0, 0))],
        out_specs=[pl.BlockSpec((2, tm), lambda i: (0, i)),
                   pl.BlockSpec((2, tm), lambda i: (0, i)),
                   pl.BlockSpec((2, tm), lambda i: (0, i)),
                   pl.BlockSpec((N_EXPERTS, LANES), lambda i: (0, 0))],
        out_shape=[jax.ShapeDtypeStruct((2, t), I32), jax.ShapeDtypeStruct((2, t), F32),
                   jax.ShapeDtypeStruct((2, t), I32), jax.ShapeDtypeStruct((N_EXPERTS, LANES), F32)],
        scratch_shapes=[pltpu.VMEM((N_EXPERTS, LANES), F32)],
        compiler_params=_cparams(("arbitrary",)), name="moe_route")(
            h, hb, rw_pieces, router_b.reshape(N_EXPERTS, 1), tri)
    return e, gate, rank, cnt[:, 0].astype(I32)


def _dispatch_body(tt, zero_flag_ref, dest_ref, hb_ref, xb_hbm, stage_ref, zero_ref, sem):
    half = stage_ref.shape[1]

    @pl.when(pl.program_id(0) == 0)
    def _():
        zero_ref[...] = jnp.zeros(zero_ref.shape, U32)

        def zfill(b, carry):
            @pl.when(zero_flag_ref[b] > 0)
            def _():
                start = pl.multiple_of(b * MOE_BLOCK, MOE_BLOCK)
                cp = pltpu.make_async_copy(zero_ref, xb_hbm.at[pl.ds(start, MOE_BLOCK), :], sem)
                cp.start()
                cp.wait()
            return carry

        lax.fori_loop(0, zero_flag_ref.shape[0], zfill, 0)

    stage_ref[...] = _pack_bf16_pair(hb_ref[:, :half].astype(F32), hb_ref[:, half:].astype(F32))

    def issue(j, carry):
        for k in range(2):
            pltpu.make_async_copy(stage_ref.at[pl.ds(j, 1), :],
                                  xb_hbm.at[pl.ds(dest_ref[k, j], 1), :], sem).start(priority=k)
        return carry

    lax.fori_loop(0, tt, issue, 0, unroll=8)
    for k in range(2):
        pltpu.make_async_copy(stage_ref, xb_hbm.at[pl.ds(0, tt), :], sem).wait()


def moe_dispatch(hb, dest, zero_flag, n_slots):
    t, d = hb.shape
    tt = _pick(t, (256, 128, 96, 64, 32))
    return pl.pallas_call(
        functools.partial(_dispatch_body, tt),
        grid_spec=pltpu.PrefetchScalarGridSpec(
            num_scalar_prefetch=1, grid=(t // tt,),
            in_specs=[pl.BlockSpec((2, tt), lambda i, zf: (0, i), memory_space=pltpu.SMEM),
                      pl.BlockSpec((tt, d), lambda i, zf: (i, 0))],
            out_specs=pl.BlockSpec(memory_space=pl.ANY),
            scratch_shapes=[pltpu.VMEM((tt, d // 2), U32), pltpu.VMEM((MOE_BLOCK, d // 2), U32),
                            pltpu.SemaphoreType.DMA(())]),
        out_shape=jax.ShapeDtypeStruct((n_slots, d // 2), U32),
        compiler_params=_cparams(("arbitrary",)), name="moe_dispatch")(zero_flag, dest, hb)


def _expert_body(be_ref, nu_ref, x_ref, wg_ref, wu_ref, wd_ref, y_ref, wgb_ref, wub_ref, wdb_ref):
    i = pl.program_id(0)
    used = i < nu_ref[0]
    half = x_ref.shape[1]

    @pl.when(jnp.logical_or(i == 0, be_ref[i] != be_ref[jnp.maximum(i - 1, 0)]))
    def _():
        wgb_ref[...] = wg_ref[0, 0].astype(BF16)
        wub_ref[...] = wu_ref[0, 0].astype(BF16)
        wdb_ref[...] = wd_ref[0, 0].astype(BF16)

    @pl.when(used)
    def _():
        lo, hi = _unpack_bf16_pair(x_ref[...])
        lo = lo.astype(BF16)
        hi = hi.astype(BF16)
        g = (jnp.dot(lo, wgb_ref[:half, :], preferred_element_type=F32)
             + jnp.dot(hi, wgb_ref[half:, :], preferred_element_type=F32))
        u = (jnp.dot(lo, wub_ref[:half, :], preferred_element_type=F32)
             + jnp.dot(hi, wub_ref[half:, :], preferred_element_type=F32))
        a = (g * _sigmoid(g) * u).astype(BF16)
        y = jnp.dot(a, wdb_ref[...], preferred_element_type=F32)
        y_ref[...] = _pack_bf16_pair(y[:, :half], y[:, half:])

    @pl.when(jnp.logical_not(used))
    def _():
        y_ref[...] = jnp.zeros(y_ref.shape, U32)


def moe_experts(xb, blk_expert, n_used, wg, wu, wd, layer):
    n_slots, half = xb.shape
    d = 2 * half
    ff = wg.shape[3]
    nblk = n_slots // MOE_BLOCK
    return pl.pallas_call(
        _expert_body,
        grid_spec=pltpu.PrefetchScalarGridSpec(
            num_scalar_prefetch=2, grid=(nblk,),
            in_specs=[pl.BlockSpec((MOE_BLOCK, half), lambda i, be, nu: (i, 0)),
                      pl.BlockSpec((1, 1, d, ff), lambda i, be, nu: (layer, be[i], 0, 0)),
                      pl.BlockSpec((1, 1, d, ff), lambda i, be, nu: (layer, be[i], 0, 0)),
                      pl.BlockSpec((1, 1, ff, d), lambda i, be, nu: (layer, be[i], 0, 0))],
            out_specs=pl.BlockSpec((MOE_BLOCK, half), lambda i, be, nu: (i, 0)),
            scratch_shapes=[pltpu.VMEM((d, ff), BF16), pltpu.VMEM((d, ff), BF16), pltpu.VMEM((ff, d), BF16)]),
        out_shape=jax.ShapeDtypeStruct((n_slots, half), U32),
        compiler_params=_cparams(("arbitrary",)), name="moe_experts")(
            blk_expert, n_used, xb, wg, wu, wd)


def _combine_body(tt, alpha, n_tiles, dest_ref, dnext_ref, gate_ref, h_ref, g_ref, b_ref, y_hbm, ho_ref, hb_ref,
                  buf_ref, sem):
    i = pl.program_id(0)
    slot = i % 2
    nslot = 1 - slot

    def gather(idx_ref, j, to_slot):
        for k in range(2):
            pltpu.make_async_copy(y_hbm.at[pl.ds(idx_ref[k, j], 1), :],
                                  buf_ref.at[to_slot, k, pl.ds(j, 1), :], sem.at[to_slot]).start(priority=k)

    def wait_slot(s):
        for k in range(2):
            pltpu.make_async_copy(y_hbm.at[pl.ds(0, tt), :], buf_ref.at[s, k], sem.at[s]).wait()

    @pl.when(i == 0)
    def _():
        def first(j, carry):
            gather(dest_ref, j, 0)
            return carry
        lax.fori_loop(0, tt, first, 0, unroll=8)

    wait_slot(slot)

    def ahead(j, carry):
        gather(dnext_ref, j, nslot)
        return carry

    lax.fori_loop(0, tt, ahead, 0, unroll=8)

    def rows_body(c, carry):
        r0 = pl.multiple_of(c * COMBINE_ROWS, COMBINE_ROWS)
        rows = pl.ds(r0, COMBINE_ROWS)
        gate = gate_ref[rows, :]
        lo0, hi0 = _unpack_bf16_pair(buf_ref[slot, 0, rows, :])
        lo1, hi1 = _unpack_bf16_pair(buf_ref[slot, 1, rows, :])
        ffn = jnp.concatenate([gate[:, 0:1] * lo0 + gate[:, 1:2] * lo1,
                               gate[:, 0:1] * hi0 + gate[:, 1:2] * hi1], axis=-1)
        o = _layer_norm(alpha * h_ref[rows, :] + ffn, g_ref[...], b_ref[...])
        ho_ref[rows, :] = o
        hb_ref[rows, :] = o.astype(BF16)
        return carry

    lax.fori_loop(0, tt // COMBINE_ROWS, rows_body, 0, unroll=4)

    @pl.when(i == n_tiles - 1)
    def _():
        wait_slot(nslot)


def moe_combine(yb, dest, gate_t, h, ln_g, ln_b, alpha, n_rows):
    t, d = h.shape
    tt = next(c for c in (256, 128, 96, 64, 32) if t % c == 0 and n_rows % c == 0)
    n_tiles = n_rows // tt
    return pl.pallas_call(
        functools.partial(_combine_body, tt, alpha, n_tiles), grid=(n_tiles,),
        in_specs=[pl.BlockSpec((2, tt), lambda i: (0, i), memory_space=pltpu.SMEM),
                  pl.BlockSpec((2, tt), lambda i: (0, jnp.minimum(i + 1, n_tiles - 1)), memory_space=pltpu.SMEM),
                  pl.BlockSpec((tt, 2), lambda i: (i, 0)),
                  pl.BlockSpec((tt, d), lambda i: (i, 0)),
                  pl.BlockSpec((1, d), lambda i: (0, 0)),
                  pl.BlockSpec((1, d), lambda i: (0, 0)),
                  pl.BlockSpec(memory_space=pl.ANY)],
        out_specs=[pl.BlockSpec((tt, d), lambda i: (i, 0)), pl.BlockSpec((tt, d), lambda i: (i, 0))],
        out_shape=[jax.ShapeDtypeStruct((n_rows, d), F32), jax.ShapeDtypeStruct((n_rows, d), BF16)],
        scratch_shapes=[pltpu.VMEM((2, 2, tt, d // 2), U32), pltpu.SemaphoreType.DMA((2,))],
        compiler_params=_cparams(("arbitrary",)), name="moe_combine")(
            dest, dest, gate_t, h, ln_g.reshape(1, d), ln_b.reshape(1, d), yb)


def moe_layer(h, hb, router_w, router_b, w_gate, w_up, w_down, layer, ln_g, ln_b, alpha, n_rows_out):
    t, d = h.shape
    e_idx, gate, rank, counts = moe_route(h, hb, router_w, router_b)
    padded = (counts + MOE_BLOCK - 1) // MOE_BLOCK * MOE_BLOCK
    pends = jnp.cumsum(padded)
    pstart = pends - padded
    experts = jnp.arange(N_EXPERTS, dtype=I32)[:, None, None]
    dest = jnp.sum(jnp.where(e_idx[None] == experts, pstart[:, None, None], 0), axis=0) + rank
    nblk = -(-(2 * t) // MOE_BLOCK) + N_EXPERTS
    n_used = (pends[-1] // MOE_BLOCK).astype(I32)
    blk = jnp.arange(nblk, dtype=I32)
    blk_first_row = jnp.minimum(blk, n_used - 1) * MOE_BLOCK
    blk_expert = jnp.minimum(jnp.sum((pends[None, :] <= blk_first_row[:, None]).astype(I32), axis=1),
                             N_EXPERTS - 1)
    zero_flag = ((blk >= n_used) | (blk == pends[blk_expert] // MOE_BLOCK - 1)).astype(I32)
    xb = moe_dispatch(hb, dest, zero_flag, nblk * MOE_BLOCK)
    yb = moe_experts(xb, blk_expert, n_used.reshape(1), w_gate, w_up, w_down, layer)
    return moe_combine(yb, dest, gate.T, h, ln_g, ln_b, alpha, n_rows_out)


def kernel(x, meta_tokens, ln_mix_g, ln_mix_b, ln_ffn_g, ln_ffn_b, conv_pw1_w, conv_pw1_b, conv_dw_w, conv_dw_b, conv_ln_g, conv_ln_b, conv_pw2_w, conv_pw2_b, mla_wdq, mla_q_norm_g, mla_wuq, mla_wdkv, mla_kv_norm_g, mla_wukv, mla_wo, hgrn_w_in, hgrn_lb_logits, hgrn_norm_g, hgrn_wo, router_w, router_b, moe_w_gate, moe_w_up, moe_w_down):
    nb, s_len, d = x.shape
    nm = meta_tokens.shape[0]
    depth = ln_mix_g.shape[0]
    alpha = float((2 * depth) ** 0.25)
    zero_bias = jnp.zeros((d,), F32)

    meta = jnp.broadcast_to(meta_tokens[None].astype(x.dtype), (nb, nm, d)).reshape(nb * nm, d)
    h = jnp.concatenate([x.reshape(nb * s_len, d), meta], axis=0)
    hb = h.astype(BF16)
    p_lb = jax.nn.softmax(hgrn_lb_logits.astype(F32), axis=0)
    lower_bounds = jnp.cumsum(p_lb, axis=0) - p_lb[0]

    for i in range(depth):
        j = i // N_MIXERS
        kind = i % N_MIXERS
        if kind == 0:
            u = mm_glu(hb, conv_pw1_w[j].astype(BF16), conv_pw1_b[j])
            mix_in = conv_ln_swish(u, conv_dw_w[j], conv_dw_b[j], conv_ln_g[j], conv_ln_b[j], nb, s_len, nm)
            w_out, b_out = conv_pw2_w[j], conv_pw2_b[j]
        elif kind == 1:
            mix_in = mla_mixer(hb, mla_wdq[j], mla_q_norm_g[j], mla_wuq[j], mla_wdkv[j], mla_kv_norm_g[j],
                               mla_wukv[j], nb, s_len, nm)
            w_out, b_out = mla_wo[j], zero_bias
        else:
            qfig = mm_plain(hb, hgrn_w_in[j].astype(BF16))
            mix_in = hgrn_mixer(qfig, lower_bounds[i], hgrn_norm_g[j], nb, s_len, nm)
            w_out, b_out = hgrn_wo[j], zero_bias
        h, hb = mm_res_ln(mix_in, h, w_out.astype(BF16), b_out, ln_mix_g[i], ln_mix_b[i], alpha)
        n_rows_out = nb * s_len if i == depth - 1 else h.shape[0]
        h, hb = moe_layer(h, hb, router_w, router_b, moe_w_gate, moe_w_up, moe_w_down, i,
                          ln_ffn_g[i], ln_ffn_b[i], alpha, n_rows_out)
    return h.reshape(nb, s_len, d)
```

```python
import functools

import jax
import jax.numpy as jnp
from jax import lax
from jax.experimental import pallas as pl
from jax.experimental.pallas import tpu as pltpu

F32 = jnp.float32
BF16 = jnp.bfloat16
I32 = jnp.int32
U32 = jnp.uint32

LANES = 128
CHUNK = 64
N_MIXERS = 3
CONV_WIDTH = 31
CONV_HALO = 32
CONV_LANES = 256
HEAD_DIM = 128
MLA_ROPE = 64
ROPE_THETA = 10000.0
HGRN_CHUNK = 128
N_EXPERTS = 32
N_GROUPS = 4
EXPERTS_PER_GROUP = N_EXPERTS // N_GROUPS
MOE_BLOCK = 512
COMBINE_ROWS = 16
LN_EPS = 1e-5
RMS_EPS = 1e-6
VMEM_LIMIT = 52 * 1024 * 1024


def _pick(n, cands):
    for c in cands:
        if n % c == 0:
            return c
    raise ValueError(f"no tile for {n} in {cands}")


_ROW_TILES = (768, 512, 384, 256, 192, 176, 128, 96, 64, 48, 32, 16)


def _cparams(sem, vmem=VMEM_LIMIT):
    return pltpu.CompilerParams(dimension_semantics=sem, vmem_limit_bytes=vmem)


def _layer_norm(x, g, b):
    mu = jnp.mean(x, axis=-1, keepdims=True)
    xc = x - mu
    var = jnp.mean(xc * xc, axis=-1, keepdims=True)
    return xc * lax.rsqrt(var + LN_EPS) * g + b


def _rms_norm(x, g):
    return x * lax.rsqrt(jnp.mean(x * x, axis=-1, keepdims=True) + RMS_EPS) * g


def _sigmoid(x):
    return 1.0 / (1.0 + jnp.exp(-x))


def _pack_bf16_pair(lo, hi):
    lo_b = lax.bitcast_convert_type(lo.astype(BF16).astype(F32), U32)
    hi_b = lax.bitcast_convert_type(hi.astype(BF16).astype(F32), U32)
    return (hi_b & jnp.uint32(0xFFFF0000)) | (lo_b >> 16)


def _unpack_bf16_pair(w):
    lo = lax.bitcast_convert_type(w << 16, F32)
    hi = lax.bitcast_convert_type(w & jnp.uint32(0xFFFF0000), F32)
    return lo, hi


def _write_meta_rows(meta_ref, o_ref):
    o_ref[...] = jnp.zeros(o_ref.shape, o_ref.dtype)
    o_ref[0:meta_ref.shape[0], :] = meta_ref[...]


def _rows_call(body, n_rows, tm, row_ins, full_ins, outs, name):
    grid = (n_rows // tm,)
    in_specs = [pl.BlockSpec((tm, a.shape[1]), lambda i: (i, 0)) for a in row_ins]
    in_specs += [pl.BlockSpec(a.shape, lambda i, nd=a.ndim: (0,) * nd, pipeline_mode=pl.Buffered(1))
                 for a in full_ins]
    out_specs = [pl.BlockSpec((tm, n), lambda i: (i, 0)) for n, _ in outs]
    out_shape = [jax.ShapeDtypeStruct((n_rows, n), dt) for n, dt in outs]
    return pl.pallas_call(
        body, grid=grid, in_specs=in_specs, out_specs=out_specs, out_shape=out_shape,
        compiler_params=_cparams(("parallel",)), name=name)(*row_ins, *full_ins)


def _mm_res_ln_body(alpha, x_ref, h_ref, w_ref, bias_ref, g_ref, b_ref, ho_ref, hb_ref):
    acc = jnp.dot(x_ref[...], w_ref[...], preferred_element_type=F32)
    y = alpha * h_ref[...] + (acc + bias_ref[...])
    o = _layer_norm(y, g_ref[...], b_ref[...])
    ho_ref[...] = o
    hb_ref[...] = o.astype(BF16)


def mm_res_ln(x_bf, h, w_bf, bias, g, b, alpha):
    t, d = h.shape
    tm = _pick(t, (384, 256, 192, 176, 128, 96, 64, 48, 32, 16))
    return _rows_call(functools.partial(_mm_res_ln_body, alpha), t, tm, [x_bf, h],
                      [w_bf, bias.reshape(1, d), g.reshape(1, d), b.reshape(1, d)],
                      [(d, F32), (d, BF16)], "mm_res_ln")


def _glu_body(x_ref, wa_ref, wg_ref, ba_ref, bg_ref, u_ref):
    x = x_ref[...]
    a = jnp.dot(x, wa_ref[...], preferred_element_type=F32) + ba_ref[...]
    g = jnp.dot(x, wg_ref[...], preferred_element_type=F32) + bg_ref[...]
    u_ref[...] = a * _sigmoid(g)


def mm_glu(x_bf, w_bf, bias):
    t, k = x_bf.shape
    d = w_bf.shape[1] // 2
    tm = _pick(t, _ROW_TILES)
    tn = min(d, 1024)
    nj = d // tn
    bias2 = bias.reshape(1, 2 * d)
    return pl.pallas_call(
        _glu_body, grid=(nj, t // tm),
        in_specs=[pl.BlockSpec((tm, k), lambda j, i: (i, 0)),
                  pl.BlockSpec((k, tn), lambda j, i: (0, j)),
                  pl.BlockSpec((k, tn), lambda j, i: (0, j + nj)),
                  pl.BlockSpec((1, tn), lambda j, i: (0, j)),
                  pl.BlockSpec((1, tn), lambda j, i: (0, j + nj))],
        out_specs=pl.BlockSpec((tm, tn), lambda j, i: (i, j)),
        out_shape=jax.ShapeDtypeStruct((t, d), F32),
        compiler_params=_cparams(("parallel", "parallel")), name="mm_glu")(x_bf, w_bf, w_bf, bias2, bias2)


def _mm_plain_body(x_ref, w_ref, o_ref):
    o_ref[...] = jnp.dot(x_ref[...], w_ref[...], preferred_element_type=F32).astype(o_ref.dtype)


def mm_plain(x_bf, w_bf, out_dtype=F32):
    t, k = x_bf.shape
    n = w_bf.shape[1]
    tm = _pick(t, _ROW_TILES)
    tn = min(n, 2048)
    return pl.pallas_call(
        _mm_plain_body, grid=(n // tn, t // tm),
        in_specs=[pl.BlockSpec((tm, k), lambda j, i: (i, 0)),
                  pl.BlockSpec((k, tn), lambda j, i: (0, j))],
        out_specs=pl.BlockSpec((tm, tn), lambda j, i: (i, j)),
        out_shape=jax.ShapeDtypeStruct((t, n), out_dtype),
        compiler_params=_cparams(("parallel", "parallel")), name="mm_plain")(x_bf, w_bf)


def _conv_rows(ext_ref, w_ref, y_ref, sh_ref, rows, rc, lc):
    d = y_ref.shape[1]
    shift = CONV_HALO - (CONV_WIDTH - 1)
    sub = 8
    n_sh = sh_ref.shape[1]

    def lane_body(c, carry):
        l0 = pl.multiple_of(c * lc, lc)
        lanes = pl.ds(l0, lc)
        wv = w_ref[:, lanes]
        for b in range(1, sub):
            sh_ref[b - 1] = ext_ref[pl.ds(b, n_sh), lanes]
        for r in range(rows // rc):
            acc = None
            for k in range(CONV_WIDTH):
                b = (shift + k) % sub
                a = r * rc + (shift + k) - b
                src = ext_ref[pl.ds(a, rc), lanes] if b == 0 else sh_ref[b - 1, pl.ds(a, rc), :]
                term = src * wv[k:k + 1, :]
                acc = term if acc is None else acc + term
            y_ref[pl.ds(r * rc, rc), lanes] = acc
        return carry

    lax.fori_loop(0, d // lc, lane_body, 0)


def _conv_epilogue(y_ref, dwb_ref, g_ref, b_ref, o_ref):
    rows = y_ref.shape[0]
    step = 16

    def body(c, carry):
        sl = pl.ds(pl.multiple_of(c * step, step), step)
        z = _layer_norm(y_ref[sl, :] + dwb_ref[...], g_ref[...], b_ref[...])
        o_ref[sl, :] = (z * _sigmoid(z)).astype(BF16)
        return carry

    lax.fori_loop(0, rows // step, body, 0, unroll=min(4, rows // step))


def _conv_main_body(tr, nst, n_real, cur_ref, prev_ref, meta_ref, w_ref, dwb_ref, g_ref, b_ref, vmeta_ref, o_ref,
                    ext_ref, y_ref, sh_ref):
    step = pl.program_id(0)
    nm = meta_ref.shape[0]

    @pl.when(step < n_real)
    def _():
        @pl.when(step % nst == 0)
        def _():
            ext_ref[0:CONV_HALO - nm, :] = jnp.zeros((CONV_HALO - nm, ext_ref.shape[1]), F32)
            ext_ref[CONV_HALO - nm:CONV_HALO, :] = meta_ref[...]

        @pl.when(step % nst > 0)
        def _():
            ext_ref[0:CONV_HALO, :] = prev_ref[...]

        ext_ref[CONV_HALO:CONV_HALO + tr, :] = cur_ref[...]
        _conv_rows(ext_ref, w_ref, y_ref, sh_ref, tr, 64, CONV_LANES)
        _conv_epilogue(y_ref, dwb_ref, g_ref, b_ref, o_ref)

    @pl.when(step == n_real)
    def _():
        _write_meta_rows(vmeta_ref, o_ref)


def _conv_meta_body(meta_ref, w_ref, dwb_ref, g_ref, b_ref, o_ref, ext_ref, y_ref, sh_ref):
    nm = meta_ref.shape[0]
    ext_ref[0:CONV_HALO, :] = jnp.zeros((CONV_HALO, ext_ref.shape[1]), F32)
    ext_ref[CONV_HALO:CONV_HALO + nm, :] = meta_ref[...]
    _conv_rows(ext_ref, w_ref, y_ref, sh_ref, nm, nm, CONV_LANES)
    _conv_epilogue(y_ref, dwb_ref, g_ref, b_ref, o_ref)


def conv_ln_swish(u, dw_w, dw_b, ln_g, ln_b, nb, s_len, nm):
    t, d = u.shape
    tr_rows = nb * s_len
    tr = _pick(s_len, (512, 256))
    w_pad = jnp.concatenate([dw_w, jnp.zeros((CONV_HALO - CONV_WIDTH, d), F32)], axis=0)
    vecs = [dw_b.reshape(1, d), ln_g.reshape(1, d), ln_b.reshape(1, d)]
    meta_blk = tr_rows // nm
    out_meta = pl.pallas_call(
        _conv_meta_body, grid=(nb,),
        in_specs=[pl.BlockSpec((nm, d), lambda b: (meta_blk + b, 0)),
                  pl.BlockSpec((CONV_HALO, d), lambda b: (0, 0))]
        + [pl.BlockSpec((1, d), lambda b: (0, 0))] * 3,
        out_specs=pl.BlockSpec((nm, d), lambda b: (b, 0)),
        out_shape=jax.ShapeDtypeStruct((nb * nm, d), BF16),
        scratch_shapes=[pltpu.VMEM((CONV_HALO + nm, d), F32), pltpu.VMEM((nm, d), F32),
                        pltpu.VMEM((7, CONV_HALO + nm - 8, CONV_LANES), F32)],
        compiler_params=_cparams(("parallel",)), name="conv_meta")(u, w_pad, *vecs)
    nst = s_len // tr
    n_real = nb * nst
    halo_per_tile = tr // CONV_HALO
    assert nb * nm <= tr

    def real(g):
        return jnp.minimum(g, n_real - 1)

    return pl.pallas_call(
        functools.partial(_conv_main_body, tr, nst, n_real), grid=(n_real + 1,),
        in_specs=[pl.BlockSpec((tr, d), lambda g: (real(g), 0)),
                  pl.BlockSpec((CONV_HALO, d), lambda g: (jnp.maximum(real(g) * halo_per_tile - 1, 0), 0)),
                  pl.BlockSpec((nm, d), lambda g: (meta_blk + real(g) // nst, 0)),
                  pl.BlockSpec((CONV_HALO, d), lambda g: (0, 0))]
        + [pl.BlockSpec((1, d), lambda g: (0, 0))] * 3
        + [pl.BlockSpec((nb * nm, d), lambda g: (0, 0))],
        out_specs=pl.BlockSpec((tr, d), lambda g: (g, 0)),
        out_shape=jax.ShapeDtypeStruct((t, d), BF16),
        scratch_shapes=[pltpu.VMEM((CONV_HALO + tr, d), F32), pltpu.VMEM((tr, d), F32),
                        pltpu.VMEM((7, CONV_HALO + tr - 8, CONV_LANES), F32)],
        compiler_params=_cparams(("arbitrary",)), name="conv_main")(u, u, u, w_pad, *vecs, out_meta)


def _mla_proj_body(ql, kvl, hd, scale, x_ref, cos_ref, sin_ref, w1_ref, qg_ref, kvg_ref, wq_ref, wkv_ref,
                   qn_ref, qr_ref, kn_ref, v_ref, kr_ref):
    a = jnp.dot(x_ref[...], w1_ref[...], preferred_element_type=F32)
    cos = cos_ref[...]
    sin = sin_ref[...]
    cq = _rms_norm(a[:, :ql], qg_ref[...]).astype(BF16)
    ckv = _rms_norm(a[:, ql:ql + kvl], kvg_ref[...]).astype(BF16)
    r0 = ql + kvl
    kr_ref[...] = (a[:, r0:r0 + LANES] * cos + a[:, r0 + LANES:r0 + 2 * LANES] * sin).astype(BF16)
    qa = jnp.dot(cq, wq_ref[...], preferred_element_type=F32)
    nh = hd // LANES
    cos_t = jnp.tile(cos, (1, nh))
    sin_t = jnp.tile(sin, (1, nh))
    qn_ref[...] = (qa[:, :hd] * scale).astype(BF16)
    qr_ref[...] = ((qa[:, hd:2 * hd] * cos_t + qa[:, 2 * hd:] * sin_t) * scale).astype(BF16)
    kv = jnp.dot(ckv, wkv_ref[...], preferred_element_type=F32)
    kn_ref[...] = kv[:, :hd].astype(BF16)
    v_ref[...] = kv[:, hd:].astype(BF16)


def _attn_body(tq, nb, qn_ref, qr_ref, kn_ref, kr_ref, v_ref, knm_ref, krm_ref, vm_ref, ometa_ref, o_ref, kf_ref):
    @pl.when(pl.program_id(0) == nb)
    def _():
        _write_meta_rows(ometa_ref, o_ref)

    @pl.when(pl.program_id(0) < nb)
    def _():
        _attn_tiles(tq, qn_ref, qr_ref, kn_ref, kr_ref, v_ref, knm_ref, krm_ref, vm_ref, o_ref, kf_ref)


def _attn_tiles(tq, qn_ref, qr_ref, kn_ref, kr_ref, v_ref, knm_ref, krm_ref, vm_ref, o_ref, kf_ref):
    s_len = qn_ref.shape[0]
    nt = (((1,), (1,)), ((), ()))
    kf_ref[:, :LANES] = kn_ref[...]
    kf_ref[:, LANES:] = kr_ref[...]
    km = jnp.concatenate([knm_ref[...], krm_ref[...]], axis=-1)
    vm = vm_ref[...]
    row_c = lax.broadcasted_iota(I32, (tq, tq), 0) // CHUNK
    col_c = lax.broadcasted_iota(I32, (tq, tq), 1) // CHUNK
    visible = col_c <= row_c
    for i in range(s_len // tq):
        r0 = i * tq
        q = jnp.concatenate([qn_ref[r0:r0 + tq, :], qr_ref[r0:r0 + tq, :]], axis=-1)
        s_m = lax.dot_general(q, km, nt, preferred_element_type=F32)
        s_d = lax.dot_general(q, kf_ref[r0:r0 + tq, :], nt, preferred_element_type=F32)
        s_d = jnp.where(visible, s_d, -jnp.inf)
        m = jnp.maximum(jnp.max(s_m, axis=-1, keepdims=True), jnp.max(s_d, axis=-1, keepdims=True))
        if i > 0:
            s_p = lax.dot_general(q, kf_ref[0:r0, :], nt, preferred_element_type=F32)
            m = jnp.maximum(m, jnp.max(s_p, axis=-1, keepdims=True))
        p_m = jnp.exp(s_m - m)
        p_d = jnp.exp(s_d - m)
        l = jnp.sum(p_m, axis=-1, keepdims=True) + jnp.sum(p_d, axis=-1, keepdims=True)
        acc = jnp.dot(p_m.astype(BF16), vm, preferred_element_type=F32)
        acc = acc + jnp.dot(p_d.astype(BF16), v_ref[r0:r0 + tq, :], preferred_element_type=F32)
        if i > 0:
            p_p = jnp.exp(s_p - m)
            l = l + jnp.sum(p_p, axis=-1, keepdims=True)
            acc = acc + jnp.dot(p_p.astype(BF16), v_ref[0:r0, :], preferred_element_type=F32)
        o_ref[r0:r0 + tq, :] = (acc / l).astype(BF16)


def _attn_meta_body(qn_ref, qr_ref, kn_ref, kr_ref, v_ref, o_ref):
    nt = (((1,), (1,)), ((), ()))
    q = jnp.concatenate([qn_ref[...], qr_ref[...]], axis=-1)
    k = jnp.concatenate([kn_ref[...], kr_ref[...]], axis=-1)
    s = lax.dot_general(q, k, nt, preferred_element_type=F32)
    p = jnp.exp(s - jnp.max(s, axis=-1, keepdims=True))
    l = jnp.sum(p, axis=-1, keepdims=True)
    o_ref[...] = (jnp.dot(p.astype(BF16), v_ref[...], preferred_element_type=F32) / l).astype(BF16)


def _rope_rows(n_pos):
    inv = ROPE_THETA ** (-jnp.arange(0, MLA_ROPE, 2, dtype=F32) / MLA_ROPE)
    ang = jnp.arange(n_pos, dtype=F32)[:, None] * inv[None, :]
    pad = jnp.zeros((n_pos, LANES - MLA_ROPE), F32)
    cos = jnp.concatenate([jnp.cos(ang), jnp.cos(ang), pad], axis=1)
    sin = jnp.concatenate([jnp.sin(ang), jnp.sin(ang), pad], axis=1)
    return cos, sin


def _pad_rope_cols(w):
    half = MLA_ROPE // 2
    z = jnp.zeros((w.shape[0], LANES - MLA_ROPE), w.dtype)
    rot = jnp.concatenate([-w[:, half:], w[:, :half]], axis=1)
    return jnp.concatenate([w, z], axis=1), jnp.concatenate([rot, z], axis=1)


def mla_mixer(hb, wdq, q_norm_g, wuq, wdkv, kv_norm_g, wukv, nb, s_len, nm):
    t, d = hb.shape
    ql = wdq.shape[1]
    kvl = kv_norm_g.shape[0]
    nh = wuq.shape[1] // (HEAD_DIM + MLA_ROPE)
    hd = nh * HEAD_DIM
    tr_rows = nb * s_len
    scale = float((HEAD_DIM + MLA_ROPE) ** -0.5)
    kr_w, kr_rot = _pad_rope_cols(wdkv[:, kvl:])
    w1 = jnp.concatenate([wdq, wdkv[:, :kvl], kr_w, kr_rot], axis=1).astype(BF16)
    wuq3 = wuq.reshape(ql, nh, HEAD_DIM + MLA_ROPE)
    q_rope = wuq3[:, :, HEAD_DIM:]
    half = MLA_ROPE // 2
    zq = jnp.zeros((ql, nh, LANES - MLA_ROPE), F32)
    q_rope_p = jnp.concatenate([q_rope, zq], axis=2).reshape(ql, hd)
    q_rot_p = jnp.concatenate([-q_rope[:, :, half:], q_rope[:, :, :half], zq], axis=2).reshape(ql, hd)
    wq = jnp.concatenate([wuq3[:, :, :HEAD_DIM].reshape(ql, hd), q_rope_p, q_rot_p], axis=1).astype(BF16)
    wukv3 = wukv.reshape(kvl, nh, 2 * HEAD_DIM)
    wkv = jnp.concatenate([wukv3[:, :, :HEAD_DIM].reshape(kvl, hd),
                           wukv3[:, :, HEAD_DIM:].reshape(kvl, hd)], axis=1).astype(BF16)
    cos_p, sin_p = _rope_rows(nm + s_len)
    cos_rows = jnp.concatenate([jnp.tile(cos_p[nm:], (nb, 1)), jnp.tile(cos_p[:nm], (nb, 1))], axis=0)
    sin_rows = jnp.concatenate([jnp.tile(sin_p[nm:], (nb, 1)), jnp.tile(sin_p[:nm], (nb, 1))], axis=0)

    tm = _pick(t, (256, 128, 96, 64, 48, 32, 16))
    qn, qr, kn, v, kr = _rows_call(
        functools.partial(_mla_proj_body, ql, kvl, hd, scale), t, tm, [hb, cos_rows, sin_rows],
        [w1, q_norm_g.reshape(1, ql), kv_norm_g.reshape(1, kvl), wq, wkv],
        [(hd, BF16), (hd, BF16), (hd, BF16), (hd, BF16), (LANES, BF16)], "mla_proj")

    meta_blk = tr_rows // nm
    mspec = lambda: pl.BlockSpec((nm, LANES), lambda b, h: (meta_blk + b, h))
    mspec0 = lambda: pl.BlockSpec((nm, LANES), lambda b, h: (meta_blk + b, 0))
    o_meta = pl.pallas_call(
        _attn_meta_body, grid=(nb, nh),
        in_specs=[mspec(), mspec(), mspec(), mspec0(), mspec()],
        out_specs=pl.BlockSpec((nm, LANES), lambda b, h: (b, h)),
        out_shape=jax.ShapeDtypeStruct((nb * nm, hd), BF16),
        compiler_params=_cparams(("parallel", "parallel")), name="attn_meta")(qn, qr, kn, kr, v)

    tq = 256
    assert nb * nm <= s_len

    def real(b):
        return jnp.minimum(b, nb - 1)

    kspec = lambda: pl.BlockSpec((s_len, LANES), lambda b, h: (real(b), h))
    kspec0 = lambda: pl.BlockSpec((s_len, LANES), lambda b, h: (real(b), 0))
    m2 = lambda: pl.BlockSpec((nm, LANES), lambda b, h: (meta_blk + real(b), h))
    m20 = lambda: pl.BlockSpec((nm, LANES), lambda b, h: (meta_blk + real(b), 0))
    return pl.pallas_call(
        functools.partial(_attn_body, tq, nb), grid=(nb + 1, nh),
        in_specs=[kspec(), kspec(), kspec(), kspec0(), kspec(), m2(), m20(), m2(),
                  pl.BlockSpec((nb * nm, LANES), lambda b, h: (0, h))],
        out_specs=pl.BlockSpec((s_len, LANES), lambda b, h: (b, h)),
        out_shape=jax.ShapeDtypeStruct((t, hd), BF16),
        scratch_shapes=[pltpu.VMEM((s_len, 2 * LANES), BF16)],
        compiler_params=_cparams(("arbitrary", "arbitrary")), name="attn_main")(
            qn, qr, kn, kr, v, kn, kr, v, o_meta)


def _hgrn_gates(fz, lb):
    return lb + (1.0 - lb) * _sigmoid(fz), (1.0 - lb) * _sigmoid(-fz)


def _hgrn_level_index(c):
    t = lax.broadcasted_iota(I32, (c, c), 0)
    s = lax.broadcasted_iota(I32, (c, c), 1)
    lvl = 31 - lax.clz(t ^ s)
    return jnp.where(s > t, -2, lvl)


def _hgrn_block(q, f, k, iv, st, lvl):
    c = q.shape[0]
    nt = (((1,), (1,)), ((), ()))
    tn = (((0,), (0,)), ((), ()))
    row = lax.broadcasted_iota(I32, (c, HEAD_DIM), 0)
    scores = jnp.where(lvl == -1, lax.dot_general(q.astype(BF16), k.astype(BF16), nt,
                                                  preferred_element_type=F32), 0.0)
    qa = q * f
    kb = k
    tot = f
    h, idx = 1, 0
    while h < c:
        prod = lax.dot_general(qa.astype(BF16), kb.astype(BF16), nt, preferred_element_type=F32)
        scores = jnp.where(lvl == idx, prod, scores)
        right = (row & h) != 0
        left_tot = pltpu.roll(tot, h, 0)
        right_tot = pltpu.roll(tot, c - h, 0)
        qa = qa * jnp.where(right, left_tot, 1.0)
        kb = kb * jnp.where(right, 1.0, right_tot)
        tot = tot * jnp.where(right, left_tot, right_tot)
        h, idx = 2 * h, idx + 1
    ib = iv.astype(BF16)
    o = jnp.dot(scores.astype(BF16), ib, preferred_element_type=F32)
    o = o + lax.dot_general(qa.astype(BF16), st.astype(BF16), nt, preferred_element_type=F32)
    st_new = st * tot[0:1, :] + lax.dot_general(ib, kb.astype(BF16), tn, preferred_element_type=F32)
    return o, st_new


def _hgrn_out(o, gate, ng):
    o = o * lax.rsqrt(jnp.mean(o * o, axis=-1, keepdims=True) + RMS_EPS) * ng
    return (o * (gate * _sigmoid(gate))).astype(BF16)


def _hgrn_meta_body(gh, q_ref, fz_ref, i_ref, g_ref, lb_ref, ng_ref, o_ref, st_ref):
    nm = q_ref.shape[0]
    c = HGRN_CHUNK
    lvl = _hgrn_level_index(c)
    zeros = jnp.zeros((c - nm, HEAD_DIM), F32)
    for g in range(gh):
        sl = slice(g * HEAD_DIM, (g + 1) * HEAD_DIM)
        f, k = _hgrn_gates(fz_ref[:, sl], lb_ref[:, sl])
        o, st = _hgrn_block(jnp.concatenate([zeros, q_ref[:, sl]], axis=0),
                            jnp.concatenate([zeros + 1.0, f], axis=0),
                            jnp.concatenate([zeros, k], axis=0),
                            jnp.concatenate([zeros, i_ref[:, sl]], axis=0),
                            jnp.zeros((HEAD_DIM, HEAD_DIM), F32), lvl)
        st_ref[0, g] = st
        o_ref[:, sl] = _hgrn_out(o[c - nm:], g_ref[:, sl], ng_ref[...])


def _hgrn_main_body(gh, ts, nst, n_real, q_ref, fz_ref, i_ref, g_ref, lb_ref, ng_ref, st0_ref, ometa_ref, o_ref,
                    st_ref):
    c = HGRN_CHUNK
    step = pl.program_id(1)

    @pl.when(step == n_real)
    def _():
        _write_meta_rows(ometa_ref, o_ref)

    @pl.when(step < n_real)
    def _():
        lvl = _hgrn_level_index(c)

        @pl.when(step % nst == 0)
        def _():
            st_ref[...] = st0_ref[0]

        def chunk_body(cidx, carry):
            rows = pl.ds(pl.multiple_of(cidx * c, c), c)
            for g in range(gh):
                sl = slice(g * HEAD_DIM, (g + 1) * HEAD_DIM)
                f, k = _hgrn_gates(fz_ref[rows, sl], lb_ref[:, sl])
                o, st = _hgrn_block(q_ref[rows, sl], f, k, i_ref[rows, sl], st_ref[g], lvl)
                st_ref[g] = st
                o_ref[rows, sl] = _hgrn_out(o, g_ref[rows, sl], ng_ref[...])
            return carry

        lax.fori_loop(0, ts // c, chunk_body, 0)


def hgrn_mixer(qfig, lb, norm_g, nb, s_len, nm):
    t, d4 = qfig.shape
    d = d4 // 4
    nh = d // HEAD_DIM
    gh = 4 if nh % 4 == 0 else 1
    gw = gh * HEAD_DIM
    ng_blocks = d // gw
    tr_rows = nb * s_len
    meta_blk = tr_rows // nm
    lb2 = lb.reshape(1, d)
    ng2 = norm_g.reshape(1, HEAD_DIM)

    def mspec(sec):
        return pl.BlockSpec((nm, gw), lambda b, h: (meta_blk + b, sec * ng_blocks + h))

    o_meta, st0 = pl.pallas_call(
        functools.partial(_hgrn_meta_body, gh), grid=(nb, ng_blocks),
        in_specs=[mspec(0), mspec(1), mspec(2), mspec(3),
                  pl.BlockSpec((1, gw), lambda b, h: (0, h)),
                  pl.BlockSpec((1, HEAD_DIM), lambda b, h: (0, 0))],
        out_specs=[pl.BlockSpec((nm, gw), lambda b, h: (b, h)),
                   pl.BlockSpec((1, gh, HEAD_DIM, HEAD_DIM), lambda b, h: (b, h, 0, 0))],
        out_shape=[jax.ShapeDtypeStruct((nb * nm, d), BF16),
                   jax.ShapeDtypeStruct((nb, nh, HEAD_DIM, HEAD_DIM), F32)],
        compiler_params=_cparams(("parallel", "parallel")), name="hgrn_meta")(qfig, qfig, qfig, qfig, lb2, ng2)

    ts = _pick(s_len, (512, 256))
    nst = s_len // ts
    n_real = nb * nst
    assert nb * nm <= ts

    def real(g):
        return jnp.minimum(g, n_real - 1)

    def rspec(sec):
        return pl.BlockSpec((ts, gw), lambda h, g: (real(g), sec * ng_blocks + h))

    return pl.pallas_call(
        functools.partial(_hgrn_main_body, gh, ts, nst, n_real), grid=(ng_blocks, n_real + 1),
        in_specs=[rspec(0), rspec(1), rspec(2), rspec(3),
                  pl.BlockSpec((1, gw), lambda h, g: (0, h)),
                  pl.BlockSpec((1, HEAD_DIM), lambda h, g: (0, 0)),
                  pl.BlockSpec((1, gh, HEAD_DIM, HEAD_DIM), lambda h, g: (real(g) // nst, h, 0, 0)),
                  pl.BlockSpec((nb * nm, gw), lambda h, g: (0, h))],
        out_specs=pl.BlockSpec((ts, gw), lambda h, g: (g, h)),
        out_shape=jax.ShapeDtypeStruct((t, d), BF16),
        scratch_shapes=[pltpu.VMEM((gh, HEAD_DIM, HEAD_DIM), F32)],
        compiler_params=_cparams(("parallel", "arbitrary")), name="hgrn_main")(
            qfig, qfig, qfig, qfig, lb2, ng2, st0, o_meta)


def _first_index_of_max(vals, idx, n, axis):
    mx = jnp.max(vals, axis=axis, keepdims=True)
    first = jnp.min(jnp.where(vals == mx, idx, n), axis=axis, keepdims=True)
    return mx, first


def _router_body(tm, h_ref, hb_ref, rw_ref, rb_ref, tri_ref, e_ref, gate_ref, rank_ref, cnt_ref, base_ref):
    @pl.when(pl.program_id(0) == 0)
    def _():
        base_ref[...] = jnp.zeros(base_ref.shape, F32)

    nt = (((1,), (1,)), ((), ()))
    hb = hb_ref[...]
    resid = (h_ref[...] - hb.astype(F32)).astype(BF16)
    main = lax.dot_general(rw_ref[...], hb, nt, preferred_element_type=F32)
    logits = (main[:N_EXPERTS] + main[N_EXPERTS:2 * N_EXPERTS] + main[2 * N_EXPERTS:]
              + lax.dot_general(rw_ref[:N_EXPERTS, :], resid, nt, preferred_element_type=F32))
    scores = _sigmoid(logits)
    sel = scores + rb_ref[...]
    g, epg = N_GROUPS, EXPERTS_PER_GROUP
    sel3 = sel.reshape(g, epg, tm)
    sc3 = scores.reshape(g, epg, tm)
    idx3 = lax.broadcasted_iota(I32, (g, epg, tm), 1)
    m1, i1 = _first_index_of_max(sel3, idx3, epg, 1)
    rest = jnp.where(idx3 == i1, -jnp.inf, sel3)
    m2, i2 = _first_index_of_max(rest, idx3, epg, 1)
    gidx = lax.broadcasted_iota(I32, (g, 1, tm), 0)
    _, gtop3 = _first_index_of_max(m1 + m2, gidx, g, 0)
    pick = gidx == gtop3
    gtop = gtop3[0]
    l1 = jnp.sum(jnp.where(pick, i1, 0), axis=0)
    l2 = jnp.sum(jnp.where(pick, i2, 0), axis=0)
    sc_in = jnp.sum(jnp.where(pick, sc3, 0.0), axis=0)
    idx2 = lax.broadcasted_iota(I32, (epg, tm), 0)
    s1 = jnp.sum(jnp.where(idx2 == l1, sc_in, 0.0), axis=0, keepdims=True)
    s2 = jnp.sum(jnp.where(idx2 == l2, sc_in, 0.0), axis=0, keepdims=True)
    e1 = gtop * epg + l1
    e2 = gtop * epg + l2
    e_ref[0:1, :] = e1
    e_ref[1:2, :] = e2
    den = s1 + s2
    gate_ref[0:1, :] = s1 / den
    gate_ref[1:2, :] = s2 / den
    eidx = lax.broadcasted_iota(I32, (N_EXPERTS, tm), 0)
    oh1 = (eidx == e1).astype(F32)
    oh2 = (eidx == e2).astype(F32)
    oh = jnp.concatenate([oh1, oh2], axis=0).astype(BF16)
    pre = jnp.dot(oh, tri_ref[...], preferred_element_type=F32)
    base = base_ref[:, 0:1]
    tot1 = jnp.sum(oh1, axis=1, keepdims=True)
    tot2 = jnp.sum(oh2, axis=1, keepdims=True)
    r1 = jnp.sum(oh1 * (base + pre[:N_EXPERTS]), axis=0, keepdims=True)
    r2 = jnp.sum(oh2 * (base + tot1 + pre[N_EXPERTS:]), axis=0, keepdims=True)
    rank_ref[0:1, :] = r1.astype(I32)
    rank_ref[1:2, :] = r2.astype(I32)
    new_base = jnp.broadcast_to(base + tot1 + tot2, base_ref.shape)
    base_ref[...] = new_base
    cnt_ref[...] = new_base


def moe_route(h, hb, router_w, router_b):
    t, d = h.shape
    tm = _pick(t, (256, 128, 96, 64, 32))
    tri = (lax.broadcasted_iota(I32, (tm, tm), 0) < lax.broadcasted_iota(I32, (tm, tm), 1)).astype(BF16)
    rw_t = router_w.T.astype(F32)
    rw1 = rw_t.astype(BF16)
    rw2 = (rw_t - rw1.astype(F32)).astype(BF16)
    rw3 = (rw_t - rw1.astype(F32) - rw2.astype(F32)).astype(BF16)
    rw_pieces = jnp.concatenate([rw1, rw2, rw3], axis=0)
    e, gate, rank, cnt = pl.pallas_call(
        functools.partial(_router_body, tm), grid=(t // tm,),
        in_specs=[pl.BlockSpec((tm, d), lambda i: (i, 0)),
                  pl.BlockSpec((tm, d), lambda i: (i, 0)),
                  pl.BlockSpec((3 * N_EXPERTS, d), lambda i: (0, 0)),
                  pl.BlockSpec((N_EXPERTS, 1), lambda i: (0, 0)),
                  pl.BlockSpec((tm, tm), lambda i: (0, 0))],
        out_specs=[pl.BlockSpec((2, tm), lambda i: (0, i)),
                   pl.BlockSpec((2, tm), lambda i: (0, i)),
                   pl.BlockSpec((2, tm), lambda i: (0, i)),
                   pl.BlockSpec((N_EXPERTS, LANES), lambda i: (0, 0))],
        out_shape=[jax.ShapeDtypeStruct((2, t), I32), jax.ShapeDtypeStruct((2, t), F32),
                   jax.ShapeDtypeStruct((2, t), I32), jax.ShapeDtypeStruct((N_EXPERTS, LANES), F32)],
        scratch_shapes=[pltpu.VMEM((N_EXPERTS, LANES), F32)],
        compiler_params=_cparams(("arbitrary",)), name="moe_route")(
            h, hb, rw_pieces, router_b.reshape(N_EXPERTS, 1), tri)
    return e, gate, rank, cnt[:, 0].astype(I32)


def _dispatch_body(tt, zero_flag_ref, dest_ref, hb_ref, xb_hbm, stage_ref, zero_ref, sem):
    half = stage_ref.shape[1]

    @pl.when(pl.program_id(0) == 0)
    def _():
        zero_ref[...] = jnp.zeros(zero_ref.shape, U32)

        def zfill(b, carry):
            @pl.when(zero_flag_ref[b] > 0)
            def _():
                start = pl.multiple_of(b * MOE_BLOCK, MOE_BLOCK)
                cp = pltpu.make_async_copy(zero_ref, xb_hbm.at[pl.ds(start, MOE_BLOCK), :], sem)
                cp.start()
                cp.wait()
            return carry

        lax.fori_loop(0, zero_flag_ref.shape[0], zfill, 0)

    stage_ref[...] = _pack_bf16_pair(hb_ref[:, :half].astype(F32), hb_ref[:, half:].astype(F32))

    def issue(j, carry):
        for k in range(2):
            pltpu.make_async_copy(stage_ref.at[pl.ds(j, 1), :],
                                  xb_hbm.at[pl.ds(dest_ref[k, j], 1), :], sem).start(priority=k)
        return carry

    lax.fori_loop(0, tt, issue, 0, unroll=8)
    for k in range(2):
        pltpu.make_async_copy(stage_ref, xb_hbm.at[pl.ds(0, tt), :], sem).wait()


def moe_dispatch(hb, dest, zero_flag, n_slots):
    t, d = hb.shape
    tt = _pick(t, (256, 128, 96, 64, 32))
    return pl.pallas_call(
        functools.partial(_dispatch_body, tt),
        grid_spec=pltpu.PrefetchScalarGridSpec(
            num_scalar_prefetch=1, grid=(t // tt,),
            in_specs=[pl.BlockSpec((2, tt), lambda i, zf: (0, i), memory_space=pltpu.SMEM),
                      pl.BlockSpec((tt, d), lambda i, zf: (i, 0))],
            out_specs=pl.BlockSpec(memory_space=pl.ANY),
            scratch_shapes=[pltpu.VMEM((tt, d // 2), U32), pltpu.VMEM((MOE_BLOCK, d // 2), U32),
                            pltpu.SemaphoreType.DMA(())]),
        out_shape=jax.ShapeDtypeStruct((n_slots, d // 2), U32),
        compiler_params=_cparams(("arbitrary",)), name="moe_dispatch")(zero_flag, dest, hb)


def _expert_body(be_ref, nu_ref, x_ref, wg_ref, wu_ref, wd_ref, y_ref, wgb_ref, wub_ref, wdb_ref):
    i = pl.program_id(0)
    used = i < nu_ref[0]
    half = x_ref.shape[1]

    @pl.when(jnp.logical_or(i == 0, be_ref[i] != be_ref[jnp.maximum(i - 1, 0)]))
    def _():
        wgb_ref[...] = wg_ref[0, 0].astype(BF16)
        wub_ref[...] = wu_ref[0, 0].astype(BF16)
        wdb_ref[...] = wd_ref[0, 0].astype(BF16)

    @pl.when(used)
    def _():
        lo, hi = _unpack_bf16_pair(x_ref[...])
        lo = lo.astype(BF16)
        hi = hi.astype(BF16)
        g = (jnp.dot(lo, wgb_ref[:half, :], preferred_element_type=F32)
             + jnp.dot(hi, wgb_ref[half:, :], preferred_element_type=F32))
        u = (jnp.dot(lo, wub_ref[:half, :], preferred_element_type=F32)
             + jnp.dot(hi, wub_ref[half:, :], preferred_element_type=F32))
        a = (g * _sigmoid(g) * u).astype(BF16)
        y = jnp.dot(a, wdb_ref[...], preferred_element_type=F32)
        y_ref[...] = _pack_bf16_pair(y[:, :half], y[:, half:])

    @pl.when(jnp.logical_not(used))
    def _():
        y_ref[...] = jnp.zeros(y_ref.shape, U32)


def moe_experts(xb, blk_expert, n_used, wg, wu, wd, layer):
    n_slots, half = xb.shape
    d = 2 * half
    ff = wg.shape[3]
    nblk = n_slots // MOE_BLOCK
    return pl.pallas_call(
        _expert_body,
        grid_spec=pltpu.PrefetchScalarGridSpec(
            num_scalar_prefetch=2, grid=(nblk,),
            in_specs=[pl.BlockSpec((MOE_BLOCK, half), lambda i, be, nu: (i, 0)),
                      pl.BlockSpec((1, 1, d, ff), lambda i, be, nu: (layer, be[i], 0, 0)),
                      pl.BlockSpec((1, 1, d, ff), lambda i, be, nu: (layer, be[i], 0, 0)),
                      pl.BlockSpec((1, 1, ff, d), lambda i, be, nu: (layer, be[i], 0, 0))],
            out_specs=pl.BlockSpec((MOE_BLOCK, half), lambda i, be, nu: (i, 0)),
            scratch_shapes=[pltpu.VMEM((d, ff), BF16), pltpu.VMEM((d, ff), BF16), pltpu.VMEM((ff, d), BF16)]),
        out_shape=jax.ShapeDtypeStruct((n_slots, half), U32),
        compiler_params=_cparams(("arbitrary",)), name="moe_experts")(
            blk_expert, n_used, xb, wg, wu, wd)


def _combine_body(tt, alpha, n_tiles, dest_ref, dnext_ref, gate_ref, h_ref, g_ref, b_ref, y_hbm, ho_ref, hb_ref,
                  buf_ref, sem):
    i = pl.program_id(0)
    slot = i % 2
    nslot = 1 - slot

    def gather(idx_ref, j, to_slot):
        for k in range(2):
            pltpu.make_async_copy(y_hbm.at[pl.ds(idx_ref[k, j], 1), :],
                                  buf_ref.at[to_slot, k, pl.ds(j, 1), :], sem.at[to_slot]).start(priority=k)

    def wait_slot(s):
        for k in range(2):
            pltpu.make_async_copy(y_hbm.at[pl.ds(0, tt), :], buf_ref.at[s, k], sem.at[s]).wait()

    @pl.when(i == 0)
    def _():
        def first(j, carry):
            gather(dest_ref, j, 0)
            return carry
        lax.fori_loop(0, tt, first, 0, unroll=8)

    wait_slot(slot)

    def ahead(j, carry):
        gather(dnext_ref, j, nslot)
        return carry

    lax.fori_loop(0, tt, ahead, 0, unroll=8)

    def rows_body(c, carry):
        r0 = pl.multiple_of(c * COMBINE_ROWS, COMBINE_ROWS)
        rows = pl.ds(r0, COMBINE_ROWS)
        gate = gate_ref[rows, :]
        lo0, hi0 = _unpack_bf16_pair(buf_ref[slot, 0, rows, :])
        lo1, hi1 = _unpack_bf16_pair(buf_ref[slot, 1, rows, :])
        ffn = jnp.concatenate([gate[:, 0:1] * lo0 + gate[:, 1:2] * lo1,
                               gate[:, 0:1] * hi0 + gate[:, 1:2] * hi1], axis=-1)
        o = _layer_norm(alpha * h_ref[rows, :] + ffn, g_ref[...], b_ref[...])
        ho_ref[rows, :] = o
        hb_ref[rows, :] = o.astype(BF16)
        return carry

    lax.fori_loop(0, tt // COMBINE_ROWS, rows_body, 0, unroll=4)

    @pl.when(i == n_tiles - 1)
    def _():
        wait_slot(nslot)


def moe_combine(yb, dest, gate_t, h, ln_g, ln_b, alpha, n_rows):
    t, d = h.shape
    tt = next(c for c in (256, 128, 96, 64, 32) if t % c == 0 and n_rows % c == 0)
    n_tiles = n_rows // tt
    return pl.pallas_call(
        functools.partial(_combine_body, tt, alpha, n_tiles), grid=(n_tiles,),
        in_specs=[pl.BlockSpec((2, tt), lambda i: (0, i), memory_space=pltpu.SMEM),
                  pl.BlockSpec((2, tt), lambda i: (0, jnp.minimum(i + 1, n_tiles - 1)), memory_space=pltpu.SMEM),
                  pl.BlockSpec((tt, 2), lambda i: (i, 0)),
                  pl.BlockSpec((tt, d), lambda i: (i, 0)),
                  pl.BlockSpec((1, d), lambda i: (0, 0)),
                  pl.BlockSpec((1, d), lambda i: (0, 0)),
                  pl.BlockSpec(memory_space=pl.ANY)],
        out_specs=[pl.BlockSpec((tt, d), lambda i: (i, 0)), pl.BlockSpec((tt, d), lambda i: (i, 0))],
        out_shape=[jax.ShapeDtypeStruct((n_rows, d), F32), jax.ShapeDtypeStruct((n_rows, d), BF16)],
        scratch_shapes=[pltpu.VMEM((2, 2, tt, d // 2), U32), pltpu.SemaphoreType.DMA((2,))],
        compiler_params=_cparams(("arbitrary",)), name="moe_combine")(
            dest, dest, gate_t, h, ln_g.reshape(1, d), ln_b.reshape(1, d), yb)


def moe_layer(h, hb, router_w, router_b, w_gate, w_up, w_down, layer, ln_g, ln_b, alpha, n_rows_out):
    t, d = h.shape
    e_idx, gate, rank, counts = moe_route(h, hb, router_w, router_b)
    padded = (counts + MOE_BLOCK - 1) // MOE_BLOCK * MOE_BLOCK
    pends = jnp.cumsum(padded)
    pstart = pends - padded
    experts = jnp.arange(N_EXPERTS, dtype=I32)[:, None, None]
    dest = jnp.sum(jnp.where(e_idx[None] == experts, pstart[:, None, None], 0), axis=0) + rank
    nblk = -(-(2 * t) // MOE_BLOCK) + N_EXPERTS
    n_used = (pends[-1] // MOE_BLOCK).astype(I32)
    blk = jnp.arange(nblk, dtype=I32)
    blk_first_row = jnp.minimum(blk, n_used - 1) * MOE_BLOCK
    blk_expert = jnp.minimum(jnp.sum((pends[None, :] <= blk_first_row[:, None]).astype(I32), axis=1),
                             N_EXPERTS - 1)
    zero_flag = ((blk >= n_used) | (blk == pends[blk_expert] // MOE_BLOCK - 1)).astype(I32)
    xb = moe_dispatch(hb, dest, zero_flag, nblk * MOE_BLOCK)
    yb = moe_experts(xb, blk_expert, n_used.reshape(1), w_gate, w_up, w_down, layer)
    return moe_combine(yb, dest, gate.T, h, ln_g, ln_b, alpha, n_rows_out)


def kernel(x, meta_tokens, ln_mix_g, ln_mix_b, ln_ffn_g, ln_ffn_b, conv_pw1_w, conv_pw1_b, conv_dw_w, conv_dw_b, conv_ln_g, conv_ln_b, conv_pw2_w, conv_pw2_b, mla_wdq, mla_q_norm_g, mla_wuq, mla_wdkv, mla_kv_norm_g, mla_wukv, mla_wo, hgrn_w_in, hgrn_lb_logits, hgrn_norm_g, hgrn_wo, router_w, router_b, moe_w_gate, moe_w_up, moe_w_down):
    nb, s_len, d = x.shape
    nm = meta_tokens.shape[0]
    depth = ln_mix_g.shape[0]
    alpha = float((2 * depth) ** 0.25)
    zero_bias = jnp.zeros((d,), F32)

    meta = jnp.broadcast_to(meta_tokens[None].astype(x.dtype), (nb, nm, d)).reshape(nb * nm, d)
    h = jnp.concatenate([x.reshape(nb * s_len, d), meta], axis=0)
    hb = h.astype(BF16)
    p_lb = jax.nn.softmax(hgrn_lb_logits.astype(F32), axis=0)
    lower_bounds = jnp.cumsum(p_lb, axis=0) - p_lb[0]

    for i in range(depth):
        j = i // N_MIXERS
        kind = i % N_MIXERS
        if kind == 0:
            u = mm_glu(hb, conv_pw1_w[j].astype(BF16), conv_pw1_b[j])
            mix_in = conv_ln_swish(u, conv_dw_w[j], conv_dw_b[j], conv_ln_g[j], conv_ln_b[j], nb, s_len, nm)
            w_out, b_out = conv_pw2_w[j], conv_pw2_b[j]
        elif kind == 1:
            mix_in = mla_mixer(hb, mla_wdq[j], mla_q_norm_g[j], mla_wuq[j], mla_wdkv[j], mla_kv_norm_g[j],
                               mla_wukv[j], nb, s_len, nm)
            w_out, b_out = mla_wo[j], zero_bias
        else:
            qfig = mm_plain(hb, hgrn_w_in[j].astype(BF16))
            mix_in = hgrn_mixer(qfig, lower_bounds[i], hgrn_norm_g[j], nb, s_len, nm)
            w_out, b_out = hgrn_wo[j], zero_bias
        h, hb = mm_res_ln(mix_in, h, w_out.astype(BF16), b_out, ln_mix_g[i], ln_mix_b[i], alpha)
        n_rows_out = nb * s_len if i == depth - 1 else h.shape[0]
        h, hb = moe_layer(h, hb, router_w, router_b, moe_w_gate, moe_w_up, moe_w_down, i,
                          ln_ffn_g[i], ln_ffn_b[i], alpha, n_rows_out)
    return h.reshape(nb, s_len, d)
```

```python
import functools

import jax
import jax.numpy as jnp
from jax import lax
from jax.experimental import pallas as pl
from jax.experimental.pallas import tpu as pltpu

F32 = jnp.float32
BF16 = jnp.bfloat16
I32 = jnp.int32
U32 = jnp.uint32

LANES = 128
CHUNK = 64
N_MIXERS = 3
CONV_WIDTH = 31
CONV_HALO = 32
CONV_LANES = 256
HEAD_DIM = 128
MLA_ROPE = 64
ROPE_THETA = 10000.0
HGRN_CHUNK = 128
N_EXPERTS = 32
N_GROUPS = 4
EXPERTS_PER_GROUP = N_EXPERTS // N_GROUPS
MOE_BLOCK = 512
COMBINE_ROWS = 16
LN_EPS = 1e-5
RMS_EPS = 1e-6
VMEM_LIMIT = 52 * 1024 * 1024


def _pick(n, cands):
    for c in cands:
        if n % c == 0:
            return c
    raise ValueError(f"no tile for {n} in {cands}")


_ROW_TILES = (768, 512, 384, 256, 192, 176, 128, 96, 64, 48, 32, 16)


def _cparams(sem, vmem=VMEM_LIMIT):
    return pltpu.CompilerParams(dimension_semantics=sem, vmem_limit_bytes=vmem)


def _layer_norm(x, g, b):
    mu = jnp.mean(x, axis=-1, keepdims=True)
    xc = x - mu
    var = jnp.mean(xc * xc, axis=-1, keepdims=True)
    return xc * lax.rsqrt(var + LN_EPS) * g + b


def _rms_norm(x, g):
    return x * lax.rsqrt(jnp.mean(x * x, axis=-1, keepdims=True) + RMS_EPS) * g


def _sigmoid(x):
    return 1.0 / (1.0 + jnp.exp(-x))


def _pack_bf16_pair(lo, hi):
    lo_b = lax.bitcast_convert_type(lo.astype(BF16).astype(F32), U32)
    hi_b = lax.bitcast_convert_type(hi.astype(BF16).astype(F32), U32)
    return (hi_b & jnp.uint32(0xFFFF0000)) | (lo_b >> 16)


def _unpack_bf16_pair(w):
    lo = lax.bitcast_convert_type(w << 16, F32)
    hi = lax.bitcast_convert_type(w & jnp.uint32(0xFFFF0000), F32)
    return lo, hi


def _write_meta_rows(meta_ref, o_ref):
    o_ref[...] = jnp.zeros(o_ref.shape, o_ref.dtype)
    o_ref[0:meta_ref.shape[0], :] = meta_ref[...]


def _rows_call(body, n_rows, tm, row_ins, full_ins, outs, name):
    grid = (n_rows // tm,)
    in_specs = [pl.BlockSpec((tm, a.shape[1]), lambda i: (i, 0)) for a in row_ins]
    in_specs += [pl.BlockSpec(a.shape, lambda i, nd=a.ndim: (0,) * nd, pipeline_mode=pl.Buffered(1))
                 for a in full_ins]
    out_specs = [pl.BlockSpec((tm, n), lambda i: (i, 0)) for n, _ in outs]
    out_shape = [jax.ShapeDtypeStruct((n_rows, n), dt) for n, dt in outs]
    return pl.pallas_call(
        body, grid=grid, in_specs=in_specs, out_specs=out_specs, out_shape=out_shape,
        compiler_params=_cparams(("parallel",)), name=name)(*row_ins, *full_ins)


def _mm_res_ln_body(alpha, x_ref, h_ref, w_ref, bias_ref, g_ref, b_ref, ho_ref, hb_ref):
    acc = jnp.dot(x_ref[...], w_ref[...], preferred_element_type=F32)
    y = alpha * h_ref[...] + (acc + bias_ref[...])
    o = _layer_norm(y, g_ref[...], b_ref[...])
    ho_ref[...] = o
    hb_ref[...] = o.astype(BF16)


def _mm_res_ln_tail_body(alpha, n_head, x_ref, h_ref, w_ref, bias_ref, g_ref, b_ref, tail_ref, ho_ref, hb_ref):
    tm = x_ref.shape[0]
    row = pl.program_id(0) * tm + lax.broadcasted_iota(I32, (tm, 1), 0)
    res = jnp.where(row < n_head, h_ref[...], tail_ref[...])
    acc = jnp.dot(x_ref[...], w_ref[...], preferred_element_type=F32)
    o = _layer_norm(alpha * res + (acc + bias_ref[...]), g_ref[...], b_ref[...])
    ho_ref[...] = o
    hb_ref[...] = o.astype(BF16)


def mm_res_ln(x_bf, h, w_bf, bias, g, b, alpha, tail=None):
    t, d = x_bf.shape[0], h.shape[1]
    tm = _pick(t, (384, 256, 192, 176, 128, 96, 64, 48, 32, 16))
    vecs = [w_bf, bias.reshape(1, d), g.reshape(1, d), b.reshape(1, d)]
    outs = [(d, F32), (d, BF16)]
    if tail is None:
        return _rows_call(functools.partial(_mm_res_ln_body, alpha), t, tm, [x_bf, h], vecs, outs, "mm_res_ln")
    n_head = h.shape[0]
    assert t - tm < n_head < t and tail.shape[0] == t - n_head
    tail_tile = jnp.concatenate([jnp.zeros((tm - tail.shape[0], d), tail.dtype), tail], axis=0)
    return _rows_call(functools.partial(_mm_res_ln_tail_body, alpha, n_head), t, tm, [x_bf, h],
                      vecs + [tail_tile], outs, "mm_res_ln")


def _glu_body(x_ref, wa_ref, wg_ref, ba_ref, bg_ref, u_ref):
    x = x_ref[...]
    a = jnp.dot(x, wa_ref[...], preferred_element_type=F32) + ba_ref[...]
    g = jnp.dot(x, wg_ref[...], preferred_element_type=F32) + bg_ref[...]
    u_ref[...] = a * _sigmoid(g)


def mm_glu(x_bf, w_bf, bias):
    t, k = x_bf.shape
    d = w_bf.shape[1] // 2
    tm = _pick(t, _ROW_TILES)
    tn = min(d, 1024)
    nj = d // tn
    bias2 = bias.reshape(1, 2 * d)
    return pl.pallas_call(
        _glu_body, grid=(nj, t // tm),
        in_specs=[pl.BlockSpec((tm, k), lambda j, i: (i, 0)),
                  pl.BlockSpec((k, tn), lambda j, i: (0, j)),
                  pl.BlockSpec((k, tn), lambda j, i: (0, j + nj)),
                  pl.BlockSpec((1, tn), lambda j, i: (0, j)),
                  pl.BlockSpec((1, tn), lambda j, i: (0, j + nj))],
        out_specs=pl.BlockSpec((tm, tn), lambda j, i: (i, j)),
        out_shape=jax.ShapeDtypeStruct((t, d), F32),
        compiler_params=_cparams(("parallel", "parallel")), name="mm_glu")(x_bf, w_bf, w_bf, bias2, bias2)


def _mm_plain_body(x_ref, w_ref, o_ref):
    o_ref[...] = jnp.dot(x_ref[...], w_ref[...], preferred_element_type=F32).astype(o_ref.dtype)


def mm_plain(x_bf, w_bf, out_dtype=F32):
    t, k = x_bf.shape
    n = w_bf.shape[1]
    tm = _pick(t, _ROW_TILES)
    tn = min(n, 2048)
    return pl.pallas_call(
        _mm_plain_body, grid=(n // tn, t // tm),
        in_specs=[pl.BlockSpec((tm, k), lambda j, i: (i, 0)),
                  pl.BlockSpec((k, tn), lambda j, i: (0, j))],
        out_specs=pl.BlockSpec((tm, tn), lambda j, i: (i, j)),
        out_shape=jax.ShapeDtypeStruct((t, n), out_dtype),
        compiler_params=_cparams(("parallel", "parallel")), name="mm_plain")(x_bf, w_bf)


def _conv_rows(ext_ref, w_ref, y_ref, sh_ref, rows, rc, lc):
    d = y_ref.shape[1]
    shift = CONV_HALO - (CONV_WIDTH - 1)
    sub = 8
    n_sh = sh_ref.shape[1]

    def lane_body(c, carry):
        l0 = pl.multiple_of(c * lc, lc)
        lanes = pl.ds(l0, lc)
        wv = w_ref[:, lanes]
        for b in range(1, sub):
            sh_ref[b - 1] = ext_ref[pl.ds(b, n_sh), lanes]
        for r in range(rows // rc):
            acc = None
            for k in range(CONV_WIDTH):
                b = (shift + k) % sub
                a = r * rc + (shift + k) - b
                src = ext_ref[pl.ds(a, rc), lanes] if b == 0 else sh_ref[b - 1, pl.ds(a, rc), :]
                term = src * wv[k:k + 1, :]
                acc = term if acc is None else acc + term
            y_ref[pl.ds(r * rc, rc), lanes] = acc
        return carry

    lax.fori_loop(0, d // lc, lane_body, 0)


def _conv_epilogue(y_ref, dwb_ref, g_ref, b_ref, o_ref):
    rows = y_ref.shape[0]
    step = 16

    def body(c, carry):
        sl = pl.ds(pl.multiple_of(c * step, step), step)
        z = _layer_norm(y_ref[sl, :] + dwb_ref[...], g_ref[...], b_ref[...])
        o_ref[sl, :] = (z * _sigmoid(z)).astype(BF16)
        return carry

    lax.fori_loop(0, rows // step, body, 0, unroll=min(4, rows // step))


def _conv_main_body(tr, nst, n_real, cur_ref, prev_ref, meta_ref, w_ref, dwb_ref, g_ref, b_ref, vmeta_ref, o_ref,
                    ext_ref, y_ref, sh_ref):
    step = pl.program_id(0)
    nm = meta_ref.shape[0]

    @pl.when(step < n_real)
    def _():
        @pl.when(step % nst == 0)
        def _():
            ext_ref[0:CONV_HALO - nm, :] = jnp.zeros((CONV_HALO - nm, ext_ref.shape[1]), F32)
            ext_ref[CONV_HALO - nm:CONV_HALO, :] = meta_ref[...]

        @pl.when(step % nst > 0)
        def _():
            ext_ref[0:CONV_HALO, :] = prev_ref[...]

        ext_ref[CONV_HALO:CONV_HALO + tr, :] = cur_ref[...]
        _conv_rows(ext_ref, w_ref, y_ref, sh_ref, tr, 64, CONV_LANES)
        _conv_epilogue(y_ref, dwb_ref, g_ref, b_ref, o_ref)

    @pl.when(step == n_real)
    def _():
        _write_meta_rows(vmeta_ref, o_ref)


def _conv_meta_body(meta_ref, w_ref, dwb_ref, g_ref, b_ref, o_ref, ext_ref, y_ref, sh_ref):
    nm = meta_ref.shape[0]
    ext_ref[0:CONV_HALO, :] = jnp.zeros((CONV_HALO, ext_ref.shape[1]), F32)
    ext_ref[CONV_HALO:CONV_HALO + nm, :] = meta_ref[...]
    _conv_rows(ext_ref, w_ref, y_ref, sh_ref, nm, nm, CONV_LANES)
    _conv_epilogue(y_ref, dwb_ref, g_ref, b_ref, o_ref)


def conv_ln_swish(u, dw_w, dw_b, ln_g, ln_b, nb, s_len, nm):
    t, d = u.shape
    tr_rows = nb * s_len
    tr = _pick(s_len, (512, 256))
    w_pad = jnp.concatenate([dw_w, jnp.zeros((CONV_HALO - CONV_WIDTH, d), F32)], axis=0)
    vecs = [dw_b.reshape(1, d), ln_g.reshape(1, d), ln_b.reshape(1, d)]
    meta_blk = tr_rows // nm
    out_meta = pl.pallas_call(
        _conv_meta_body, grid=(nb,),
        in_specs=[pl.BlockSpec((nm, d), lambda b: (meta_blk + b, 0)),
                  pl.BlockSpec((CONV_HALO, d), lambda b: (0, 0))]
        + [pl.BlockSpec((1, d), lambda b: (0, 0))] * 3,
        out_specs=pl.BlockSpec((nm, d), lambda b: (b, 0)),
        out_shape=jax.ShapeDtypeStruct((nb * nm, d), BF16),
        scratch_shapes=[pltpu.VMEM((CONV_HALO + nm, d), F32), pltpu.VMEM((nm, d), F32),
                        pltpu.VMEM((7, CONV_HALO + nm - 8, CONV_LANES), F32)],
        compiler_params=_cparams(("parallel",)), name="conv_meta")(u, w_pad, *vecs)
    nst = s_len // tr
    n_real = nb * nst
    halo_per_tile = tr // CONV_HALO
    assert nb * nm <= tr

    def real(g):
        return jnp.minimum(g, n_real - 1)

    return pl.pallas_call(
        functools.partial(_conv_main_body, tr, nst, n_real), grid=(n_real + 1,),
        in_specs=[pl.BlockSpec((tr, d), lambda g: (real(g), 0)),
                  pl.BlockSpec((CONV_HALO, d), lambda g: (jnp.maximum(real(g) * halo_per_tile - 1, 0), 0)),
                  pl.BlockSpec((nm, d), lambda g: (meta_blk + real(g) // nst, 0)),
                  pl.BlockSpec((CONV_HALO, d), lambda g: (0, 0))]
        + [pl.BlockSpec((1, d), lambda g: (0, 0))] * 3
        + [pl.BlockSpec((nb * nm, d), lambda g: (0, 0))],
        out_specs=pl.BlockSpec((tr, d), lambda g: (g, 0)),
        out_shape=jax.ShapeDtypeStruct((t, d), BF16),
        scratch_shapes=[pltpu.VMEM((CONV_HALO + tr, d), F32), pltpu.VMEM((tr, d), F32),
                        pltpu.VMEM((7, CONV_HALO + tr - 8, CONV_LANES), F32)],
        compiler_params=_cparams(("arbitrary",)), name="conv_main")(u, u, u, w_pad, *vecs, out_meta)


def _mla_proj_body(ql, kvl, hd, scale, x_ref, cos_ref, sin_ref, w1_ref, qg_ref, kvg_ref, wq_ref, wkv_ref,
                   qn_ref, qr_ref, kn_ref, v_ref, kr_ref):
    a = jnp.dot(x_ref[...], w1_ref[...], preferred_element_type=F32)
    cos = cos_ref[...]
    sin = sin_ref[...]
    cq = _rms_norm(a[:, :ql], qg_ref[...]).astype(BF16)
    ckv = _rms_norm(a[:, ql:ql + kvl], kvg_ref[...]).astype(BF16)
    r0 = ql + kvl
    kr_ref[...] = (a[:, r0:r0 + LANES] * cos + a[:, r0 + LANES:r0 + 2 * LANES] * sin).astype(BF16)
    qa = jnp.dot(cq, wq_ref[...], preferred_element_type=F32)
    nh = hd // LANES
    cos_t = jnp.tile(cos, (1, nh))
    sin_t = jnp.tile(sin, (1, nh))
    qn_ref[...] = (qa[:, :hd] * scale).astype(BF16)
    qr_ref[...] = ((qa[:, hd:2 * hd] * cos_t + qa[:, 2 * hd:] * sin_t) * scale).astype(BF16)
    kv = jnp.dot(ckv, wkv_ref[...], preferred_element_type=F32)
    kn_ref[...] = kv[:, :hd].astype(BF16)
    v_ref[...] = kv[:, hd:].astype(BF16)


def _attn_body(tq, nb, qn_ref, qr_ref, kn_ref, kr_ref, v_ref, knm_ref, krm_ref, vm_ref, ometa_ref, o_ref, kf_ref):
    @pl.when(pl.program_id(0) == nb)
    def _():
        _write_meta_rows(ometa_ref, o_ref)

    @pl.when(pl.program_id(0) < nb)
    def _():
        _attn_tiles(tq, qn_ref, qr_ref, kn_ref, kr_ref, v_ref, knm_ref, krm_ref, vm_ref, o_ref, kf_ref)


def _attn_tiles(tq, qn_ref, qr_ref, kn_ref, kr_ref, v_ref, knm_ref, krm_ref, vm_ref, o_ref, kf_ref):
    s_len = qn_ref.shape[0]
    nt = (((1,), (1,)), ((), ()))
    kf_ref[:, :LANES] = kn_ref[...]
    kf_ref[:, LANES:] = kr_ref[...]
    km = jnp.concatenate([knm_ref[...], krm_ref[...]], axis=-1)
    vm = vm_ref[...]
    row_c = lax.broadcasted_iota(I32, (tq, tq), 0) // CHUNK
    col_c = lax.broadcasted_iota(I32, (tq, tq), 1) // CHUNK
    visible = col_c <= row_c
    for i in range(s_len // tq):
        r0 = i * tq
        q = jnp.concatenate([qn_ref[r0:r0 + tq, :], qr_ref[r0:r0 + tq, :]], axis=-1)
        s_m = lax.dot_general(q, km, nt, preferred_element_type=F32)
        s_d = lax.dot_general(q, kf_ref[r0:r0 + tq, :], nt, preferred_element_type=F32)
        s_d = jnp.where(visible, s_d, -jnp.inf)
        m = jnp.maximum(jnp.max(s_m, axis=-1, keepdims=True), jnp.max(s_d, axis=-1, keepdims=True))
        if i > 0:
            s_p = lax.dot_general(q, kf_ref[0:r0, :], nt, preferred_element_type=F32)
            m = jnp.maximum(m, jnp.max(s_p, axis=-1, keepdims=True))
        p_m = jnp.exp(s_m - m)
        p_d = jnp.exp(s_d - m)
        l = jnp.sum(p_m, axis=-1, keepdims=True) + jnp.sum(p_d, axis=-1, keepdims=True)
        acc = jnp.dot(p_m.astype(BF16), vm, preferred_element_type=F32)
        acc = acc + jnp.dot(p_d.astype(BF16), v_ref[r0:r0 + tq, :], preferred_element_type=F32)
        if i > 0:
            p_p = jnp.exp(s_p - m)
            l = l + jnp.sum(p_p, axis=-1, keepdims=True)
            acc = acc + jnp.dot(p_p.astype(BF16), v_ref[0:r0, :], preferred_element_type=F32)
        o_ref[r0:r0 + tq, :] = (acc / l).astype(BF16)


def _attn_meta_body(qn_ref, qr_ref, kn_ref, kr_ref, v_ref, o_ref):
    nt = (((1,), (1,)), ((), ()))
    kr = kr_ref[...]
    for h in range(o_ref.shape[1] // LANES):
        sl = slice(h * LANES, (h + 1) * LANES)
        q = jnp.concatenate([qn_ref[:, sl], qr_ref[:, sl]], axis=-1)
        k = jnp.concatenate([kn_ref[:, sl], kr], axis=-1)
        s = lax.dot_general(q, k, nt, preferred_element_type=F32)
        p = jnp.exp(s - jnp.max(s, axis=-1, keepdims=True))
        l = jnp.sum(p, axis=-1, keepdims=True)
        o_ref[:, sl] = (jnp.dot(p.astype(BF16), v_ref[:, sl], preferred_element_type=F32) / l).astype(BF16)


def _rope_rows(n_pos):
    inv = ROPE_THETA ** (-jnp.arange(0, MLA_ROPE, 2, dtype=F32) / MLA_ROPE)
    ang = jnp.arange(n_pos, dtype=F32)[:, None] * inv[None, :]
    pad = jnp.zeros((n_pos, LANES - MLA_ROPE), F32)
    cos = jnp.concatenate([jnp.cos(ang), jnp.cos(ang), pad], axis=1)
    sin = jnp.concatenate([jnp.sin(ang), jnp.sin(ang), pad], axis=1)
    return cos, sin


def _pad_rope_cols(w):
    half = MLA_ROPE // 2
    z = jnp.zeros((w.shape[0], LANES - MLA_ROPE), w.dtype)
    rot = jnp.concatenate([-w[:, half:], w[:, :half]], axis=1)
    return jnp.concatenate([w, z], axis=1), jnp.concatenate([rot, z], axis=1)


def mla_mixer(hb, wdq, q_norm_g, wuq, wdkv, kv_norm_g, wukv, nb, s_len, nm):
    t, d = hb.shape
    ql = wdq.shape[1]
    kvl = kv_norm_g.shape[0]
    nh = wuq.shape[1] // (HEAD_DIM + MLA_ROPE)
    hd = nh * HEAD_DIM
    tr_rows = nb * s_len
    scale = float((HEAD_DIM + MLA_ROPE) ** -0.5)
    kr_w, kr_rot = _pad_rope_cols(wdkv[:, kvl:])
    w1 = jnp.concatenate([wdq, wdkv[:, :kvl], kr_w, kr_rot], axis=1).astype(BF16)
    wuq3 = wuq.reshape(ql, nh, HEAD_DIM + MLA_ROPE)
    q_rope = wuq3[:, :, HEAD_DIM:]
    half = MLA_ROPE // 2
    zq = jnp.zeros((ql, nh, LANES - MLA_ROPE), F32)
    q_rope_p = jnp.concatenate([q_rope, zq], axis=2).reshape(ql, hd)
    q_rot_p = jnp.concatenate([-q_rope[:, :, half:], q_rope[:, :, :half], zq], axis=2).reshape(ql, hd)
    wq = jnp.concatenate([wuq3[:, :, :HEAD_DIM].reshape(ql, hd), q_rope_p, q_rot_p], axis=1).astype(BF16)
    wukv3 = wukv.reshape(kvl, nh, 2 * HEAD_DIM)
    wkv = jnp.concatenate([wukv3[:, :, :HEAD_DIM].reshape(kvl, hd),
                           wukv3[:, :, HEAD_DIM:].reshape(kvl, hd)], axis=1).astype(BF16)
    cos_p, sin_p = _rope_rows(nm + s_len)
    cos_rows = jnp.concatenate([jnp.tile(cos_p[nm:], (nb, 1)), jnp.tile(cos_p[:nm], (nb, 1))], axis=0)
    sin_rows = jnp.concatenate([jnp.tile(sin_p[nm:], (nb, 1)), jnp.tile(sin_p[:nm], (nb, 1))], axis=0)

    tm = _pick(t, (256, 128, 96, 64, 48, 32, 16))
    qn, qr, kn, v, kr = _rows_call(
        functools.partial(_mla_proj_body, ql, kvl, hd, scale), t, tm, [hb, cos_rows, sin_rows],
        [w1, q_norm_g.reshape(1, ql), kv_norm_g.reshape(1, kvl), wq, wkv],
        [(hd, BF16), (hd, BF16), (hd, BF16), (hd, BF16), (LANES, BF16)], "mla_proj")

    meta_blk = tr_rows // nm
    mrow = lambda n: pl.BlockSpec((nm, n), lambda b: (meta_blk + b, 0))
    o_meta = pl.pallas_call(
        _attn_meta_body, grid=(nb,),
        in_specs=[mrow(hd), mrow(hd), mrow(hd), mrow(LANES), mrow(hd)],
        out_specs=pl.BlockSpec((nm, hd), lambda b: (b, 0)),
        out_shape=jax.ShapeDtypeStruct((nb * nm, hd), BF16),
        compiler_params=_cparams(("parallel",)), name="attn_meta")(qn, qr, kn, kr, v)

    tq = 256
    assert nb * nm <= s_len

    def real(b):
        return jnp.minimum(b, nb - 1)

    kspec = lambda: pl.BlockSpec((s_len, LANES), lambda b, h: (real(b), h))
    kspec0 = lambda: pl.BlockSpec((s_len, LANES), lambda b, h: (real(b), 0))
    m2 = lambda: pl.BlockSpec((nm, LANES), lambda b, h: (meta_blk + real(b), h))
    m20 = lambda: pl.BlockSpec((nm, LANES), lambda b, h: (meta_blk + real(b), 0))
    return pl.pallas_call(
        functools.partial(_attn_body, tq, nb), grid=(nb + 1, nh),
        in_specs=[kspec(), kspec(), kspec(), kspec0(), kspec(), m2(), m20(), m2(),
                  pl.BlockSpec((nb * nm, LANES), lambda b, h: (0, h))],
        out_specs=pl.BlockSpec((s_len, LANES), lambda b, h: (b, h)),
        out_shape=jax.ShapeDtypeStruct((t, hd), BF16),
        scratch_shapes=[pltpu.VMEM((s_len, 2 * LANES), BF16)],
        compiler_params=_cparams(("arbitrary", "arbitrary")), name="attn_main")(
            qn, qr, kn, kr, v, kn, kr, v, o_meta)


def _hgrn_gates(fz, lb):
    return lb + (1.0 - lb) * _sigmoid(fz), (1.0 - lb) * _sigmoid(-fz)


def _hgrn_level_index(c):
    t = lax.broadcasted_iota(I32, (c, c), 0)
    s = lax.broadcasted_iota(I32, (c, c), 1)
    lvl = 31 - lax.clz(t ^ s)
    return jnp.where(s > t, -2, lvl)


def _hgrn_block(q, f, k, iv, st, lvl):
    c = q.shape[0]
    nt = (((1,), (1,)), ((), ()))
    tn = (((0,), (0,)), ((), ()))
    row = lax.broadcasted_iota(I32, (c, HEAD_DIM), 0)
    scores = jnp.where(lvl == -1, lax.dot_general(q.astype(BF16), k.astype(BF16), nt,
                                                  preferred_element_type=F32), 0.0)
    qa = q * f
    kb = k
    tot = f
    h, idx = 1, 0
    while h < c:
        prod = lax.dot_general(qa.astype(BF16), kb.astype(BF16), nt, preferred_element_type=F32)
        scores = jnp.where(lvl == idx, prod, scores)
        right = (row & h) != 0
        left_tot = pltpu.roll(tot, h, 0)
        right_tot = pltpu.roll(tot, c - h, 0)
        qa = qa * jnp.where(right, left_tot, 1.0)
        kb = kb * jnp.where(right, 1.0, right_tot)
        tot = tot * jnp.where(right, left_tot, right_tot)
        h, idx = 2 * h, idx + 1
    ib = iv.astype(BF16)
    o = jnp.dot(scores.astype(BF16), ib, preferred_element_type=F32)
    o = o + lax.dot_general(qa.astype(BF16), st.astype(BF16), nt, preferred_element_type=F32)
    st_new = st * tot[0:1, :] + lax.dot_general(ib, kb.astype(BF16), tn, preferred_element_type=F32)
    return o, st_new


def _hgrn_out(o, gate, ng):
    o = o * lax.rsqrt(jnp.mean(o * o, axis=-1, keepdims=True) + RMS_EPS) * ng
    return (o * (gate * _sigmoid(gate))).astype(BF16)


def _hgrn_meta_body(gh, q_ref, fz_ref, i_ref, g_ref, lb_ref, ng_ref, o_ref, st_ref):
    nm = q_ref.shape[0]
    c = HGRN_CHUNK
    lvl = _hgrn_level_index(c)
    zeros = jnp.zeros((c - nm, HEAD_DIM), F32)
    for g in range(gh):
        sl = slice(g * HEAD_DIM, (g + 1) * HEAD_DIM)
        f, k = _hgrn_gates(fz_ref[:, sl], lb_ref[:, sl])
        o, st = _hgrn_block(jnp.concatenate([zeros, q_ref[:, sl]], axis=0),
                            jnp.concatenate([zeros + 1.0, f], axis=0),
                            jnp.concatenate([zeros, k], axis=0),
                            jnp.concatenate([zeros, i_ref[:, sl]], axis=0),
                            jnp.zeros((HEAD_DIM, HEAD_DIM), F32), lvl)
        st_ref[0, g] = st
        o_ref[:, sl] = _hgrn_out(o[c - nm:], g_ref[:, sl], ng_ref[...])


def _hgrn_main_body(gh, ts, nst, n_real, q_ref, fz_ref, i_ref, g_ref, lb_ref, ng_ref, st0_ref, ometa_ref, o_ref,
                    st_ref):
    c = HGRN_CHUNK
    step = pl.program_id(1)

    @pl.when(step == n_real)
    def _():
        _write_meta_rows(ometa_ref, o_ref)

    @pl.when(step < n_real)
    def _():
        lvl = _hgrn_level_index(c)

        @pl.when(step % nst == 0)
        def _():
            st_ref[...] = st0_ref[0]

        def chunk_body(cidx, carry):
            rows = pl.ds(pl.multiple_of(cidx * c, c), c)
            for g in range(gh):
                sl = slice(g * HEAD_DIM, (g + 1) * HEAD_DIM)
                f, k = _hgrn_gates(fz_ref[rows, sl], lb_ref[:, sl])
                o, st = _hgrn_block(q_ref[rows, sl], f, k, i_ref[rows, sl], st_ref[g], lvl)
                st_ref[g] = st
                o_ref[rows, sl] = _hgrn_out(o, g_ref[rows, sl], ng_ref[...])
            return carry

        lax.fori_loop(0, ts // c, chunk_body, 0)


def hgrn_mixer(qfig, lb, norm_g, nb, s_len, nm):
    t, d4 = qfig.shape
    d = d4 // 4
    nh = d // HEAD_DIM
    gh = 4 if nh % 4 == 0 else 1
    gw = gh * HEAD_DIM
    ng_blocks = d // gw
    tr_rows = nb * s_len
    meta_blk = tr_rows // nm
    lb2 = lb.reshape(1, d)
    ng2 = norm_g.reshape(1, HEAD_DIM)

    def mspec(sec):
        return pl.BlockSpec((nm, gw), lambda b, h: (meta_blk + b, sec * ng_blocks + h))

    o_meta, st0 = pl.pallas_call(
        functools.partial(_hgrn_meta_body, gh), grid=(nb, ng_blocks),
        in_specs=[mspec(0), mspec(1), mspec(2), mspec(3),
                  pl.BlockSpec((1, gw), lambda b, h: (0, h)),
                  pl.BlockSpec((1, HEAD_DIM), lambda b, h: (0, 0))],
        out_specs=[pl.BlockSpec((nm, gw), lambda b, h: (b, h)),
                   pl.BlockSpec((1, gh, HEAD_DIM, HEAD_DIM), lambda b, h: (b, h, 0, 0))],
        out_shape=[jax.ShapeDtypeStruct((nb * nm, d), BF16),
                   jax.ShapeDtypeStruct((nb, nh, HEAD_DIM, HEAD_DIM), F32)],
        compiler_params=_cparams(("parallel", "parallel")), name="hgrn_meta")(qfig, qfig, qfig, qfig, lb2, ng2)

    ts = _pick(s_len, (512, 256))
    nst = s_len // ts
    n_real = nb * nst
    assert nb * nm <= ts

    def real(g):
        return jnp.minimum(g, n_real - 1)

    def rspec(sec):
        return pl.BlockSpec((ts, gw), lambda h, g: (real(g), sec * ng_blocks + h))

    return pl.pallas_call(
        functools.partial(_hgrn_main_body, gh, ts, nst, n_real), grid=(ng_blocks, n_real + 1),
        in_specs=[rspec(0), rspec(1), rspec(2), rspec(3),
                  pl.BlockSpec((1, gw), lambda h, g: (0, h)),
                  pl.BlockSpec((1, HEAD_DIM), lambda h, g: (0, 0)),
                  pl.BlockSpec((1, gh, HEAD_DIM, HEAD_DIM), lambda h, g: (real(g) // nst, h, 0, 0)),
                  pl.BlockSpec((nb * nm, gw), lambda h, g: (0, h))],
        out_specs=pl.BlockSpec((ts, gw), lambda h, g: (g, h)),
        out_shape=jax.ShapeDtypeStruct((t, d), BF16),
        scratch_shapes=[pltpu.VMEM((gh, HEAD_DIM, HEAD_DIM), F32)],
        compiler_params=_cparams(("parallel", "arbitrary")), name="hgrn_main")(
            qfig, qfig, qfig, qfig, lb2, ng2, st0, o_meta)


def _first_index_of_max(vals, idx, n, axis):
    mx = jnp.max(vals, axis=axis, keepdims=True)
    first = jnp.min(jnp.where(vals == mx, idx, n), axis=axis, keepdims=True)
    return mx, first


def _router_body(tm, h_ref, hb_ref, rw_ref, rb_ref, tri_ref, e_ref, gate_ref, rank_ref, cnt_ref, base_ref):
    @pl.when(pl.program_id(0) == 0)
    def _():
        base_ref[...] = jnp.zeros(base_ref.shape, F32)

    nt = (((1,), (1,)), ((), ()))
    hb = hb_ref[...]
    resid = (h_ref[...] - hb.astype(F32)).astype(BF16)
    main = lax.dot_general(rw_ref[...], hb, nt, preferred_element_type=F32)
    logits = (main[:N_EXPERTS] + main[N_EXPERTS:2 * N_EXPERTS] + main[2 * N_EXPERTS:]
              + lax.dot_general(rw_ref[:N_EXPERTS, :], resid, nt, preferred_element_type=F32))
    scores = _sigmoid(logits)
    sel = scores + rb_ref[...]
    g, epg = N_GROUPS, EXPERTS_PER_GROUP
    sel3 = sel.reshape(g, epg, tm)
    sc3 = scores.reshape(g, epg, tm)
    idx3 = lax.broadcasted_iota(I32, (g, epg, tm), 1)
    m1, i1 = _first_index_of_max(sel3, idx3, epg, 1)
    rest = jnp.where(idx3 == i1, -jnp.inf, sel3)
    m2, i2 = _first_index_of_max(rest, idx3, epg, 1)
    gidx = lax.broadcasted_iota(I32, (g, 1, tm), 0)
    _, gtop3 = _first_index_of_max(m1 + m2, gidx, g, 0)
    pick = gidx == gtop3
    gtop = gtop3[0]
    l1 = jnp.sum(jnp.where(pick, i1, 0), axis=0)
    l2 = jnp.sum(jnp.where(pick, i2, 0), axis=0)
    sc_in = jnp.sum(jnp.where(pick, sc3, 0.0), axis=0)
    idx2 = lax.broadcasted_iota(I32, (epg, tm), 0)
    s1 = jnp.sum(jnp.where(idx2 == l1, sc_in, 0.0), axis=0, keepdims=True)
    s2 = jnp.sum(jnp.where(idx2 == l2, sc_in, 0.0), axis=0, keepdims=True)
    e1 = gtop * epg + l1
    e2 = gtop * epg + l2
    e_ref[0:1, :] = e1
    e_ref[1:2, :] = e2
    den = s1 + s2
    gate_ref[0:1, :] = s1 / den
    gate_ref[1:2, :] = s2 / den
    eidx = lax.broadcasted_iota(I32, (N_EXPERTS, tm), 0)
    oh1 = (eidx == e1).astype(F32)
    oh2 = (eidx == e2).astype(F32)
    oh = jnp.concatenate([oh1, oh2], axis=0).astype(BF16)
    pre = jnp.dot(oh, tri_ref[...], preferred_element_type=F32)
    base = base_ref[:, 0:1]
    tot1 = jnp.sum(oh1, axis=1, keepdims=True)
    tot2 = jnp.sum(oh2, axis=1, keepdims=True)
    r1 = jnp.sum(oh1 * (base + pre[:N_EXPERTS]), axis=0, keepdims=True)
    r2 = jnp.sum(oh2 * (base + tot1 + pre[N_EXPERTS:]), axis=0, keepdims=True)
    rank_ref[0:1, :] = r1.astype(I32)
    rank_ref[1:2, :] = r2.astype(I32)
    new_base = jnp.broadcast_to(base + tot1 + tot2, base_ref.shape)
    base_ref[...] = new_base
    cnt_ref[...] = new_base


def moe_route(h, hb, router_w, router_b):
    t, d = h.shape
    tm = _pick(t, (256, 128, 96, 64, 32))
    tri = (lax.broadcasted_iota(I32, (tm, tm), 0) < lax.broadcasted_iota(I32, (tm, tm), 1)).astype(BF16)
    rw_t = router_w.T.astype(F32)
    rw1 = rw_t.astype(BF16)
    rw2 = (rw_t - rw1.astype(F32)).astype(BF16)
    rw3 = (rw_t - rw1.astype(F32) - rw2.astype(F32)).astype(BF16)
    rw_pieces = jnp.concatenate([rw1, rw2, rw3], axis=0)
    e, gate, rank, cnt = pl.pallas_call(
        functools.partial(_router_body, tm), grid=(t // tm,),
        in_specs=[pl.BlockSpec((tm, d), lambda i: (i, 0)),
                  pl.BlockSpec((tm, d), lambda i: (i, 0)),
                  pl.BlockSpec((3 * N_EXPERTS, d), lambda i: (0, 0)),
                  pl.BlockSpec((N_EXPERTS, 1), lambda i: (0, 0)),
                  pl.BlockSpec((tm, tm), lambda i: (0, 0))],
        out_specs=[pl.BlockSpec((2, tm), lambda i: (0, i)),
                   pl.BlockSpec((2, tm), lambda i: (0, i)),
                   pl.BlockSpec((2, tm), lambda i: (0, i)),
                   pl.BlockSpec((N_EXPERTS, LANES), lambda i: (0, 0))],
        out_shape=[jax.ShapeDtypeStruct((2, t), I32), jax.ShapeDtypeStruct((2, t), F32),
                   jax.ShapeDtypeStruct((2, t), I32), jax.ShapeDtypeStruct((N_EXPERTS, LANES), F32)],
        scratch_shapes=[pltpu.VMEM((N_EXPERTS, LANES), F32)],
        compiler_params=_cparams(("arbitrary",)), name="moe_route")(
            h, hb, rw_pieces, router_b.reshape(N_EXPERTS, 1), tri)
    return e, gate, rank, cnt[:, 0].astype(I32)


def _dispatch_body(tt, zero_flag_ref, dest_ref, hb_ref, xb_hbm, stage_ref, zero_ref, sem):
    half = stage_ref.shape[1]

    @pl.when(pl.program_id(0) == 0)
    def _():
        zero_ref[...] = jnp.zeros(zero_ref.shape, U32)

        def zfill(b, carry):
            @pl.when(zero_flag_ref[b] > 0)
            def _():
                start = pl.multiple_of(b * MOE_BLOCK, MOE_BLOCK)
                cp = pltpu.make_async_copy(zero_ref, xb_hbm.at[pl.ds(start, MOE_BLOCK), :], sem)
                cp.start()
                cp.wait()
            return carry

        lax.fori_loop(0, zero_flag_ref.shape[0], zfill, 0)

    stage_ref[...] = _pack_bf16_pair(hb_ref[:, :half].astype(F32), hb_ref[:, half:].astype(F32))

    def issue(j, carry):
        for k in range(2):
            pltpu.make_async_copy(stage_ref.at[pl.ds(j, 1), :],
                                  xb_hbm.at[pl.ds(dest_ref[k, j], 1), :], sem).start(priority=k)
        return carry

    lax.fori_loop(0, tt, issue, 0, unroll=8)
    for k in range(2):
        pltpu.make_async_copy(stage_ref, xb_hbm.at[pl.ds(0, tt), :], sem).wait()


def moe_dispatch(hb, dest, zero_flag, n_slots):
    t, d = hb.shape
    tt = _pick(t, (256, 128, 96, 64, 32))
    return pl.pallas_call(
        functools.partial(_dispatch_body, tt),
        grid_spec=pltpu.PrefetchScalarGridSpec(
            num_scalar_prefetch=1, grid=(t // tt,),
            in_specs=[pl.BlockSpec((2, tt), lambda i, zf: (0, i), memory_space=pltpu.SMEM),
                      pl.BlockSpec((tt, d), lambda i, zf: (i, 0))],
            out_specs=pl.BlockSpec(memory_space=pl.ANY),
            scratch_shapes=[pltpu.VMEM((tt, d // 2), U32), pltpu.VMEM((MOE_BLOCK, d // 2), U32),
                            pltpu.SemaphoreType.DMA(())]),
        out_shape=jax.ShapeDtypeStruct((n_slots, d // 2), U32),
        compiler_params=_cparams(("arbitrary",)), name="moe_dispatch")(zero_flag, dest, hb)


def _expert_body(be_ref, nu_ref, x_ref, wg_ref, wu_ref, wd_ref, y_ref, wgb_ref, wub_ref, wdb_ref):
    i = pl.program_id(0)
    used = i < nu_ref[0]
    half = x_ref.shape[1]

    @pl.when(jnp.logical_or(i == 0, be_ref[i] != be_ref[jnp.maximum(i - 1, 0)]))
    def _():
        wgb_ref[...] = wg_ref[0, 0].astype(BF16)
        wub_ref[...] = wu_ref[0, 0].astype(BF16)
        wdb_ref[...] = wd_ref[0, 0].astype(BF16)

    @pl.when(used)
    def _():
        lo, hi = _unpack_bf16_pair(x_ref[...])
        lo = lo.astype(BF16)
        hi = hi.astype(BF16)
        g = (jnp.dot(lo, wgb_ref[:half, :], preferred_element_type=F32)
             + jnp.dot(hi, wgb_ref[half:, :], preferred_element_type=F32))
        u = (jnp.dot(lo, wub_ref[:half, :], preferred_element_type=F32)
             + jnp.dot(hi, wub_ref[half:, :], preferred_element_type=F32))
        a = (g * _sigmoid(g) * u).astype(BF16)
        y = jnp.dot(a, wdb_ref[...], preferred_element_type=F32)
        y_ref[...] = _pack_bf16_pair(y[:, :half], y[:, half:])

    @pl.when(jnp.logical_not(used))
    def _():
        y_ref[...] = jnp.zeros(y_ref.shape, U32)


def moe_experts(xb, blk_expert, n_used, wg, wu, wd, layer):
    n_slots, half = xb.shape
    d = 2 * half
    ff = wg.shape[3]
    nblk = n_slots // MOE_BLOCK
    return pl.pallas_call(
        _expert_body,
        grid_spec=pltpu.PrefetchScalarGridSpec(
            num_scalar_prefetch=2, grid=(nblk,),
            in_specs=[pl.BlockSpec((MOE_BLOCK, half), lambda i, be, nu: (i, 0)),
                      pl.BlockSpec((1, 1, d, ff), lambda i, be, nu: (layer, be[i], 0, 0)),
                      pl.BlockSpec((1, 1, d, ff), lambda i, be, nu: (layer, be[i], 0, 0)),
                      pl.BlockSpec((1, 1, ff, d), lambda i, be, nu: (layer, be[i], 0, 0))],
            out_specs=pl.BlockSpec((MOE_BLOCK, half), lambda i, be, nu: (i, 0)),
            scratch_shapes=[pltpu.VMEM((d, ff), BF16), pltpu.VMEM((d, ff), BF16), pltpu.VMEM((ff, d), BF16)]),
        out_shape=jax.ShapeDtypeStruct((n_slots, half), U32),
        compiler_params=_cparams(("arbitrary",)), name="moe_experts")(
            blk_expert, n_used, xb, wg, wu, wd)


def _combine_body(tt, alpha, n_tiles, dest_ref, dnext_ref, gate_ref, h_ref, g_ref, b_ref, y_hbm, ho_ref, hb_ref,
                  buf_ref, sem):
    i = pl.program_id(0)
    slot = i % 2
    nslot = 1 - slot

    def gather(idx_ref, j, to_slot):
        for k in range(2):
            pltpu.make_async_copy(y_hbm.at[pl.ds(idx_ref[k, j], 1), :],
                                  buf_ref.at[to_slot, k, pl.ds(j, 1), :], sem.at[to_slot]).start(priority=k)

    def wait_slot(s):
        for k in range(2):
            pltpu.make_async_copy(y_hbm.at[pl.ds(0, tt), :], buf_ref.at[s, k], sem.at[s]).wait()

    @pl.when(i == 0)
    def _():
        def first(j, carry):
            gather(dest_ref, j, 0)
            return carry
        lax.fori_loop(0, tt, first, 0, unroll=8)

    wait_slot(slot)

    def ahead(j, carry):
        gather(dnext_ref, j, nslot)
        return carry

    lax.fori_loop(0, tt, ahead, 0, unroll=8)

    def rows_body(c, carry):
        r0 = pl.multiple_of(c * COMBINE_ROWS, COMBINE_ROWS)
        rows = pl.ds(r0, COMBINE_ROWS)
        gate = gate_ref[rows, :]
        lo0, hi0 = _unpack_bf16_pair(buf_ref[slot, 0, rows, :])
        lo1, hi1 = _unpack_bf16_pair(buf_ref[slot, 1, rows, :])
        ffn = jnp.concatenate([gate[:, 0:1] * lo0 + gate[:, 1:2] * lo1,
                               gate[:, 0:1] * hi0 + gate[:, 1:2] * hi1], axis=-1)
        o = _layer_norm(alpha * h_ref[rows, :] + ffn, g_ref[...], b_ref[...])
        ho_ref[rows, :] = o
        hb_ref[rows, :] = o.astype(BF16)
        return carry

    lax.fori_loop(0, tt // COMBINE_ROWS, rows_body, 0, unroll=4)

    @pl.when(i == n_tiles - 1)
    def _():
        wait_slot(nslot)


def moe_combine(yb, dest, gate_t, h, ln_g, ln_b, alpha, n_rows):
    t, d = h.shape
    tt = next(c for c in (256, 128, 96, 64, 32) if t % c == 0 and n_rows % c == 0)
    n_tiles = n_rows // tt
    return pl.pallas_call(
        functools.partial(_combine_body, tt, alpha, n_tiles), grid=(n_tiles,),
        in_specs=[pl.BlockSpec((2, tt), lambda i: (0, i), memory_space=pltpu.SMEM),
                  pl.BlockSpec((2, tt), lambda i: (0, jnp.minimum(i + 1, n_tiles - 1)), memory_space=pltpu.SMEM),
                  pl.BlockSpec((tt, 2), lambda i: (i, 0)),
                  pl.BlockSpec((tt, d), lambda i: (i, 0)),
                  pl.BlockSpec((1, d), lambda i: (0, 0)),
                  pl.BlockSpec((1, d), lambda i: (0, 0)),
                  pl.BlockSpec(memory_space=pl.ANY)],
        out_specs=[pl.BlockSpec((tt, d), lambda i: (i, 0)), pl.BlockSpec((tt, d), lambda i: (i, 0))],
        out_shape=[jax.ShapeDtypeStruct((n_rows, d), F32), jax.ShapeDtypeStruct((n_rows, d), BF16)],
        scratch_shapes=[pltpu.VMEM((2, 2, tt, d // 2), U32), pltpu.SemaphoreType.DMA((2,))],
        compiler_params=_cparams(("arbitrary",)), name="moe_combine")(
            dest, dest, gate_t, h, ln_g.reshape(1, d), ln_b.reshape(1, d), yb)


def moe_layer(h, hb, router_w, router_b, w_gate, w_up, w_down, layer, ln_g, ln_b, alpha, n_rows_out):
    t, d = h.shape
    e_idx, gate, rank, counts = moe_route(h, hb, router_w, router_b)
    padded = (counts + MOE_BLOCK - 1) // MOE_BLOCK * MOE_BLOCK
    pends = jnp.cumsum(padded)
    pstart = pends - padded
    experts = jnp.arange(N_EXPERTS, dtype=I32)[:, None, None]
    dest = jnp.sum(jnp.where(e_idx[None] == experts, pstart[:, None, None], 0), axis=0) + rank
    nblk = -(-(2 * t) // MOE_BLOCK) + N_EXPERTS
    n_used = (pends[-1] // MOE_BLOCK).astype(I32)
    blk = jnp.arange(nblk, dtype=I32)
    blk_first_row = jnp.minimum(blk, n_used - 1) * MOE_BLOCK
    blk_expert = jnp.minimum(jnp.sum((pends[None, :] <= blk_first_row[:, None]).astype(I32), axis=1),
                             N_EXPERTS - 1)
    zero_flag = ((blk >= n_used) | (blk == pends[blk_expert] // MOE_BLOCK - 1)).astype(I32)
    xb = moe_dispatch(hb, dest, zero_flag, nblk * MOE_BLOCK)
    yb = moe_experts(xb, blk_expert, n_used.reshape(1), w_gate, w_up, w_down, layer)
    return moe_combine(yb, dest, gate.T, h, ln_g, ln_b, alpha, n_rows_out)


def kernel(x, meta_tokens, ln_mix_g, ln_mix_b, ln_ffn_g, ln_ffn_b, conv_pw1_w, conv_pw1_b, conv_dw_w, conv_dw_b, conv_ln_g, conv_ln_b, conv_pw2_w, conv_pw2_b, mla_wdq, mla_q_norm_g, mla_wuq, mla_wdkv, mla_kv_norm_g, mla_wukv, mla_wo, hgrn_w_in, hgrn_lb_logits, hgrn_norm_g, hgrn_wo, router_w, router_b, moe_w_gate, moe_w_up, moe_w_down):
    nb, s_len, d = x.shape
    nm = meta_tokens.shape[0]
    depth = ln_mix_g.shape[0]
    alpha = float((2 * depth) ** 0.25)
    zero_bias = jnp.zeros((d,), F32)

    meta = jnp.broadcast_to(meta_tokens[None].astype(x.dtype), (nb, nm, d)).reshape(nb * nm, d)
    h = x.reshape(nb * s_len, d)
    tail = meta
    hb = jnp.concatenate([h.astype(BF16), meta.astype(BF16)], axis=0)
    p_lb = jax.nn.softmax(hgrn_lb_logits.astype(F32), axis=0)
    lower_bounds = jnp.cumsum(p_lb, axis=0) - p_lb[0]

    for i in range(depth):
        j = i // N_MIXERS
        kind = i % N_MIXERS
        if kind == 0:
            u = mm_glu(hb, conv_pw1_w[j].astype(BF16), conv_pw1_b[j])
            mix_in = conv_ln_swish(u, conv_dw_w[j], conv_dw_b[j], conv_ln_g[j], conv_ln_b[j], nb, s_len, nm)
            w_out, b_out = conv_pw2_w[j], conv_pw2_b[j]
        elif kind == 1:
            mix_in = mla_mixer(hb, mla_wdq[j], mla_q_norm_g[j], mla_wuq[j], mla_wdkv[j], mla_kv_norm_g[j],
                               mla_wukv[j], nb, s_len, nm)
            w_out, b_out = mla_wo[j], zero_bias
        else:
            qfig = mm_plain(hb, hgrn_w_in[j].astype(BF16))
            mix_in = hgrn_mixer(qfig, lower_bounds[i], hgrn_norm_g[j], nb, s_len, nm)
            w_out, b_out = hgrn_wo[j], zero_bias
        h, hb = mm_res_ln(mix_in, h, w_out.astype(BF16), b_out, ln_mix_g[i], ln_mix_b[i], alpha, tail)
        tail = None
        n_rows_out = nb * s_len if i == depth - 1 else h.shape[0]
        h, hb = moe_layer(h, hb, router_w, router_b, moe_w_gate, moe_w_up, moe_w_down, i,
                          ln_ffn_g[i], ln_ffn_b[i], alpha, n_rows_out)
    return h.reshape(nb, s_len, d)
```

```python
import functools

import jax
import jax.numpy as jnp
from jax import lax
from jax.experimental import pallas as pl
from jax.experimental.pallas import tpu as pltpu

F32 = jnp.float32
BF16 = jnp.bfloat16
I32 = jnp.int32
U32 = jnp.uint32

LANES = 128
CHUNK = 64
N_MIXERS = 3
CONV_WIDTH = 31
CONV_HALO = 32
CONV_LANES = 256
HEAD_DIM = 128
MLA_ROPE = 64
ROPE_THETA = 10000.0
HGRN_CHUNK = 128
N_EXPERTS = 32
N_GROUPS = 4
EXPERTS_PER_GROUP = N_EXPERTS // N_GROUPS
MOE_BLOCK = 512
COMBINE_ROWS = 16
LN_EPS = 1e-5
RMS_EPS = 1e-6
VMEM_LIMIT = 52 * 1024 * 1024


def _pick(n, cands):
    for c in cands:
        if n % c == 0:
            return c
    raise ValueError(f"no tile for {n} in {cands}")


_ROW_TILES = (768, 512, 384, 256, 192, 176, 128, 96, 64, 48, 32, 16)


def _cparams(sem, vmem=VMEM_LIMIT):
    return pltpu.CompilerParams(dimension_semantics=sem, vmem_limit_bytes=vmem)


def _layer_norm(x, g, b):
    mu = jnp.mean(x, axis=-1, keepdims=True)
    xc = x - mu
    var = jnp.mean(xc * xc, axis=-1, keepdims=True)
    return xc * lax.rsqrt(var + LN_EPS) * g + b


def _rms_norm(x, g):
    return x * lax.rsqrt(jnp.mean(x * x, axis=-1, keepdims=True) + RMS_EPS) * g


def _sigmoid(x):
    return 1.0 / (1.0 + jnp.exp(-x))


def _pack_bf16_pair(lo, hi):
    lo_b = lax.bitcast_convert_type(lo.astype(BF16).astype(F32), U32)
    hi_b = lax.bitcast_convert_type(hi.astype(BF16).astype(F32), U32)
    return (hi_b & jnp.uint32(0xFFFF0000)) | (lo_b >> 16)


def _unpack_bf16_pair(w):
    lo = lax.bitcast_convert_type(w << 16, F32)
    hi = lax.bitcast_convert_type(w & jnp.uint32(0xFFFF0000), F32)
    return lo, hi


def _write_meta_rows(meta_ref, o_ref):
    o_ref[...] = jnp.zeros(o_ref.shape, o_ref.dtype)
    o_ref[0:meta_ref.shape[0], :] = meta_ref[...]


def _rows_call(body, n_rows, tm, row_ins, full_ins, outs, name):
    grid = (n_rows // tm,)
    in_specs = [pl.BlockSpec((tm, a.shape[1]), lambda i: (i, 0)) for a in row_ins]
    in_specs += [pl.BlockSpec(a.shape, lambda i, nd=a.ndim: (0,) * nd, pipeline_mode=pl.Buffered(1))
                 for a in full_ins]
    out_specs = [pl.BlockSpec((tm, n), lambda i: (i, 0)) for n, _ in outs]
    out_shape = [jax.ShapeDtypeStruct((n_rows, n), dt) for n, dt in outs]
    return pl.pallas_call(
        body, grid=grid, in_specs=in_specs, out_specs=out_specs, out_shape=out_shape,
        compiler_params=_cparams(("parallel",)), name=name)(*row_ins, *full_ins)


def _mm_res_ln_body(alpha, x_ref, h_ref, w_ref, bias_ref, g_ref, b_ref, ho_ref, hb_ref):
    acc = jnp.dot(x_ref[...], w_ref[...], preferred_element_type=F32)
    y = alpha * h_ref[...] + (acc + bias_ref[...])
    o = _layer_norm(y, g_ref[...], b_ref[...])
    ho_ref[...] = o
    hb_ref[...] = o.astype(BF16)


def _mm_res_ln_tail_body(alpha, n_head, x_ref, h_ref, w_ref, bias_ref, g_ref, b_ref, tail_ref, ho_ref, hb_ref):
    tm = x_ref.shape[0]
    row = pl.program_id(0) * tm + lax.broadcasted_iota(I32, (tm, 1), 0)
    res = jnp.where(row < n_head, h_ref[...], tail_ref[...])
    acc = jnp.dot(x_ref[...], w_ref[...], preferred_element_type=F32)
    o = _layer_norm(alpha * res + (acc + bias_ref[...]), g_ref[...], b_ref[...])
    ho_ref[...] = o
    hb_ref[...] = o.astype(BF16)


def mm_res_ln(x_bf, h, w_bf, bias, g, b, alpha, tail=None):
    t, d = x_bf.shape[0], h.shape[1]
    tm = _pick(t, (384, 256, 192, 176, 128, 96, 64, 48, 32, 16))
    vecs = [w_bf, bias.reshape(1, d), g.reshape(1, d), b.reshape(1, d)]
    outs = [(d, F32), (d, BF16)]
    if tail is None:
        return _rows_call(functools.partial(_mm_res_ln_body, alpha), t, tm, [x_bf, h], vecs, outs, "mm_res_ln")
    n_head = h.shape[0]
    assert t - tm < n_head < t and tail.shape[0] == t - n_head
    tail_tile = jnp.concatenate([jnp.zeros((tm - tail.shape[0], d), tail.dtype), tail], axis=0)
    return _rows_call(functools.partial(_mm_res_ln_tail_body, alpha, n_head), t, tm, [x_bf, h],
                      vecs + [tail_tile], outs, "mm_res_ln")


def _glu_body(x_ref, wa_ref, wg_ref, ba_ref, bg_ref, u_ref):
    x = x_ref[...]
    a = jnp.dot(x, wa_ref[...], preferred_element_type=F32) + ba_ref[...]
    g = jnp.dot(x, wg_ref[...], preferred_element_type=F32) + bg_ref[...]
    u_ref[...] = a * _sigmoid(g)


def mm_glu(x_bf, w_bf, bias):
    t, k = x_bf.shape
    d = w_bf.shape[1] // 2
    tm = _pick(t, _ROW_TILES)
    tn = min(d, 1024)
    nj = d // tn
    bias2 = bias.reshape(1, 2 * d)
    return pl.pallas_call(
        _glu_body, grid=(nj, t // tm),
        in_specs=[pl.BlockSpec((tm, k), lambda j, i: (i, 0)),
                  pl.BlockSpec((k, tn), lambda j, i: (0, j)),
                  pl.BlockSpec((k, tn), lambda j, i: (0, j + nj)),
                  pl.BlockSpec((1, tn), lambda j, i: (0, j)),
                  pl.BlockSpec((1, tn), lambda j, i: (0, j + nj))],
        out_specs=pl.BlockSpec((tm, tn), lambda j, i: (i, j)),
        out_shape=jax.ShapeDtypeStruct((t, d), F32),
        compiler_params=_cparams(("parallel", "parallel")), name="mm_glu")(x_bf, w_bf, w_bf, bias2, bias2)


def _mm_plain_body(x_ref, w_ref, o_ref):
    o_ref[...] = jnp.dot(x_ref[...], w_ref[...], preferred_element_type=F32).astype(o_ref.dtype)


def mm_plain(x_bf, w_bf, out_dtype=F32):
    t, k = x_bf.shape
    n = w_bf.shape[1]
    tm = _pick(t, _ROW_TILES)
    tn = min(n, 2048)
    return pl.pallas_call(
        _mm_plain_body, grid=(n // tn, t // tm),
        in_specs=[pl.BlockSpec((tm, k), lambda j, i: (i, 0)),
                  pl.BlockSpec((k, tn), lambda j, i: (0, j))],
        out_specs=pl.BlockSpec((tm, tn), lambda j, i: (i, j)),
        out_shape=jax.ShapeDtypeStruct((t, n), out_dtype),
        compiler_params=_cparams(("parallel", "parallel")), name="mm_plain")(x_bf, w_bf)


def _conv_rows(ext_ref, w_ref, y_ref, sh_ref, rows, rc, lc):
    d = y_ref.shape[1]
    shift = CONV_HALO - (CONV_WIDTH - 1)
    sub = 8
    n_sh = sh_ref.shape[1]

    def lane_body(c, carry):
        l0 = pl.multiple_of(c * lc, lc)
        lanes = pl.ds(l0, lc)
        wv = w_ref[:, lanes]
        for b in range(1, sub):
            sh_ref[b - 1] = ext_ref[pl.ds(b, n_sh), lanes]
        for r in range(rows // rc):
            acc = None
            for k in range(CONV_WIDTH):
                b = (shift + k) % sub
                a = r * rc + (shift + k) - b
                src = ext_ref[pl.ds(a, rc), lanes] if b == 0 else sh_ref[b - 1, pl.ds(a, rc), :]
                term = src * wv[k:k + 1, :]
                acc = term if acc is None else acc + term
            y_ref[pl.ds(r * rc, rc), lanes] = acc
        return carry

    lax.fori_loop(0, d // lc, lane_body, 0)


def _conv_epilogue(y_ref, dwb_ref, g_ref, b_ref, o_ref):
    rows = y_ref.shape[0]
    step = 16

    def body(c, carry):
        sl = pl.ds(pl.multiple_of(c * step, step), step)
        z = _layer_norm(y_ref[sl, :] + dwb_ref[...], g_ref[...], b_ref[...])
        o_ref[sl, :] = (z * _sigmoid(z)).astype(BF16)
        return carry

    lax.fori_loop(0, rows // step, body, 0, unroll=min(4, rows // step))


def _conv_main_body(tr, nst, n_real, cur_ref, prev_ref, meta_ref, w_ref, dwb_ref, g_ref, b_ref, vmeta_ref, o_ref,
                    ext_ref, y_ref, sh_ref):
    step = pl.program_id(0)
    nm = meta_ref.shape[0]

    @pl.when(step < n_real)
    def _():
        @pl.when(step % nst == 0)
        def _():
            ext_ref[0:CONV_HALO - nm, :] = jnp.zeros((CONV_HALO - nm, ext_ref.shape[1]), F32)
            ext_ref[CONV_HALO - nm:CONV_HALO, :] = meta_ref[...]

        @pl.when(step % nst > 0)
        def _():
            ext_ref[0:CONV_HALO, :] = prev_ref[...]

        ext_ref[CONV_HALO:CONV_HALO + tr, :] = cur_ref[...]
        _conv_rows(ext_ref, w_ref, y_ref, sh_ref, tr, 64, CONV_LANES)
        _conv_epilogue(y_ref, dwb_ref, g_ref, b_ref, o_ref)

    @pl.when(step == n_real)
    def _():
        _write_meta_rows(vmeta_ref, o_ref)


def _conv_meta_body(meta_ref, w_ref, dwb_ref, g_ref, b_ref, o_ref, ext_ref, y_ref, sh_ref):
    nm = meta_ref.shape[0]
    ext_ref[0:CONV_HALO, :] = jnp.zeros((CONV_HALO, ext_ref.shape[1]), F32)
    ext_ref[CONV_HALO:CONV_HALO + nm, :] = meta_ref[...]
    _conv_rows(ext_ref, w_ref, y_ref, sh_ref, nm, nm, CONV_LANES)
    _conv_epilogue(y_ref, dwb_ref, g_ref, b_ref, o_ref)


def conv_ln_swish(u, dw_w, dw_b, ln_g, ln_b, nb, s_len, nm):
    t, d = u.shape
    tr_rows = nb * s_len
    tr = _pick(s_len, (512, 256))
    w_pad = jnp.concatenate([dw_w, jnp.zeros((CONV_HALO - CONV_WIDTH, d), F32)], axis=0)
    vecs = [dw_b.reshape(1, d), ln_g.reshape(1, d), ln_b.reshape(1, d)]
    meta_blk = tr_rows // nm
    out_meta = pl.pallas_call(
        _conv_meta_body, grid=(nb,),
        in_specs=[pl.BlockSpec((nm, d), lambda b: (meta_blk + b, 0)),
                  pl.BlockSpec((CONV_HALO, d), lambda b: (0, 0))]
        + [pl.BlockSpec((1, d), lambda b: (0, 0))] * 3,
        out_specs=pl.BlockSpec((nm, d), lambda b: (b, 0)),
        out_shape=jax.ShapeDtypeStruct((nb * nm, d), BF16),
        scratch_shapes=[pltpu.VMEM((CONV_HALO + nm, d), F32), pltpu.VMEM((nm, d), F32),
                        pltpu.VMEM((7, CONV_HALO + nm - 8, CONV_LANES), F32)],
        compiler_params=_cparams(("parallel",)), name="conv_meta")(u, w_pad, *vecs)
    nst = s_len // tr
    n_real = nb * nst
    halo_per_tile = tr // CONV_HALO
    assert nb * nm <= tr

    def real(g):
        return jnp.minimum(g, n_real - 1)

    return pl.pallas_call(
        functools.partial(_conv_main_body, tr, nst, n_real), grid=(n_real + 1,),
        in_specs=[pl.BlockSpec((tr, d), lambda g: (real(g), 0)),
                  pl.BlockSpec((CONV_HALO, d), lambda g: (jnp.maximum(real(g) * halo_per_tile - 1, 0), 0)),
                  pl.BlockSpec((nm, d), lambda g: (meta_blk + real(g) // nst, 0)),
                  pl.BlockSpec((CONV_HALO, d), lambda g: (0, 0))]
        + [pl.BlockSpec((1, d), lambda g: (0, 0))] * 3
        + [pl.BlockSpec((nb * nm, d), lambda g: (0, 0))],
        out_specs=pl.BlockSpec((tr, d), lambda g: (g, 0)),
        out_shape=jax.ShapeDtypeStruct((t, d), BF16),
        scratch_shapes=[pltpu.VMEM((CONV_HALO + tr, d), F32), pltpu.VMEM((tr, d), F32),
                        pltpu.VMEM((7, CONV_HALO + tr - 8, CONV_LANES), F32)],
        compiler_params=_cparams(("arbitrary",)), name="conv_main")(u, u, u, w_pad, *vecs, out_meta)


def _mla_proj_body(ql, kvl, hd, scale, x_ref, cos_ref, sin_ref, w1_ref, qg_ref, kvg_ref, wq_ref, wkv_ref,
                   qn_ref, qr_ref, kn_ref, v_ref, kr_ref):
    a = jnp.dot(x_ref[...], w1_ref[...], preferred_element_type=F32)
    cos = cos_ref[...]
    sin = sin_ref[...]
    cq = _rms_norm(a[:, :ql], qg_ref[...]).astype(BF16)
    ckv = _rms_norm(a[:, ql:ql + kvl], kvg_ref[...]).astype(BF16)
    r0 = ql + kvl
    kr_ref[...] = (a[:, r0:r0 + LANES] * cos + a[:, r0 + LANES:r0 + 2 * LANES] * sin).astype(BF16)
    qa = jnp.dot(cq, wq_ref[...], preferred_element_type=F32)
    nh = hd // LANES
    cos_t = jnp.tile(cos, (1, nh))
    sin_t = jnp.tile(sin, (1, nh))
    qn_ref[...] = (qa[:, :hd] * scale).astype(BF16)
    qr_ref[...] = ((qa[:, hd:2 * hd] * cos_t + qa[:, 2 * hd:] * sin_t) * scale).astype(BF16)
    kv = jnp.dot(ckv, wkv_ref[...], preferred_element_type=F32)
    kn_ref[...] = kv[:, :hd].astype(BF16)
    v_ref[...] = kv[:, hd:].astype(BF16)


def _attn_body(tq, nb, qn_ref, qr_ref, kn_ref, kr_ref, v_ref, knm_ref, krm_ref, vm_ref, ometa_ref, o_ref, kf_ref):
    @pl.when(pl.program_id(0) == nb)
    def _():
        _write_meta_rows(ometa_ref, o_ref)

    @pl.when(pl.program_id(0) < nb)
    def _():
        _attn_tiles(tq, qn_ref, qr_ref, kn_ref, kr_ref, v_ref, knm_ref, krm_ref, vm_ref, o_ref, kf_ref)


def _attn_tiles(tq, qn_ref, qr_ref, kn_ref, kr_ref, v_ref, knm_ref, krm_ref, vm_ref, o_ref, kf_ref):
    s_len = qn_ref.shape[0]
    nt = (((1,), (1,)), ((), ()))
    kf_ref[:, :LANES] = kn_ref[...]
    kf_ref[:, LANES:] = kr_ref[...]
    km = jnp.concatenate([knm_ref[...], krm_ref[...]], axis=-1)
    vm = vm_ref[...]
    row_c = lax.broadcasted_iota(I32, (tq, tq), 0) // CHUNK
    col_c = lax.broadcasted_iota(I32, (tq, tq), 1) // CHUNK
    visible = col_c <= row_c
    for i in range(s_len // tq):
        r0 = i * tq
        q = jnp.concatenate([qn_ref[r0:r0 + tq, :], qr_ref[r0:r0 + tq, :]], axis=-1)
        s_m = lax.dot_general(q, km, nt, preferred_element_type=F32)
        s_d = lax.dot_general(q, kf_ref[r0:r0 + tq, :], nt, preferred_element_type=F32)
        s_d = jnp.where(visible, s_d, -jnp.inf)
        m = jnp.maximum(jnp.max(s_m, axis=-1, keepdims=True), jnp.max(s_d, axis=-1, keepdims=True))
        if i > 0:
            s_p = lax.dot_general(q, kf_ref[0:r0, :], nt, preferred_element_type=F32)
            m = jnp.maximum(m, jnp.max(s_p, axis=-1, keepdims=True))
        p_m = jnp.exp(s_m - m)
        p_d = jnp.exp(s_d - m)
        l = jnp.sum(p_m, axis=-1, keepdims=True) + jnp.sum(p_d, axis=-1, keepdims=True)
        acc = jnp.dot(p_m.astype(BF16), vm, preferred_element_type=F32)
        acc = acc + jnp.dot(p_d.astype(BF16), v_ref[r0:r0 + tq, :], preferred_element_type=F32)
        if i > 0:
            p_p = jnp.exp(s_p - m)
            l = l + jnp.sum(p_p, axis=-1, keepdims=True)
            acc = acc + jnp.dot(p_p.astype(BF16), v_ref[0:r0, :], preferred_element_type=F32)
        o_ref[r0:r0 + tq, :] = (acc / l).astype(BF16)


def _attn_meta_body(qn_ref, qr_ref, kn_ref, kr_ref, v_ref, o_ref):
    nt = (((1,), (1,)), ((), ()))
    kr = kr_ref[...]
    for h in range(o_ref.shape[1] // LANES):
        sl = slice(h * LANES, (h + 1) * LANES)
        q = jnp.concatenate([qn_ref[:, sl], qr_ref[:, sl]], axis=-1)
        k = jnp.concatenate([kn_ref[:, sl], kr], axis=-1)
        s = lax.dot_general(q, k, nt, preferred_element_type=F32)
        p = jnp.exp(s - jnp.max(s, axis=-1, keepdims=True))
        l = jnp.sum(p, axis=-1, keepdims=True)
        o_ref[:, sl] = (jnp.dot(p.astype(BF16), v_ref[:, sl], preferred_element_type=F32) / l).astype(BF16)


def _rope_rows(n_pos):
    inv = ROPE_THETA ** (-jnp.arange(0, MLA_ROPE, 2, dtype=F32) / MLA_ROPE)
    ang = jnp.arange(n_pos, dtype=F32)[:, None] * inv[None, :]
    pad = jnp.zeros((n_pos, LANES - MLA_ROPE), F32)
    cos = jnp.concatenate([jnp.cos(ang), jnp.cos(ang), pad], axis=1)
    sin = jnp.concatenate([jnp.sin(ang), jnp.sin(ang), pad], axis=1)
    return cos, sin


def _pad_rope_cols(w):
    half = MLA_ROPE // 2
    z = jnp.zeros((w.shape[0], LANES - MLA_ROPE), w.dtype)
    rot = jnp.concatenate([-w[:, half:], w[:, :half]], axis=1)
    return jnp.concatenate([w, z], axis=1), jnp.concatenate([rot, z], axis=1)


def mla_mixer(hb, wdq, q_norm_g, wuq, wdkv, kv_norm_g, wukv, nb, s_len, nm):
    t, d = hb.shape
    ql = wdq.shape[1]
    kvl = kv_norm_g.shape[0]
    nh = wuq.shape[1] // (HEAD_DIM + MLA_ROPE)
    hd = nh * HEAD_DIM
    tr_rows = nb * s_len
    scale = float((HEAD_DIM + MLA_ROPE) ** -0.5)
    kr_w, kr_rot = _pad_rope_cols(wdkv[:, kvl:])
    w1 = jnp.concatenate([wdq, wdkv[:, :kvl], kr_w, kr_rot], axis=1).astype(BF16)
    wuq3 = wuq.reshape(ql, nh, HEAD_DIM + MLA_ROPE)
    q_rope = wuq3[:, :, HEAD_DIM:]
    half = MLA_ROPE // 2
    zq = jnp.zeros((ql, nh, LANES - MLA_ROPE), F32)
    q_rope_p = jnp.concatenate([q_rope, zq], axis=2).reshape(ql, hd)
    q_rot_p = jnp.concatenate([-q_rope[:, :, half:], q_rope[:, :, :half], zq], axis=2).reshape(ql, hd)
    wq = jnp.concatenate([wuq3[:, :, :HEAD_DIM].reshape(ql, hd), q_rope_p, q_rot_p], axis=1).astype(BF16)
    wukv3 = wukv.reshape(kvl, nh, 2 * HEAD_DIM)
    wkv = jnp.concatenate([wukv3[:, :, :HEAD_DIM].reshape(kvl, hd),
                           wukv3[:, :, HEAD_DIM:].reshape(kvl, hd)], axis=1).astype(BF16)
    cos_p, sin_p = _rope_rows(nm + s_len)
    cos_rows = jnp.concatenate([jnp.tile(cos_p[nm:], (nb, 1)), jnp.tile(cos_p[:nm], (nb, 1))], axis=0)
    sin_rows = jnp.concatenate([jnp.tile(sin_p[nm:], (nb, 1)), jnp.tile(sin_p[:nm], (nb, 1))], axis=0)

    tm = _pick(t, (256, 128, 96, 64, 48, 32, 16))
    qn, qr, kn, v, kr = _rows_call(
        functools.partial(_mla_proj_body, ql, kvl, hd, scale), t, tm, [hb, cos_rows, sin_rows],
        [w1, q_norm_g.reshape(1, ql), kv_norm_g.reshape(1, kvl), wq, wkv],
        [(hd, BF16), (hd, BF16), (hd, BF16), (hd, BF16), (LANES, BF16)], "mla_proj")

    meta_blk = tr_rows // nm
    mrow = lambda n: pl.BlockSpec((nm, n), lambda b: (meta_blk + b, 0))
    o_meta = pl.pallas_call(
        _attn_meta_body, grid=(nb,),
        in_specs=[mrow(hd), mrow(hd), mrow(hd), mrow(LANES), mrow(hd)],
        out_specs=pl.BlockSpec((nm, hd), lambda b: (b, 0)),
        out_shape=jax.ShapeDtypeStruct((nb * nm, hd), BF16),
        compiler_params=_cparams(("parallel",)), name="attn_meta")(qn, qr, kn, kr, v)

    tq = 512
    assert nb * nm <= s_len

    def real(b):
        return jnp.minimum(b, nb - 1)

    kspec = lambda: pl.BlockSpec((s_len, LANES), lambda b, h: (real(b), h))
    kspec0 = lambda: pl.BlockSpec((s_len, LANES), lambda b, h: (real(b), 0))
    m2 = lambda: pl.BlockSpec((nm, LANES), lambda b, h: (meta_blk + real(b), h))
    m20 = lambda: pl.BlockSpec((nm, LANES), lambda b, h: (meta_blk + real(b), 0))
    return pl.pallas_call(
        functools.partial(_attn_body, tq, nb), grid=(nb + 1, nh),
        in_specs=[kspec(), kspec(), kspec(), kspec0(), kspec(), m2(), m20(), m2(),
                  pl.BlockSpec((nb * nm, LANES), lambda b, h: (0, h))],
        out_specs=pl.BlockSpec((s_len, LANES), lambda b, h: (b, h)),
        out_shape=jax.ShapeDtypeStruct((t, hd), BF16),
        scratch_shapes=[pltpu.VMEM((s_len, 2 * LANES), BF16)],
        compiler_params=_cparams(("arbitrary", "arbitrary")), name="attn_main")(
            qn, qr, kn, kr, v, kn, kr, v, o_meta)


def _hgrn_gates(fz, lb):
    return lb + (1.0 - lb) * _sigmoid(fz), (1.0 - lb) * _sigmoid(-fz)


def _hgrn_level_index(c):
    t = lax.broadcasted_iota(I32, (c, c), 0)
    s = lax.broadcasted_iota(I32, (c, c), 1)
    lvl = 31 - lax.clz(t ^ s)
    return jnp.where(s > t, -2, lvl)


def _hgrn_block(q, f, k, iv, st, lvl):
    c = q.shape[0]
    nt = (((1,), (1,)), ((), ()))
    tn = (((0,), (0,)), ((), ()))
    row = lax.broadcasted_iota(I32, (c, HEAD_DIM), 0)
    scores = jnp.where(lvl == -1, lax.dot_general(q.astype(BF16), k.astype(BF16), nt,
                                                  preferred_element_type=F32), 0.0)
    qa = q * f
    kb = k
    tot = f
    h, idx = 1, 0
    while h < c:
        prod = lax.dot_general(qa.astype(BF16), kb.astype(BF16), nt, preferred_element_type=F32)
        scores = jnp.where(lvl == idx, prod, scores)
        right = (row & h) != 0
        left_tot = pltpu.roll(tot, h, 0)
        right_tot = pltpu.roll(tot, c - h, 0)
        qa = qa * jnp.where(right, left_tot, 1.0)
        kb = kb * jnp.where(right, 1.0, right_tot)
        tot = tot * jnp.where(right, left_tot, right_tot)
        h, idx = 2 * h, idx + 1
    ib = iv.astype(BF16)
    o = jnp.dot(scores.astype(BF16), ib, preferred_element_type=F32)
    o = o + lax.dot_general(qa.astype(BF16), st.astype(BF16), nt, preferred_element_type=F32)
    st_new = st * tot[0:1, :] + lax.dot_general(ib, kb.astype(BF16), tn, preferred_element_type=F32)
    return o, st_new


def _hgrn_out(o, gate, ng):
    o = o * lax.rsqrt(jnp.mean(o * o, axis=-1, keepdims=True) + RMS_EPS) * ng
    return (o * (gate * _sigmoid(gate))).astype(BF16)


def _hgrn_meta_body(gh, q_ref, fz_ref, i_ref, g_ref, lb_ref, ng_ref, o_ref, st_ref):
    nm = q_ref.shape[0]
    c = HGRN_CHUNK
    lvl = _hgrn_level_index(c)
    zeros = jnp.zeros((c - nm, HEAD_DIM), F32)
    for g in range(gh):
        sl = slice(g * HEAD_DIM, (g + 1) * HEAD_DIM)
        f, k = _hgrn_gates(fz_ref[:, sl], lb_ref[:, sl])
        o, st = _hgrn_block(jnp.concatenate([zeros, q_ref[:, sl]], axis=0),
                            jnp.concatenate([zeros + 1.0, f], axis=0),
                            jnp.concatenate([zeros, k], axis=0),
                            jnp.concatenate([zeros, i_ref[:, sl]], axis=0),
                            jnp.zeros((HEAD_DIM, HEAD_DIM), F32), lvl)
        st_ref[0, g] = st
        o_ref[:, sl] = _hgrn_out(o[c - nm:], g_ref[:, sl], ng_ref[...])


def _hgrn_main_body(gh, ts, nst, n_real, q_ref, fz_ref, i_ref, g_ref, lb_ref, ng_ref, st0_ref, ometa_ref, o_ref,
                    st_ref):
    c = HGRN_CHUNK
    step = pl.program_id(1)

    @pl.when(step == n_real)
    def _():
        _write_meta_rows(ometa_ref, o_ref)

    @pl.when(step < n_real)
    def _():
        lvl = _hgrn_level_index(c)

        @pl.when(step % nst == 0)
        def _():
            st_ref[...] = st0_ref[0]

        def chunk_body(cidx, carry):
            rows = pl.ds(pl.multiple_of(cidx * c, c), c)
            for g in range(gh):
                sl = slice(g * HEAD_DIM, (g + 1) * HEAD_DIM)
                f, k = _hgrn_gates(fz_ref[rows, sl], lb_ref[:, sl])
                o, st = _hgrn_block(q_ref[rows, sl], f, k, i_ref[rows, sl], st_ref[g], lvl)
                st_ref[g] = st
                o_ref[rows, sl] = _hgrn_out(o, g_ref[rows, sl], ng_ref[...])
            return carry

        lax.fori_loop(0, ts // c, chunk_body, 0)


def hgrn_mixer(qfig, lb, norm_g, nb, s_len, nm):
    t, d4 = qfig.shape
    d = d4 // 4
    nh = d // HEAD_DIM
    gh = 4 if nh % 4 == 0 else 1
    gw = gh * HEAD_DIM
    ng_blocks = d // gw
    tr_rows = nb * s_len
    meta_blk = tr_rows // nm
    lb2 = lb.reshape(1, d)
    ng2 = norm_g.reshape(1, HEAD_DIM)

    def mspec(sec):
        return pl.BlockSpec((nm, gw), lambda b, h: (meta_blk + b, sec * ng_blocks + h))

    o_meta, st0 = pl.pallas_call(
        functools.partial(_hgrn_meta_body, gh), grid=(nb, ng_blocks),
        in_specs=[mspec(0), mspec(1), mspec(2), mspec(3),
                  pl.BlockSpec((1, gw), lambda b, h: (0, h)),
                  pl.BlockSpec((1, HEAD_DIM), lambda b, h: (0, 0))],
        out_specs=[pl.BlockSpec((nm, gw), lambda b, h: (b, h)),
                   pl.BlockSpec((1, gh, HEAD_DIM, HEAD_DIM), lambda b, h: (b, h, 0, 0))],
        out_shape=[jax.ShapeDtypeStruct((nb * nm, d), BF16),
                   jax.ShapeDtypeStruct((nb, nh, HEAD_DIM, HEAD_DIM), F32)],
        compiler_params=_cparams(("parallel", "parallel")), name="hgrn_meta")(qfig, qfig, qfig, qfig, lb2, ng2)

    ts = _pick(s_len, (512, 256))
    nst = s_len // ts
    n_real = nb * nst
    assert nb * nm <= ts

    def real(g):
        return jnp.minimum(g, n_real - 1)

    def rspec(sec):
        return pl.BlockSpec((ts, gw), lambda h, g: (real(g), sec * ng_blocks + h))

    return pl.pallas_call(
        functools.partial(_hgrn_main_body, gh, ts, nst, n_real), grid=(ng_blocks, n_real + 1),
        in_specs=[rspec(0), rspec(1), rspec(2), rspec(3),
                  pl.BlockSpec((1, gw), lambda h, g: (0, h)),
                  pl.BlockSpec((1, HEAD_DIM), lambda h, g: (0, 0)),
                  pl.BlockSpec((1, gh, HEAD_DIM, HEAD_DIM), lambda h, g: (real(g) // nst, h, 0, 0)),
                  pl.BlockSpec((nb * nm, gw), lambda h, g: (0, h))],
        out_specs=pl.BlockSpec((ts, gw), lambda h, g: (g, h)),
        out_shape=jax.ShapeDtypeStruct((t, d), BF16),
        scratch_shapes=[pltpu.VMEM((gh, HEAD_DIM, HEAD_DIM), F32)],
        compiler_params=_cparams(("parallel", "arbitrary")), name="hgrn_main")(
            qfig, qfig, qfig, qfig, lb2, ng2, st0, o_meta)


def _first_index_of_max(vals, idx, n, axis):
    mx = jnp.max(vals, axis=axis, keepdims=True)
    first = jnp.min(jnp.where(vals == mx, idx, n), axis=axis, keepdims=True)
    return mx, first


def _router_body(tm, h_ref, hb_ref, rw_ref, rb_ref, tri_ref, e_ref, gate_ref, rank_ref, cnt_ref, base_ref):
    @pl.when(pl.program_id(0) == 0)
    def _():
        base_ref[...] = jnp.zeros(base_ref.shape, F32)

    nt = (((1,), (1,)), ((), ()))
    hb = hb_ref[...]
    resid = (h_ref[...] - hb.astype(F32)).astype(BF16)
    main = lax.dot_general(rw_ref[...], hb, nt, preferred_element_type=F32)
    logits = (main[:N_EXPERTS] + main[N_EXPERTS:2 * N_EXPERTS] + main[2 * N_EXPERTS:]
              + lax.dot_general(rw_ref[:N_EXPERTS, :], resid, nt, preferred_element_type=F32))
    scores = _sigmoid(logits)
    sel = scores + rb_ref[...]
    g, epg = N_GROUPS, EXPERTS_PER_GROUP
    sel3 = sel.reshape(g, epg, tm)
    sc3 = scores.reshape(g, epg, tm)
    idx3 = lax.broadcasted_iota(I32, (g, epg, tm), 1)
    m1, i1 = _first_index_of_max(sel3, idx3, epg, 1)
    rest = jnp.where(idx3 == i1, -jnp.inf, sel3)
    m2, i2 = _first_index_of_max(rest, idx3, epg, 1)
    gidx = lax.broadcasted_iota(I32, (g, 1, tm), 0)
    _, gtop3 = _first_index_of_max(m1 + m2, gidx, g, 0)
    pick = gidx == gtop3
    gtop = gtop3[0]
    l1 = jnp.sum(jnp.where(pick, i1, 0), axis=0)
    l2 = jnp.sum(jnp.where(pick, i2, 0), axis=0)
    sc_in = jnp.sum(jnp.where(pick, sc3, 0.0), axis=0)
    idx2 = lax.broadcasted_iota(I32, (epg, tm), 0)
    s1 = jnp.sum(jnp.where(idx2 == l1, sc_in, 0.0), axis=0, keepdims=True)
    s2 = jnp.sum(jnp.where(idx2 == l2, sc_in, 0.0), axis=0, keepdims=True)
    e1 = gtop * epg + l1
    e2 = gtop * epg + l2
    e_ref[0:1, :] = e1
    e_ref[1:2, :] = e2
    den = s1 + s2
    gate_ref[0:1, :] = s1 / den
    gate_ref[1:2, :] = s2 / den
    eidx = lax.broadcasted_iota(I32, (N_EXPERTS, tm), 0)
    oh1 = (eidx == e1).astype(F32)
    oh2 = (eidx == e2).astype(F32)
    oh = jnp.concatenate([oh1, oh2], axis=0).astype(BF16)
    pre = jnp.dot(oh, tri_ref[...], preferred_element_type=F32)
    base = base_ref[:, 0:1]
    tot1 = jnp.sum(oh1, axis=1, keepdims=True)
    tot2 = jnp.sum(oh2, axis=1, keepdims=True)
    r1 = jnp.sum(oh1 * (base + pre[:N_EXPERTS]), axis=0, keepdims=True)
    r2 = jnp.sum(oh2 * (base + tot1 + pre[N_EXPERTS:]), axis=0, keepdims=True)
    rank_ref[0:1, :] = r1.astype(I32)
    rank_ref[1:2, :] = r2.astype(I32)
    new_base = jnp.broadcast_to(base + tot1 + tot2, base_ref.shape)
    base_ref[...] = new_base
    cnt_ref[...] = new_base


def moe_route(h, hb, router_w, router_b):
    t, d = h.shape
    tm = _pick(t, (256, 128, 96, 64, 32))
    tri = (lax.broadcasted_iota(I32, (tm, tm), 0) < lax.broadcasted_iota(I32, (tm, tm), 1)).astype(BF16)
    rw_t = router_w.T.astype(F32)
    rw1 = rw_t.astype(BF16)
    rw2 = (rw_t - rw1.astype(F32)).astype(BF16)
    rw3 = (rw_t - rw1.astype(F32) - rw2.astype(F32)).astype(BF16)
    rw_pieces = jnp.concatenate([rw1, rw2, rw3], axis=0)
    e, gate, rank, cnt = pl.pallas_call(
        functools.partial(_router_body, tm), grid=(t // tm,),
        in_specs=[pl.BlockSpec((tm, d), lambda i: (i, 0)),
                  pl.BlockSpec((tm, d), lambda i: (i, 0)),
                  pl.BlockSpec((3 * N_EXPERTS, d), lambda i: (0, 0)),
                  pl.BlockSpec((N_EXPERTS, 1), lambda i: (0, 0)),
                  pl.BlockSpec((tm, tm), lambda i: (0, 0))],
        out_specs=[pl.BlockSpec((2, tm), lambda i: (0, i)),
                   pl.BlockSpec((2, tm), lambda i: (0, i)),
                   pl.BlockSpec((2, tm), lambda i: (0, i)),
                   pl.BlockSpec((N_EXPERTS, LANES), lambda i: (0, 0))],
        out_shape=[jax.ShapeDtypeStruct((2, t), I32), jax.ShapeDtypeStruct((2, t), F32),
                   jax.ShapeDtypeStruct((2, t), I32), jax.ShapeDtypeStruct((N_EXPERTS, LANES), F32)],
        scratch_shapes=[pltpu.VMEM((N_EXPERTS, LANES), F32)],
        compiler_params=_cparams(("arbitrary",)), name="moe_route")(
            h, hb, rw_pieces, router_b.reshape(N_EXPERTS, 1), tri)
    return e, gate, rank, cnt[:, 0].astype(I32)


def _dispatch_body(tt, zero_flag_ref, dest_ref, hb_ref, xb_hbm, stage_ref, zero_ref, sem):
    half = stage_ref.shape[1]

    @pl.when(pl.program_id(0) == 0)
    def _():
        zero_ref[...] = jnp.zeros(zero_ref.shape, U32)

        def zfill(b, carry):
            @pl.when(zero_flag_ref[b] > 0)
            def _():
                start = pl.multiple_of(b * MOE_BLOCK, MOE_BLOCK)
                cp = pltpu.make_async_copy(zero_ref, xb_hbm.at[pl.ds(start, MOE_BLOCK), :], sem)
                cp.start()
                cp.wait()
            return carry

        lax.fori_loop(0, zero_flag_ref.shape[0], zfill, 0)

    stage_ref[...] = _pack_bf16_pair(hb_ref[:, :half].astype(F32), hb_ref[:, half:].astype(F32))

    def issue(j, carry):
        for k in range(2):
            pltpu.make_async_copy(stage_ref.at[pl.ds(j, 1), :],
                                  xb_hbm.at[pl.ds(dest_ref[k, j], 1), :], sem).start(priority=k)
        return carry

    lax.fori_loop(0, tt, issue, 0, unroll=8)
    for k in range(2):
        pltpu.make_async_copy(stage_ref, xb_hbm.at[pl.ds(0, tt), :], sem).wait()


def moe_dispatch(hb, dest, zero_flag, n_slots):
    t, d = hb.shape
    tt = _pick(t, (256, 128, 96, 64, 32))
    return pl.pallas_call(
        functools.partial(_dispatch_body, tt),
        grid_spec=pltpu.PrefetchScalarGridSpec(
            num_scalar_prefetch=1, grid=(t // tt,),
            in_specs=[pl.BlockSpec((2, tt), lambda i, zf: (0, i), memory_space=pltpu.SMEM),
                      pl.BlockSpec((tt, d), lambda i, zf: (i, 0))],
            out_specs=pl.BlockSpec(memory_space=pl.ANY),
            scratch_shapes=[pltpu.VMEM((tt, d // 2), U32), pltpu.VMEM((MOE_BLOCK, d // 2), U32),
                            pltpu.SemaphoreType.DMA(())]),
        out_shape=jax.ShapeDtypeStruct((n_slots, d // 2), U32),
        compiler_params=_cparams(("arbitrary",)), name="moe_dispatch")(zero_flag, dest, hb)


def _expert_body(be_ref, nu_ref, x_ref, wg_ref, wu_ref, wd_ref, y_ref, wgb_ref, wub_ref, wdb_ref):
    i = pl.program_id(0)
    used = i < nu_ref[0]
    half = x_ref.shape[1]

    @pl.when(jnp.logical_or(i == 0, be_ref[i] != be_ref[jnp.maximum(i - 1, 0)]))
    def _():
        wgb_ref[...] = wg_ref[0, 0].astype(BF16)
        wub_ref[...] = wu_ref[0, 0].astype(BF16)
        wdb_ref[...] = wd_ref[0, 0].astype(BF16)

    @pl.when(used)
    def _():
        lo, hi = _unpack_bf16_pair(x_ref[...])
        lo = lo.astype(BF16)
        hi = hi.astype(BF16)
        g = (jnp.dot(lo, wgb_ref[:half, :], preferred_element_type=F32)
             + jnp.dot(hi, wgb_ref[half:, :], preferred_element_type=F32))
        u = (jnp.dot(lo, wub_ref[:half, :], preferred_element_type=F32)
             + jnp.dot(hi, wub_ref[half:, :], preferred_element_type=F32))
        a = (g * _sigmoid(g) * u).astype(BF16)
        y = jnp.dot(a, wdb_ref[...], preferred_element_type=F32)
        y_ref[...] = _pack_bf16_pair(y[:, :half], y[:, half:])

    @pl.when(jnp.logical_not(used))
    def _():
        y_ref[...] = jnp.zeros(y_ref.shape, U32)


def moe_experts(xb, blk_expert, n_used, wg, wu, wd, layer):
    n_slots, half = xb.shape
    d = 2 * half
    ff = wg.shape[3]
    nblk = n_slots // MOE_BLOCK
    return pl.pallas_call(
        _expert_body,
        grid_spec=pltpu.PrefetchScalarGridSpec(
            num_scalar_prefetch=2, grid=(nblk,),
            in_specs=[pl.BlockSpec((MOE_BLOCK, half), lambda i, be, nu: (i, 0)),
                      pl.BlockSpec((1, 1, d, ff), lambda i, be, nu: (layer, be[i], 0, 0)),
                      pl.BlockSpec((1, 1, d, ff), lambda i, be, nu: (layer, be[i], 0, 0)),
                      pl.BlockSpec((1, 1, ff, d), lambda i, be, nu: (layer, be[i], 0, 0))],
            out_specs=pl.BlockSpec((MOE_BLOCK, half), lambda i, be, nu: (i, 0)),
            scratch_shapes=[pltpu.VMEM((d, ff), BF16), pltpu.VMEM((d, ff), BF16), pltpu.VMEM((ff, d), BF16)]),
        out_shape=jax.ShapeDtypeStruct((n_slots, half), U32),
        compiler_params=_cparams(("arbitrary",)), name="moe_experts")(
            blk_expert, n_used, xb, wg, wu, wd)


def _combine_body(tt, alpha, n_tiles, dest_ref, dnext_ref, gate_ref, h_ref, g_ref, b_ref, y_hbm, ho_ref, hb_ref,
                  buf_ref, sem):
    i = pl.program_id(0)
    slot = i % 2
    nslot = 1 - slot

    def gather(idx_ref, j, to_slot):
        for k in range(2):
            pltpu.make_async_copy(y_hbm.at[pl.ds(idx_ref[k, j], 1), :],
                                  buf_ref.at[to_slot, k, pl.ds(j, 1), :], sem.at[to_slot]).start(priority=k)

    def wait_slot(s):
        for k in range(2):
            pltpu.make_async_copy(y_hbm.at[pl.ds(0, tt), :], buf_ref.at[s, k], sem.at[s]).wait()

    @pl.when(i == 0)
    def _():
        def first(j, carry):
            gather(dest_ref, j, 0)
            return carry
        lax.fori_loop(0, tt, first, 0, unroll=8)

    wait_slot(slot)

    def ahead(j, carry):
        gather(dnext_ref, j, nslot)
        return carry

    lax.fori_loop(0, tt, ahead, 0, unroll=8)

    def rows_body(c, carry):
        r0 = pl.multiple_of(c * COMBINE_ROWS, COMBINE_ROWS)
        rows = pl.ds(r0, COMBINE_ROWS)
        gate = gate_ref[rows, :]
        lo0, hi0 = _unpack_bf16_pair(buf_ref[slot, 0, rows, :])
        lo1, hi1 = _unpack_bf16_pair(buf_ref[slot, 1, rows, :])
        ffn = jnp.concatenate([gate[:, 0:1] * lo0 + gate[:, 1:2] * lo1,
                               gate[:, 0:1] * hi0 + gate[:, 1:2] * hi1], axis=-1)
        o = _layer_norm(alpha * h_ref[rows, :] + ffn, g_ref[...], b_ref[...])
        ho_ref[rows, :] = o
        hb_ref[rows, :] = o.astype(BF16)
        return carry

    lax.fori_loop(0, tt // COMBINE_ROWS, rows_body, 0, unroll=4)

    @pl.when(i == n_tiles - 1)
    def _():
        wait_slot(nslot)


def moe_combine(yb, dest, gate_t, h, ln_g, ln_b, alpha, n_rows):
    t, d = h.shape
    tt = next(c for c in (256, 128, 96, 64, 32) if t % c == 0 and n_rows % c == 0)
    n_tiles = n_rows // tt
    return pl.pallas_call(
        functools.partial(_combine_body, tt, alpha, n_tiles), grid=(n_tiles,),
        in_specs=[pl.BlockSpec((2, tt), lambda i: (0, i), memory_space=pltpu.SMEM),
                  pl.BlockSpec((2, tt), lambda i: (0, jnp.minimum(i + 1, n_tiles - 1)), memory_space=pltpu.SMEM),
                  pl.BlockSpec((tt, 2), lambda i: (i, 0)),
                  pl.BlockSpec((tt, d), lambda i: (i, 0)),
                  pl.BlockSpec((1, d), lambda i: (0, 0)),
                  pl.BlockSpec((1, d), lambda i: (0, 0)),
                  pl.BlockSpec(memory_space=pl.ANY)],
        out_specs=[pl.BlockSpec((tt, d), lambda i: (i, 0)), pl.BlockSpec((tt, d), lambda i: (i, 0))],
        out_shape=[jax.ShapeDtypeStruct((n_rows, d), F32), jax.ShapeDtypeStruct((n_rows, d), BF16)],
        scratch_shapes=[pltpu.VMEM((2, 2, tt, d // 2), U32), pltpu.SemaphoreType.DMA((2,))],
        compiler_params=_cparams(("arbitrary",)), name="moe_combine")(
            dest, dest, gate_t, h, ln_g.reshape(1, d), ln_b.reshape(1, d), yb)


def moe_layer(h, hb, router_w, router_b, w_gate, w_up, w_down, layer, ln_g, ln_b, alpha, n_rows_out):
    t, d = h.shape
    e_idx, gate, rank, counts = moe_route(h, hb, router_w, router_b)
    padded = (counts + MOE_BLOCK - 1) // MOE_BLOCK * MOE_BLOCK
    pends = jnp.cumsum(padded)
    pstart = pends - padded
    experts = jnp.arange(N_EXPERTS, dtype=I32)[:, None, None]
    dest = jnp.sum(jnp.where(e_idx[None] == experts, pstart[:, None, None], 0), axis=0) + rank
    nblk = -(-(2 * t) // MOE_BLOCK) + N_EXPERTS
    n_used = (pends[-1] // MOE_BLOCK).astype(I32)
    blk = jnp.arange(nblk, dtype=I32)
    blk_first_row = jnp.minimum(blk, n_used - 1) * MOE_BLOCK
    blk_expert = jnp.minimum(jnp.sum((pends[None, :] <= blk_first_row[:, None]).astype(I32), axis=1),
                             N_EXPERTS - 1)
    zero_flag = ((blk >= n_used) | (blk == pends[blk_expert] // MOE_BLOCK - 1)).astype(I32)
    xb = moe_dispatch(hb, dest, zero_flag, nblk * MOE_BLOCK)
    yb = moe_experts(xb, blk_expert, n_used.reshape(1), w_gate, w_up, w_down, layer)
    return moe_combine(yb, dest, gate.T, h, ln_g, ln_b, alpha, n_rows_out)


def kernel(x, meta_tokens, ln_mix_g, ln_mix_b, ln_ffn_g, ln_ffn_b, conv_pw1_w, conv_pw1_b, conv_dw_w, conv_dw_b, conv_ln_g, conv_ln_b, conv_pw2_w, conv_pw2_b, mla_wdq, mla_q_norm_g, mla_wuq, mla_wdkv, mla_kv_norm_g, mla_wukv, mla_wo, hgrn_w_in, hgrn_lb_logits, hgrn_norm_g, hgrn_wo, router_w, router_b, moe_w_gate, moe_w_up, moe_w_down):
    nb, s_len, d = x.shape
    nm = meta_tokens.shape[0]
    depth = ln_mix_g.shape[0]
    alpha = float((2 * depth) ** 0.25)
    zero_bias = jnp.zeros((d,), F32)

    meta = jnp.broadcast_to(meta_tokens[None].astype(x.dtype), (nb, nm, d)).reshape(nb * nm, d)
    h = x.reshape(nb * s_len, d)
    tail = meta
    hb = jnp.concatenate([h.astype(BF16), meta.astype(BF16)], axis=0)
    p_lb = jax.nn.softmax(hgrn_lb_logits.astype(F32), axis=0)
    lower_bounds = jnp.cumsum(p_lb, axis=0) - p_lb[0]

    for i in range(depth):
        j = i // N_MIXERS
        kind = i % N_MIXERS
        if kind == 0:
            u = mm_glu(hb, conv_pw1_w[j].astype(BF16), conv_pw1_b[j])
            mix_in = conv_ln_swish(u, conv_dw_w[j], conv_dw_b[j], conv_ln_g[j], conv_ln_b[j], nb, s_len, nm)
            w_out, b_out = conv_pw2_w[j], conv_pw2_b[j]
        elif kind == 1:
            mix_in = mla_mixer(hb, mla_wdq[j], mla_q_norm_g[j], mla_wuq[j], mla_wdkv[j], mla_kv_norm_g[j],
                               mla_wukv[j], nb, s_len, nm)
            w_out, b_out = mla_wo[j], zero_bias
        else:
            qfig = mm_plain(hb, hgrn_w_in[j].astype(BF16))
            mix_in = hgrn_mixer(qfig, lower_bounds[i], hgrn_norm_g[j], nb, s_len, nm)
            w_out, b_out = hgrn_wo[j], zero_bias
        h, hb = mm_res_ln(mix_in, h, w_out.astype(BF16), b_out, ln_mix_g[i], ln_mix_b[i], alpha, tail)
        tail = None
        n_rows_out = nb * s_len if i == depth - 1 else h.shape[0]
        h, hb = moe_layer(h, hb, router_w, router_b, moe_w_gate, moe_w_up, moe_w_down, i,
                          ln_ffn_g[i], ln_ffn_b[i], alpha, n_rows_out)
    return h.reshape(nb, s_len, d)
```

```python
import functools

import jax
import jax.numpy as jnp
from jax import lax
from jax.experimental import pallas as pl
from jax.experimental.pallas import tpu as pltpu

F32 = jnp.float32
BF16 = jnp.bfloat16
I32 = jnp.int32
U32 = jnp.uint32

LANES = 128
CHUNK = 64
N_MIXERS = 3
CONV_WIDTH = 31
CONV_HALO = 32
CONV_LANES = 256
HEAD_DIM = 128
MLA_ROPE = 64
ROPE_THETA = 10000.0
HGRN_CHUNK = 128
N_EXPERTS = 32
N_GROUPS = 4
EXPERTS_PER_GROUP = N_EXPERTS // N_GROUPS
MOE_BLOCK = 512
COMBINE_ROWS = 16
LN_EPS = 1e-5
RMS_EPS = 1e-6
VMEM_LIMIT = 52 * 1024 * 1024


def _pick(n, cands):
    for c in cands:
        if n % c == 0:
            return c
    raise ValueError(f"no tile for {n} in {cands}")


_ROW_TILES = (768, 512, 384, 256, 192, 176, 128, 96, 64, 48, 32, 16)


def _cparams(sem, vmem=VMEM_LIMIT):
    return pltpu.CompilerParams(dimension_semantics=sem, vmem_limit_bytes=vmem)


def _layer_norm(x, g, b):
    mu = jnp.mean(x, axis=-1, keepdims=True)
    xc = x - mu
    var = jnp.mean(xc * xc, axis=-1, keepdims=True)
    return xc * lax.rsqrt(var + LN_EPS) * g + b


def _rms_norm(x, g):
    return x * lax.rsqrt(jnp.mean(x * x, axis=-1, keepdims=True) + RMS_EPS) * g


def _sigmoid(x):
    return 1.0 / (1.0 + jnp.exp(-x))


def _pack_bf16_pair(lo, hi):
    lo_b = lax.bitcast_convert_type(lo.astype(BF16).astype(F32), U32)
    hi_b = lax.bitcast_convert_type(hi.astype(BF16).astype(F32), U32)
    return (hi_b & jnp.uint32(0xFFFF0000)) | (lo_b >> 16)


def _unpack_bf16_pair(w):
    lo = lax.bitcast_convert_type(w << 16, F32)
    hi = lax.bitcast_convert_type(w & jnp.uint32(0xFFFF0000), F32)
    return lo, hi


def _write_meta_rows(meta_ref, o_ref):
    o_ref[...] = jnp.zeros(o_ref.shape, o_ref.dtype)
    o_ref[0:meta_ref.shape[0], :] = meta_ref[...]


def _rows_call(body, n_rows, tm, row_ins, full_ins, outs, name):
    grid = (n_rows // tm,)
    in_specs = [pl.BlockSpec((tm, a.shape[1]), lambda i: (i, 0)) for a in row_ins]
    in_specs += [pl.BlockSpec(a.shape, lambda i, nd=a.ndim: (0,) * nd, pipeline_mode=pl.Buffered(1))
                 for a in full_ins]
    out_specs = [pl.BlockSpec((tm, n), lambda i: (i, 0)) for n, _ in outs]
    out_shape = [jax.ShapeDtypeStruct((n_rows, n), dt) for n, dt in outs]
    return pl.pallas_call(
        body, grid=grid, in_specs=in_specs, out_specs=out_specs, out_shape=out_shape,
        compiler_params=_cparams(("parallel",)), name=name)(*row_ins, *full_ins)


def _mm_res_ln_body(alpha, x_ref, h_ref, w_ref, bias_ref, g_ref, b_ref, ho_ref, hb_ref):
    acc = jnp.dot(x_ref[...], w_ref[...], preferred_element_type=F32)
    y = alpha * h_ref[...] + (acc + bias_ref[...])
    o = _layer_norm(y, g_ref[...], b_ref[...])
    ho_ref[...] = o
    hb_ref[...] = o.astype(BF16)


def _mm_res_ln_tail_body(alpha, n_head, x_ref, h_ref, w_ref, bias_ref, g_ref, b_ref, tail_ref, ho_ref, hb_ref):
    tm = x_ref.shape[0]
    row = pl.program_id(0) * tm + lax.broadcasted_iota(I32, (tm, 1), 0)
    res = jnp.where(row < n_head, h_ref[...], tail_ref[...])
    acc = jnp.dot(x_ref[...], w_ref[...], preferred_element_type=F32)
    o = _layer_norm(alpha * res + (acc + bias_ref[...]), g_ref[...], b_ref[...])
    ho_ref[...] = o
    hb_ref[...] = o.astype(BF16)


def mm_res_ln(x_bf, h, w_bf, bias, g, b, alpha, tail=None):
    t, d = x_bf.shape[0], h.shape[1]
    tm = _pick(t, (384, 256, 192, 176, 128, 96, 64, 48, 32, 16))
    vecs = [w_bf, bias.reshape(1, d), g.reshape(1, d), b.reshape(1, d)]
    outs = [(d, F32), (d, BF16)]
    if tail is None:
        return _rows_call(functools.partial(_mm_res_ln_body, alpha), t, tm, [x_bf, h], vecs, outs, "mm_res_ln")
    n_head = h.shape[0]
    assert t - tm < n_head < t and tail.shape[0] == t - n_head
    tail_tile = jnp.concatenate([jnp.zeros((tm - tail.shape[0], d), tail.dtype), tail], axis=0)
    return _rows_call(functools.partial(_mm_res_ln_tail_body, alpha, n_head), t, tm, [x_bf, h],
                      vecs + [tail_tile], outs, "mm_res_ln")


def _glu_body(x_ref, wa_ref, wg_ref, ba_ref, bg_ref, u_ref):
    x = x_ref[...]
    a = jnp.dot(x, wa_ref[...], preferred_element_type=F32) + ba_ref[...]
    g = jnp.dot(x, wg_ref[...], preferred_element_type=F32) + bg_ref[...]
    u_ref[...] = a * _sigmoid(g)


def mm_glu(x_bf, w_bf, bias):
    t, k = x_bf.shape
    d = w_bf.shape[1] // 2
    tm = _pick(t, _ROW_TILES)
    tn = min(d, 1024)
    nj = d // tn
    bias2 = bias.reshape(1, 2 * d)
    return pl.pallas_call(
        _glu_body, grid=(nj, t // tm),
        in_specs=[pl.BlockSpec((tm, k), lambda j, i: (i, 0)),
                  pl.BlockSpec((k, tn), lambda j, i: (0, j)),
                  pl.BlockSpec((k, tn), lambda j, i: (0, j + nj)),
                  pl.BlockSpec((1, tn), lambda j, i: (0, j)),
                  pl.BlockSpec((1, tn), lambda j, i: (0, j + nj))],
        out_specs=pl.BlockSpec((tm, tn), lambda j, i: (i, j)),
        out_shape=jax.ShapeDtypeStruct((t, d), F32),
        compiler_params=_cparams(("parallel", "parallel")), name="mm_glu")(x_bf, w_bf, w_bf, bias2, bias2)


def _mm_plain_body(x_ref, w_ref, o_ref):
    o_ref[...] = jnp.dot(x_ref[...], w_ref[...], preferred_element_type=F32).astype(o_ref.dtype)


def mm_plain(x_bf, w_bf, out_dtype=F32):
    t, k = x_bf.shape
    n = w_bf.shape[1]
    tm = _pick(t, _ROW_TILES)
    tn = min(n, 2048)
    return pl.pallas_call(
        _mm_plain_body, grid=(n // tn, t // tm),
        in_specs=[pl.BlockSpec((tm, k), lambda j, i: (i, 0)),
                  pl.BlockSpec((k, tn), lambda j, i: (0, j))],
        out_specs=pl.BlockSpec((tm, tn), lambda j, i: (i, j)),
        out_shape=jax.ShapeDtypeStruct((t, n), out_dtype),
        compiler_params=_cparams(("parallel", "parallel")), name="mm_plain")(x_bf, w_bf)


def _conv_rows(ext_ref, w_ref, y_ref, sh_ref, rows, rc, lc):
    d = y_ref.shape[1]
    shift = CONV_HALO - (CONV_WIDTH - 1)
    sub = 8
    n_sh = sh_ref.shape[1]

    def lane_body(c, carry):
        l0 = pl.multiple_of(c * lc, lc)
        lanes = pl.ds(l0, lc)
        wv = w_ref[:, lanes]
        for b in range(1, sub):
            sh_ref[b - 1] = ext_ref[pl.ds(b, n_sh), lanes]
        for r in range(rows // rc):
            acc = None
            for k in range(CONV_WIDTH):
                b = (shift + k) % sub
                a = r * rc + (shift + k) - b
                src = ext_ref[pl.ds(a, rc), lanes] if b == 0 else sh_ref[b - 1, pl.ds(a, rc), :]
                term = src * wv[k:k + 1, :]
                acc = term if acc is None else acc + term
            y_ref[pl.ds(r * rc, rc), lanes] = acc
        return carry

    lax.fori_loop(0, d // lc, lane_body, 0)


def _conv_epilogue(y_ref, dwb_ref, g_ref, b_ref, o_ref):
    rows = y_ref.shape[0]
    step = 16

    def body(c, carry):
        sl = pl.ds(pl.multiple_of(c * step, step), step)
        z = _layer_norm(y_ref[sl, :] + dwb_ref[...], g_ref[...], b_ref[...])
        o_ref[sl, :] = (z * _sigmoid(z)).astype(BF16)
        return carry

    lax.fori_loop(0, rows // step, body, 0, unroll=min(4, rows // step))


def _conv_main_body(tr, nst, n_real, cur_ref, prev_ref, meta_ref, w_ref, dwb_ref, g_ref, b_ref, vmeta_ref, o_ref,
                    ext_ref, y_ref, sh_ref):
    step = pl.program_id(0)
    nm = meta_ref.shape[0]

    @pl.when(step < n_real)
    def _():
        @pl.when(step % nst == 0)
        def _():
            ext_ref[0:CONV_HALO - nm, :] = jnp.zeros((CONV_HALO - nm, ext_ref.shape[1]), F32)
            ext_ref[CONV_HALO - nm:CONV_HALO, :] = meta_ref[...]

        @pl.when(step % nst > 0)
        def _():
            ext_ref[0:CONV_HALO, :] = prev_ref[...]

        ext_ref[CONV_HALO:CONV_HALO + tr, :] = cur_ref[...]
        _conv_rows(ext_ref, w_ref, y_ref, sh_ref, tr, 64, CONV_LANES)
        _conv_epilogue(y_ref, dwb_ref, g_ref, b_ref, o_ref)

    @pl.when(step == n_real)
    def _():
        _write_meta_rows(vmeta_ref, o_ref)


def _conv_meta_body(meta_ref, w_ref, dwb_ref, g_ref, b_ref, o_ref, ext_ref, y_ref, sh_ref):
    nm = meta_ref.shape[0]
    ext_ref[0:CONV_HALO, :] = jnp.zeros((CONV_HALO, ext_ref.shape[1]), F32)
    ext_ref[CONV_HALO:CONV_HALO + nm, :] = meta_ref[...]
    _conv_rows(ext_ref, w_ref, y_ref, sh_ref, nm, nm, CONV_LANES)
    _conv_epilogue(y_ref, dwb_ref, g_ref, b_ref, o_ref)


def conv_ln_swish(u, dw_w, dw_b, ln_g, ln_b, nb, s_len, nm):
    t, d = u.shape
    tr_rows = nb * s_len
    tr = _pick(s_len, (512, 256))
    w_pad = jnp.concatenate([dw_w, jnp.zeros((CONV_HALO - CONV_WIDTH, d), F32)], axis=0)
    vecs = [dw_b.reshape(1, d), ln_g.reshape(1, d), ln_b.reshape(1, d)]
    meta_blk = tr_rows // nm
    out_meta = pl.pallas_call(
        _conv_meta_body, grid=(nb,),
        in_specs=[pl.BlockSpec((nm, d), lambda b: (meta_blk + b, 0)),
                  pl.BlockSpec((CONV_HALO, d), lambda b: (0, 0))]
        + [pl.BlockSpec((1, d), lambda b: (0, 0))] * 3,
        out_specs=pl.BlockSpec((nm, d), lambda b: (b, 0)),
        out_shape=jax.ShapeDtypeStruct((nb * nm, d), BF16),
        scratch_shapes=[pltpu.VMEM((CONV_HALO + nm, d), F32), pltpu.VMEM((nm, d), F32),
                        pltpu.VMEM((7, CONV_HALO + nm - 8, CONV_LANES), F32)],
        compiler_params=_cparams(("parallel",)), name="conv_meta")(u, w_pad, *vecs)
    nst = s_len // tr
    n_real = nb * nst
    halo_per_tile = tr // CONV_HALO
    assert nb * nm <= tr

    def real(g):
        return jnp.minimum(g, n_real - 1)

    return pl.pallas_call(
        functools.partial(_conv_main_body, tr, nst, n_real), grid=(n_real + 1,),
        in_specs=[pl.BlockSpec((tr, d), lambda g: (real(g), 0)),
                  pl.BlockSpec((CONV_HALO, d), lambda g: (jnp.maximum(real(g) * halo_per_tile - 1, 0), 0)),
                  pl.BlockSpec((nm, d), lambda g: (meta_blk + real(g) // nst, 0)),
                  pl.BlockSpec((CONV_HALO, d), lambda g: (0, 0))]
        + [pl.BlockSpec((1, d), lambda g: (0, 0))] * 3
        + [pl.BlockSpec((nb * nm, d), lambda g: (0, 0))],
        out_specs=pl.BlockSpec((tr, d), lambda g: (g, 0)),
        out_shape=jax.ShapeDtypeStruct((t, d), BF16),
        scratch_shapes=[pltpu.VMEM((CONV_HALO + tr, d), F32), pltpu.VMEM((tr, d), F32),
                        pltpu.VMEM((7, CONV_HALO + tr - 8, CONV_LANES), F32)],
        compiler_params=_cparams(("arbitrary",)), name="conv_main")(u, u, u, w_pad, *vecs, out_meta)


def _mla_proj_body(ql, kvl, hd, scale, x_ref, cos_ref, sin_ref, w1_ref, qg_ref, kvg_ref, wq_ref, wkv_ref,
                   qn_ref, qr_ref, kn_ref, v_ref, kr_ref):
    a = jnp.dot(x_ref[...], w1_ref[...], preferred_element_type=F32)
    cos = cos_ref[...]
    sin = sin_ref[...]
    cq = _rms_norm(a[:, :ql], qg_ref[...]).astype(BF16)
    ckv = _rms_norm(a[:, ql:ql + kvl], kvg_ref[...]).astype(BF16)
    r0 = ql + kvl
    kr_ref[...] = (a[:, r0:r0 + LANES] * cos + a[:, r0 + LANES:r0 + 2 * LANES] * sin).astype(BF16)
    qa = jnp.dot(cq, wq_ref[...], preferred_element_type=F32)
    nh = hd // LANES
    cos_t = jnp.tile(cos, (1, nh))
    sin_t = jnp.tile(sin, (1, nh))
    qn_ref[...] = (qa[:, :hd] * scale).astype(BF16)
    qr_ref[...] = ((qa[:, hd:2 * hd] * cos_t + qa[:, 2 * hd:] * sin_t) * scale).astype(BF16)
    kv = jnp.dot(ckv, wkv_ref[...], preferred_element_type=F32)
    kn_ref[...] = kv[:, :hd].astype(BF16)
    v_ref[...] = kv[:, hd:].astype(BF16)


def _attn_body(tq, nb, qn_ref, qr_ref, kn_ref, kr_ref, v_ref, knm_ref, krm_ref, vm_ref, ometa_ref, o_ref, kf_ref):
    @pl.when(pl.program_id(0) == nb)
    def _():
        _write_meta_rows(ometa_ref, o_ref)

    @pl.when(pl.program_id(0) < nb)
    def _():
        _attn_tiles(tq, qn_ref, qr_ref, kn_ref, kr_ref, v_ref, knm_ref, krm_ref, vm_ref, o_ref, kf_ref)


def _attn_tiles(tq, qn_ref, qr_ref, kn_ref, kr_ref, v_ref, knm_ref, krm_ref, vm_ref, o_ref, kf_ref):
    s_len = qn_ref.shape[0]
    nt = (((1,), (1,)), ((), ()))
    kf_ref[:, :LANES] = kn_ref[...]
    kf_ref[:, LANES:] = kr_ref[...]
    km = jnp.concatenate([knm_ref[...], krm_ref[...]], axis=-1)
    vm = vm_ref[...]
    row_c = lax.broadcasted_iota(I32, (tq, tq), 0) // CHUNK
    col_c = lax.broadcasted_iota(I32, (tq, tq), 1) // CHUNK
    visible = col_c <= row_c
    for i in range(s_len // tq):
        r0 = i * tq
        q = jnp.concatenate([qn_ref[r0:r0 + tq, :], qr_ref[r0:r0 + tq, :]], axis=-1)
        s_m = lax.dot_general(q, km, nt, preferred_element_type=F32)
        s_d = lax.dot_general(q, kf_ref[r0:r0 + tq, :], nt, preferred_element_type=F32)
        s_d = jnp.where(visible, s_d, -jnp.inf)
        m = jnp.maximum(jnp.max(s_m, axis=-1, keepdims=True), jnp.max(s_d, axis=-1, keepdims=True))
        if i > 0:
            s_p = lax.dot_general(q, kf_ref[0:r0, :], nt, preferred_element_type=F32)
            m = jnp.maximum(m, jnp.max(s_p, axis=-1, keepdims=True))
        p_m = jnp.exp(s_m - m)
        p_d = jnp.exp(s_d - m)
        l = jnp.sum(p_m, axis=-1, keepdims=True) + jnp.sum(p_d, axis=-1, keepdims=True)
        acc = jnp.dot(p_m.astype(BF16), vm, preferred_element_type=F32)
        acc = acc + jnp.dot(p_d.astype(BF16), v_ref[r0:r0 + tq, :], preferred_element_type=F32)
        if i > 0:
            p_p = jnp.exp(s_p - m)
            l = l + jnp.sum(p_p, axis=-1, keepdims=True)
            acc = acc + jnp.dot(p_p.astype(BF16), v_ref[0:r0, :], preferred_element_type=F32)
        o_ref[r0:r0 + tq, :] = (acc / l).astype(BF16)


def _attn_meta_body(qn_ref, qr_ref, kn_ref, kr_ref, v_ref, o_ref):
    nt = (((1,), (1,)), ((), ()))
    kr = kr_ref[...]
    for h in range(o_ref.shape[1] // LANES):
        sl = slice(h * LANES, (h + 1) * LANES)
        q = jnp.concatenate([qn_ref[:, sl], qr_ref[:, sl]], axis=-1)
        k = jnp.concatenate([kn_ref[:, sl], kr], axis=-1)
        s = lax.dot_general(q, k, nt, preferred_element_type=F32)
        p = jnp.exp(s - jnp.max(s, axis=-1, keepdims=True))
        l = jnp.sum(p, axis=-1, keepdims=True)
        o_ref[:, sl] = (jnp.dot(p.astype(BF16), v_ref[:, sl], preferred_element_type=F32) / l).astype(BF16)


def _rope_rows(n_pos):
    inv = ROPE_THETA ** (-jnp.arange(0, MLA_ROPE, 2, dtype=F32) / MLA_ROPE)
    ang = jnp.arange(n_pos, dtype=F32)[:, None] * inv[None, :]
    pad = jnp.zeros((n_pos, LANES - MLA_ROPE), F32)
    cos = jnp.concatenate([jnp.cos(ang), jnp.cos(ang), pad], axis=1)
    sin = jnp.concatenate([jnp.sin(ang), jnp.sin(ang), pad], axis=1)
    return cos, sin


def _pad_rope_cols(w):
    half = MLA_ROPE // 2
    z = jnp.zeros((w.shape[0], LANES - MLA_ROPE), w.dtype)
    rot = jnp.concatenate([-w[:, half:], w[:, :half]], axis=1)
    return jnp.concatenate([w, z], axis=1), jnp.concatenate([rot, z], axis=1)


def mla_mixer(hb, wdq, q_norm_g, wuq, wdkv, kv_norm_g, wukv, nb, s_len, nm):
    t, d = hb.shape
    ql = wdq.shape[1]
    kvl = kv_norm_g.shape[0]
    nh = wuq.shape[1] // (HEAD_DIM + MLA_ROPE)
    hd = nh * HEAD_DIM
    tr_rows = nb * s_len
    scale = float((HEAD_DIM + MLA_ROPE) ** -0.5)
    kr_w, kr_rot = _pad_rope_cols(wdkv[:, kvl:])
    w1 = jnp.concatenate([wdq, wdkv[:, :kvl], kr_w, kr_rot], axis=1).astype(BF16)
    wuq3 = wuq.reshape(ql, nh, HEAD_DIM + MLA_ROPE)
    q_rope = wuq3[:, :, HEAD_DIM:]
    half = MLA_ROPE // 2
    zq = jnp.zeros((ql, nh, LANES - MLA_ROPE), F32)
    q_rope_p = jnp.concatenate([q_rope, zq], axis=2).reshape(ql, hd)
    q_rot_p = jnp.concatenate([-q_rope[:, :, half:], q_rope[:, :, :half], zq], axis=2).reshape(ql, hd)
    wq = jnp.concatenate([wuq3[:, :, :HEAD_DIM].reshape(ql, hd), q_rope_p, q_rot_p], axis=1).astype(BF16)
    wukv3 = wukv.reshape(kvl, nh, 2 * HEAD_DIM)
    wkv = jnp.concatenate([wukv3[:, :, :HEAD_DIM].reshape(kvl, hd),
                           wukv3[:, :, HEAD_DIM:].reshape(kvl, hd)], axis=1).astype(BF16)
    cos_p, sin_p = _rope_rows(nm + s_len)
    cos_rows = jnp.concatenate([jnp.tile(cos_p[nm:], (nb, 1)), jnp.tile(cos_p[:nm], (nb, 1))], axis=0)
    sin_rows = jnp.concatenate([jnp.tile(sin_p[nm:], (nb, 1)), jnp.tile(sin_p[:nm], (nb, 1))], axis=0)

    tm = _pick(t, (256, 128, 96, 64, 48, 32, 16))
    qn, qr, kn, v, kr = _rows_call(
        functools.partial(_mla_proj_body, ql, kvl, hd, scale), t, tm, [hb, cos_rows, sin_rows],
        [w1, q_norm_g.reshape(1, ql), kv_norm_g.reshape(1, kvl), wq, wkv],
        [(hd, BF16), (hd, BF16), (hd, BF16), (hd, BF16), (LANES, BF16)], "mla_proj")

    meta_blk = tr_rows // nm
    mrow = lambda n: pl.BlockSpec((nm, n), lambda b: (meta_blk + b, 0))
    o_meta = pl.pallas_call(
        _attn_meta_body, grid=(nb,),
        in_specs=[mrow(hd), mrow(hd), mrow(hd), mrow(LANES), mrow(hd)],
        out_specs=pl.BlockSpec((nm, hd), lambda b: (b, 0)),
        out_shape=jax.ShapeDtypeStruct((nb * nm, hd), BF16),
        compiler_params=_cparams(("parallel",)), name="attn_meta")(qn, qr, kn, kr, v)

    tq = 512
    assert nb * nm <= s_len

    def real(b):
        return jnp.minimum(b, nb - 1)

    kspec = lambda: pl.BlockSpec((s_len, LANES), lambda b, h: (real(b), h))
    kspec0 = lambda: pl.BlockSpec((s_len, LANES), lambda b, h: (real(b), 0))
    m2 = lambda: pl.BlockSpec((nm, LANES), lambda b, h: (meta_blk + real(b), h))
    m20 = lambda: pl.BlockSpec((nm, LANES), lambda b, h: (meta_blk + real(b), 0))
    return pl.pallas_call(
        functools.partial(_attn_body, tq, nb), grid=(nb + 1, nh),
        in_specs=[kspec(), kspec(), kspec(), kspec0(), kspec(), m2(), m20(), m2(),
                  pl.BlockSpec((nb * nm, LANES), lambda b, h: (0, h))],
        out_specs=pl.BlockSpec((s_len, LANES), lambda b, h: (b, h)),
        out_shape=jax.ShapeDtypeStruct((t, hd), BF16),
        scratch_shapes=[pltpu.VMEM((s_len, 2 * LANES), BF16)],
        compiler_params=_cparams(("arbitrary", "arbitrary")), name="attn_main")(
            qn, qr, kn, kr, v, kn, kr, v, o_meta)


def _hgrn_gates(fz, lb):
    s = _sigmoid(fz)
    return lb + (1.0 - lb) * s, (1.0 - lb) * (1.0 - s)


def _hgrn_level_index(c):
    t = lax.broadcasted_iota(I32, (c, c), 0)
    s = lax.broadcasted_iota(I32, (c, c), 1)
    lvl = 31 - lax.clz(t ^ s)
    return jnp.where(s > t, -2, lvl)


def _hgrn_block(q, f, k, iv, st, lvl):
    c = q.shape[0]
    nt = (((1,), (1,)), ((), ()))
    tn = (((0,), (0,)), ((), ()))
    row = lax.broadcasted_iota(I32, (c, HEAD_DIM), 0)
    scores = jnp.where(lvl == -1, lax.dot_general(q.astype(BF16), k.astype(BF16), nt,
                                                  preferred_element_type=F32), 0.0)
    qa = q * f
    kb = k
    tot = f
    h, idx = 1, 0
    while h < c:
        prod = lax.dot_general(qa.astype(BF16), kb.astype(BF16), nt, preferred_element_type=F32)
        scores = jnp.where(lvl == idx, prod, scores)
        right = (row & h) != 0
        left_tot = pltpu.roll(tot, h, 0)
        right_tot = pltpu.roll(tot, c - h, 0)
        qa = qa * jnp.where(right, left_tot, 1.0)
        kb = kb * jnp.where(right, 1.0, right_tot)
        tot = tot * jnp.where(right, left_tot, right_tot)
        h, idx = 2 * h, idx + 1
    ib = iv.astype(BF16)
    o = jnp.dot(scores.astype(BF16), ib, preferred_element_type=F32)
    o = o + lax.dot_general(qa.astype(BF16), st.astype(BF16), nt, preferred_element_type=F32)
    st_new = st * tot[0:1, :] + lax.dot_general(ib, kb.astype(BF16), tn, preferred_element_type=F32)
    return o, st_new


def _hgrn_out(o, gate, ng):
    o = o * lax.rsqrt(jnp.mean(o * o, axis=-1, keepdims=True) + RMS_EPS) * ng
    return (o * (gate * _sigmoid(gate))).astype(BF16)


def _hgrn_meta_body(gh, q_ref, fz_ref, i_ref, g_ref, lb_ref, ng_ref, o_ref, st_ref):
    nm = q_ref.shape[0]
    c = HGRN_CHUNK
    lvl = _hgrn_level_index(c)
    zeros = jnp.zeros((c - nm, HEAD_DIM), F32)
    for g in range(gh):
        sl = slice(g * HEAD_DIM, (g + 1) * HEAD_DIM)
        f, k = _hgrn_gates(fz_ref[:, sl], lb_ref[:, sl])
        o, st = _hgrn_block(jnp.concatenate([zeros, q_ref[:, sl]], axis=0),
                            jnp.concatenate([zeros + 1.0, f], axis=0),
                            jnp.concatenate([zeros, k], axis=0),
                            jnp.concatenate([zeros, i_ref[:, sl]], axis=0),
                            jnp.zeros((HEAD_DIM, HEAD_DIM), F32), lvl)
        st_ref[0, g] = st
        o_ref[:, sl] = _hgrn_out(o[c - nm:], g_ref[:, sl], ng_ref[...])


def _hgrn_main_body(gh, ts, nst, n_real, q_ref, fz_ref, i_ref, g_ref, lb_ref, ng_ref, st0_ref, ometa_ref, o_ref,
                    st_ref):
    c = HGRN_CHUNK
    step = pl.program_id(1)

    @pl.when(step == n_real)
    def _():
        _write_meta_rows(ometa_ref, o_ref)

    @pl.when(step < n_real)
    def _():
        lvl = _hgrn_level_index(c)

        @pl.when(step % nst == 0)
        def _():
            st_ref[...] = st0_ref[0]

        def chunk_body(cidx, carry):
            rows = pl.ds(pl.multiple_of(cidx * c, c), c)
            for g in range(gh):
                sl = slice(g * HEAD_DIM, (g + 1) * HEAD_DIM)
                f, k = _hgrn_gates(fz_ref[rows, sl], lb_ref[:, sl])
                o, st = _hgrn_block(q_ref[rows, sl], f, k, i_ref[rows, sl], st_ref[g], lvl)
                st_ref[g] = st
                o_ref[rows, sl] = _hgrn_out(o, g_ref[rows, sl], ng_ref[...])
            return carry

        lax.fori_loop(0, ts // c, chunk_body, 0)


def hgrn_mixer(qfig, lb, norm_g, nb, s_len, nm):
    t, d4 = qfig.shape
    d = d4 // 4
    nh = d // HEAD_DIM
    gh = 4 if nh % 4 == 0 else 1
    gw = gh * HEAD_DIM
    ng_blocks = d // gw
    tr_rows = nb * s_len
    meta_blk = tr_rows // nm
    lb2 = lb.reshape(1, d)
    ng2 = norm_g.reshape(1, HEAD_DIM)

    def mspec(sec):
        return pl.BlockSpec((nm, gw), lambda b, h: (meta_blk + b, sec * ng_blocks + h))

    o_meta, st0 = pl.pallas_call(
        functools.partial(_hgrn_meta_body, gh), grid=(nb, ng_blocks),
        in_specs=[mspec(0), mspec(1), mspec(2), mspec(3),
                  pl.BlockSpec((1, gw), lambda b, h: (0, h)),
                  pl.BlockSpec((1, HEAD_DIM), lambda b, h: (0, 0))],
        out_specs=[pl.BlockSpec((nm, gw), lambda b, h: (b, h)),
                   pl.BlockSpec((1, gh, HEAD_DIM, HEAD_DIM), lambda b, h: (b, h, 0, 0))],
        out_shape=[jax.ShapeDtypeStruct((nb * nm, d), BF16),
                   jax.ShapeDtypeStruct((nb, nh, HEAD_DIM, HEAD_DIM), F32)],
        compiler_params=_cparams(("parallel", "parallel")), name="hgrn_meta")(qfig, qfig, qfig, qfig, lb2, ng2)

    ts = _pick(s_len, (512, 256))
    nst = s_len // ts
    n_real = nb * nst
    assert nb * nm <= ts

    def real(g):
        return jnp.minimum(g, n_real - 1)

    def rspec(sec):
        return pl.BlockSpec((ts, gw), lambda h, g: (real(g), sec * ng_blocks + h))

    return pl.pallas_call(
        functools.partial(_hgrn_main_body, gh, ts, nst, n_real), grid=(ng_blocks, n_real + 1),
        in_specs=[rspec(0), rspec(1), rspec(2), rspec(3),
                  pl.BlockSpec((1, gw), lambda h, g: (0, h)),
                  pl.BlockSpec((1, HEAD_DIM), lambda h, g: (0, 0)),
                  pl.BlockSpec((1, gh, HEAD_DIM, HEAD_DIM), lambda h, g: (real(g) // nst, h, 0, 0)),
                  pl.BlockSpec((nb * nm, gw), lambda h, g: (0, h))],
        out_specs=pl.BlockSpec((ts, gw), lambda h, g: (g, h)),
        out_shape=jax.ShapeDtypeStruct((t, d), BF16),
        scratch_shapes=[pltpu.VMEM((gh, HEAD_DIM, HEAD_DIM), F32)],
        compiler_params=_cparams(("parallel", "arbitrary")), name="hgrn_main")(
            qfig, qfig, qfig, qfig, lb2, ng2, st0, o_meta)


def _first_index_of_max(vals, idx, n, axis):
    mx = jnp.max(vals, axis=axis, keepdims=True)
    first = jnp.min(jnp.where(vals == mx, idx, n), axis=axis, keepdims=True)
    return mx, first


def _router_body(tm, h_ref, hb_ref, rw_ref, rb_ref, tri_ref, e_ref, gate_ref, rank_ref, cnt_ref, base_ref):
    @pl.when(pl.program_id(0) == 0)
    def _():
        base_ref[...] = jnp.zeros(base_ref.shape, F32)

    nt = (((1,), (1,)), ((), ()))
    hb = hb_ref[...]
    resid = (h_ref[...] - hb.astype(F32)).astype(BF16)
    main = lax.dot_general(rw_ref[...], hb, nt, preferred_element_type=F32)
    logits = (main[:N_EXPERTS] + main[N_EXPERTS:2 * N_EXPERTS] + main[2 * N_EXPERTS:]
              + lax.dot_general(rw_ref[:N_EXPERTS, :], resid, nt, preferred_element_type=F32))
    scores = _sigmoid(logits)
    sel = scores + rb_ref[...]
    g, epg = N_GROUPS, EXPERTS_PER_GROUP
    sel3 = sel.reshape(g, epg, tm)
    sc3 = scores.reshape(g, epg, tm)
    idx3 = lax.broadcasted_iota(I32, (g, epg, tm), 1)
    m1, i1 = _first_index_of_max(sel3, idx3, epg, 1)
    rest = jnp.where(idx3 == i1, -jnp.inf, sel3)
    m2, i2 = _first_index_of_max(rest, idx3, epg, 1)
    gidx = lax.broadcasted_iota(I32, (g, 1, tm), 0)
    _, gtop3 = _first_index_of_max(m1 + m2, gidx, g, 0)
    pick = gidx == gtop3
    gtop = gtop3[0]
    l1 = jnp.sum(jnp.where(pick, i1, 0), axis=0)
    l2 = jnp.sum(jnp.where(pick, i2, 0), axis=0)
    sc_in = jnp.sum(jnp.where(pick, sc3, 0.0), axis=0)
    idx2 = lax.broadcasted_iota(I32, (epg, tm), 0)
    s1 = jnp.sum(jnp.where(idx2 == l1, sc_in, 0.0), axis=0, keepdims=True)
    s2 = jnp.sum(jnp.where(idx2 == l2, sc_in, 0.0), axis=0, keepdims=True)
    e1 = gtop * epg + l1
    e2 = gtop * epg + l2
    e_ref[0:1, :] = e1
    e_ref[1:2, :] = e2
    den = s1 + s2
    gate_ref[0:1, :] = s1 / den
    gate_ref[1:2, :] = s2 / den
    eidx = lax.broadcasted_iota(I32, (N_EXPERTS, tm), 0)
    oh1 = (eidx == e1).astype(F32)
    oh2 = (eidx == e2).astype(F32)
    oh = jnp.concatenate([oh1, oh2], axis=0).astype(BF16)
    pre = jnp.dot(oh, tri_ref[...], preferred_element_type=F32)
    base = base_ref[:, 0:1]
    tot1 = jnp.sum(oh1, axis=1, keepdims=True)
    tot2 = jnp.sum(oh2, axis=1, keepdims=True)
    r1 = jnp.sum(oh1 * (base + pre[:N_EXPERTS]), axis=0, keepdims=True)
    r2 = jnp.sum(oh2 * (base + tot1 + pre[N_EXPERTS:]), axis=0, keepdims=True)
    rank_ref[0:1, :] = r1.astype(I32)
    rank_ref[1:2, :] = r2.astype(I32)
    new_base = jnp.broadcast_to(base + tot1 + tot2, base_ref.shape)
    base_ref[...] = new_base
    cnt_ref[...] = new_base


def moe_route(h, hb, router_w, router_b):
    t, d = h.shape
    tm = _pick(t, (256, 128, 96, 64, 32))
    tri = (lax.broadcasted_iota(I32, (tm, tm), 0) < lax.broadcasted_iota(I32, (tm, tm), 1)).astype(BF16)
    rw_t = router_w.T.astype(F32)
    rw1 = rw_t.astype(BF16)
    rw2 = (rw_t - rw1.astype(F32)).astype(BF16)
    rw3 = (rw_t - rw1.astype(F32) - rw2.astype(F32)).astype(BF16)
    rw_pieces = jnp.concatenate([rw1, rw2, rw3], axis=0)
    e, gate, rank, cnt = pl.pallas_call(
        functools.partial(_router_body, tm), grid=(t // tm,),
        in_specs=[pl.BlockSpec((tm, d), lambda i: (i, 0)),
                  pl.BlockSpec((tm, d), lambda i: (i, 0)),
                  pl.BlockSpec((3 * N_EXPERTS, d), lambda i: (0, 0)),
                  pl.BlockSpec((N_EXPERTS, 1), lambda i: (0, 0)),
                  pl.BlockSpec((tm, tm), lambda i: (0, 0))],
        out_specs=[pl.BlockSpec((2, tm), lambda i: (0, i)),
                   pl.BlockSpec((2, tm), lambda i: (0, i)),
                   pl.BlockSpec((2, tm), lambda i: (0, i)),
                   pl.BlockSpec((N_EXPERTS, LANES), lambda i: (0, 0))],
        out_shape=[jax.ShapeDtypeStruct((2, t), I32), jax.ShapeDtypeStruct((2, t), F32),
                   jax.ShapeDtypeStruct((2, t), I32), jax.ShapeDtypeStruct((N_EXPERTS, LANES), F32)],
        scratch_shapes=[pltpu.VMEM((N_EXPERTS, LANES), F32)],
        compiler_params=_cparams(("arbitrary",)), name="moe_route")(
            h, hb, rw_pieces, router_b.reshape(N_EXPERTS, 1), tri)
    return e, gate, rank, cnt[:, 0].astype(I32)


def _dispatch_body(tt, zero_flag_ref, dest_ref, hb_ref, xb_hbm, stage_ref, zero_ref, sem):
    half = stage_ref.shape[1]

    @pl.when(pl.program_id(0) == 0)
    def _():
        zero_ref[...] = jnp.zeros(zero_ref.shape, U32)

        def zfill(b, carry):
            @pl.when(zero_flag_ref[b] > 0)
            def _():
                start = pl.multiple_of(b * MOE_BLOCK, MOE_BLOCK)
                cp = pltpu.make_async_copy(zero_ref, xb_hbm.at[pl.ds(start, MOE_BLOCK), :], sem)
                cp.start()
                cp.wait()
            return carry

        lax.fori_loop(0, zero_flag_ref.shape[0], zfill, 0)

    stage_ref[...] = _pack_bf16_pair(hb_ref[:, :half].astype(F32), hb_ref[:, half:].astype(F32))

    def issue(j, carry):
        for k in range(2):
            pltpu.make_async_copy(stage_ref.at[pl.ds(j, 1), :],
                                  xb_hbm.at[pl.ds(dest_ref[k, j], 1), :], sem).start(priority=k)
        return carry

    lax.fori_loop(0, tt, issue, 0, unroll=8)
    for k in range(2):
        pltpu.make_async_copy(stage_ref, xb_hbm.at[pl.ds(0, tt), :], sem).wait()


def moe_dispatch(hb, dest, zero_flag, n_slots):
    t, d = hb.shape
    tt = _pick(t, (256, 128, 96, 64, 32))
    return pl.pallas_call(
        functools.partial(_dispatch_body, tt),
        grid_spec=pltpu.PrefetchScalarGridSpec(
            num_scalar_prefetch=1, grid=(t // tt,),
            in_specs=[pl.BlockSpec((2, tt), lambda i, zf: (0, i), memory_space=pltpu.SMEM),
                      pl.BlockSpec((tt, d), lambda i, zf: (i, 0))],
            out_specs=pl.BlockSpec(memory_space=pl.ANY),
            scratch_shapes=[pltpu.VMEM((tt, d // 2), U32), pltpu.VMEM((MOE_BLOCK, d // 2), U32),
                            pltpu.SemaphoreType.DMA(())]),
        out_shape=jax.ShapeDtypeStruct((n_slots, d // 2), U32),
        compiler_params=_cparams(("arbitrary",)), name="moe_dispatch")(zero_flag, dest, hb)


def _expert_body(be_ref, nu_ref, x_ref, wg_ref, wu_ref, wd_ref, y_ref, wgb_ref, wub_ref, wdb_ref):
    i = pl.program_id(0)
    used = i < nu_ref[0]
    half = x_ref.shape[1]

    @pl.when(jnp.logical_or(i == 0, be_ref[i] != be_ref[jnp.maximum(i - 1, 0)]))
    def _():
        wgb_ref[...] = wg_ref[0, 0].astype(BF16)
        wub_ref[...] = wu_ref[0, 0].astype(BF16)
        wdb_ref[...] = wd_ref[0, 0].astype(BF16)

    @pl.when(used)
    def _():
        lo, hi = _unpack_bf16_pair(x_ref[...])
        lo = lo.astype(BF16)
        hi = hi.astype(BF16)
        g = (jnp.dot(lo, wgb_ref[:half, :], preferred_element_type=F32)
             + jnp.dot(hi, wgb_ref[half:, :], preferred_element_type=F32))
        u = (jnp.dot(lo, wub_ref[:half, :], preferred_element_type=F32)
             + jnp.dot(hi, wub_ref[half:, :], preferred_element_type=F32))
        a = (g * _sigmoid(g) * u).astype(BF16)
        y = jnp.dot(a, wdb_ref[...], preferred_element_type=F32)
        y_ref[...] = _pack_bf16_pair(y[:, :half], y[:, half:])

    @pl.when(jnp.logical_not(used))
    def _():
        y_ref[...] = jnp.zeros(y_ref.shape, U32)


def moe_experts(xb, blk_expert, n_used, wg, wu, wd, layer):
    n_slots, half = xb.shape
    d = 2 * half
    ff = wg.shape[3]
    nblk = n_slots // MOE_BLOCK
    return pl.pallas_call(
        _expert_body,
        grid_spec=pltpu.PrefetchScalarGridSpec(
            num_scalar_prefetch=2, grid=(nblk,),
            in_specs=[pl.BlockSpec((MOE_BLOCK, half), lambda i, be, nu: (i, 0)),
                      pl.BlockSpec((1, 1, d, ff), lambda i, be, nu: (layer, be[i], 0, 0)),
                      pl.BlockSpec((1, 1, d, ff), lambda i, be, nu: (layer, be[i], 0, 0)),
                      pl.BlockSpec((1, 1, ff, d), lambda i, be, nu: (layer, be[i], 0, 0))],
            out_specs=pl.BlockSpec((MOE_BLOCK, half), lambda i, be, nu: (i, 0)),
            scratch_shapes=[pltpu.VMEM((d, ff), BF16), pltpu.VMEM((d, ff), BF16), pltpu.VMEM((ff, d), BF16)]),
        out_shape=jax.ShapeDtypeStruct((n_slots, half), U32),
        compiler_params=_cparams(("arbitrary",)), name="moe_experts")(
            blk_expert, n_used, xb, wg, wu, wd)


def _combine_body(tt, alpha, n_tiles, dest_ref, dnext_ref, gate_ref, h_ref, g_ref, b_ref, y_hbm, ho_ref, hb_ref,
                  buf_ref, sem):
    i = pl.program_id(0)
    slot = i % 2
    nslot = 1 - slot

    def gather(idx_ref, j, to_slot):
        for k in range(2):
            pltpu.make_async_copy(y_hbm.at[pl.ds(idx_ref[k, j], 1), :],
                                  buf_ref.at[to_slot, k, pl.ds(j, 1), :], sem.at[to_slot]).start(priority=k)

    def wait_slot(s):
        for k in range(2):
            pltpu.make_async_copy(y_hbm.at[pl.ds(0, tt), :], buf_ref.at[s, k], sem.at[s]).wait()

    @pl.when(i == 0)
    def _():
        def first(j, carry):
            gather(dest_ref, j, 0)
            return carry
        lax.fori_loop(0, tt, first, 0, unroll=8)

    wait_slot(slot)

    def ahead(j, carry):
        gather(dnext_ref, j, nslot)
        return carry

    lax.fori_loop(0, tt, ahead, 0, unroll=8)

    def rows_body(c, carry):
        r0 = pl.multiple_of(c * COMBINE_ROWS, COMBINE_ROWS)
        rows = pl.ds(r0, COMBINE_ROWS)
        gate = gate_ref[rows, :]
        lo0, hi0 = _unpack_bf16_pair(buf_ref[slot, 0, rows, :])
        lo1, hi1 = _unpack_bf16_pair(buf_ref[slot, 1, rows, :])
        ffn = jnp.concatenate([gate[:, 0:1] * lo0 + gate[:, 1:2] * lo1,
                               gate[:, 0:1] * hi0 + gate[:, 1:2] * hi1], axis=-1)
        o = _layer_norm(alpha * h_ref[rows, :] + ffn, g_ref[...], b_ref[...])
        ho_ref[rows, :] = o
        hb_ref[rows, :] = o.astype(BF16)
        return carry

    lax.fori_loop(0, tt // COMBINE_ROWS, rows_body, 0, unroll=4)

    @pl.when(i == n_tiles - 1)
    def _():
        wait_slot(nslot)


def moe_combine(yb, dest, gate_t, h, ln_g, ln_b, alpha, n_rows):
    t, d = h.shape
    tt = next(c for c in (256, 128, 96, 64, 32) if t % c == 0 and n_rows % c == 0)
    n_tiles = n_rows // tt
    return pl.pallas_call(
        functools.partial(_combine_body, tt, alpha, n_tiles), grid=(n_tiles,),
        in_specs=[pl.BlockSpec((2, tt), lambda i: (0, i), memory_space=pltpu.SMEM),
                  pl.BlockSpec((2, tt), lambda i: (0, jnp.minimum(i + 1, n_tiles - 1)), memory_space=pltpu.SMEM),
                  pl.BlockSpec((tt, 2), lambda i: (i, 0)),
                  pl.BlockSpec((tt, d), lambda i: (i, 0)),
                  pl.BlockSpec((1, d), lambda i: (0, 0)),
                  pl.BlockSpec((1, d), lambda i: (0, 0)),
                  pl.BlockSpec(memory_space=pl.ANY)],
        out_specs=[pl.BlockSpec((tt, d), lambda i: (i, 0)), pl.BlockSpec((tt, d), lambda i: (i, 0))],
        out_shape=[jax.ShapeDtypeStruct((n_rows, d), F32), jax.ShapeDtypeStruct((n_rows, d), BF16)],
        scratch_shapes=[pltpu.VMEM((2, 2, tt, d // 2), U32), pltpu.SemaphoreType.DMA((2,))],
        compiler_params=_cparams(("arbitrary",)), name="moe_combine")(
            dest, dest, gate_t, h, ln_g.reshape(1, d), ln_b.reshape(1, d), yb)


def moe_layer(h, hb, router_w, router_b, w_gate, w_up, w_down, layer, ln_g, ln_b, alpha, n_rows_out):
    t, d = h.shape
    e_idx, gate, rank, counts = moe_route(h, hb, router_w, router_b)
    padded = (counts + MOE_BLOCK - 1) // MOE_BLOCK * MOE_BLOCK
    pends = jnp.cumsum(padded)
    pstart = pends - padded
    experts = jnp.arange(N_EXPERTS, dtype=I32)[:, None, None]
    dest = jnp.sum(jnp.where(e_idx[None] == experts, pstart[:, None, None], 0), axis=0) + rank
    nblk = -(-(2 * t) // MOE_BLOCK) + N_EXPERTS
    n_used = (pends[-1] // MOE_BLOCK).astype(I32)
    blk = jnp.arange(nblk, dtype=I32)
    blk_first_row = jnp.minimum(blk, n_used - 1) * MOE_BLOCK
    blk_expert = jnp.minimum(jnp.sum((pends[None, :] <= blk_first_row[:, None]).astype(I32), axis=1),
                             N_EXPERTS - 1)
    zero_flag = ((blk >= n_used) | (blk == pends[blk_expert] // MOE_BLOCK - 1)).astype(I32)
    xb = moe_dispatch(hb, dest, zero_flag, nblk * MOE_BLOCK)
    yb = moe_experts(xb, blk_expert, n_used.reshape(1), w_gate, w_up, w_down, layer)
    return moe_combine(yb, dest, gate.T, h, ln_g, ln_b, alpha, n_rows_out)


def kernel(x, meta_tokens, ln_mix_g, ln_mix_b, ln_ffn_g, ln_ffn_b, conv_pw1_w, conv_pw1_b, conv_dw_w, conv_dw_b, conv_ln_g, conv_ln_b, conv_pw2_w, conv_pw2_b, mla_wdq, mla_q_norm_g, mla_wuq, mla_wdkv, mla_kv_norm_g, mla_wukv, mla_wo, hgrn_w_in, hgrn_lb_logits, hgrn_norm_g, hgrn_wo, router_w, router_b, moe_w_gate, moe_w_up, moe_w_down):
    nb, s_len, d = x.shape
    nm = meta_tokens.shape[0]
    depth = ln_mix_g.shape[0]
    alpha = float((2 * depth) ** 0.25)
    zero_bias = jnp.zeros((d,), F32)

    meta = jnp.broadcast_to(meta_tokens[None].astype(x.dtype), (nb, nm, d)).reshape(nb * nm, d)
    h = x.reshape(nb * s_len, d)
    tail = meta
    hb = jnp.concatenate([h.astype(BF16), meta.astype(BF16)], axis=0)
    p_lb = jax.nn.softmax(hgrn_lb_logits.astype(F32), axis=0)
    lower_bounds = jnp.cumsum(p_lb, axis=0) - p_lb[0]

    for i in range(depth):
        j = i // N_MIXERS
        kind = i % N_MIXERS
        if kind == 0:
            u = mm_glu(hb, conv_pw1_w[j].astype(BF16), conv_pw1_b[j])
            mix_in = conv_ln_swish(u, conv_dw_w[j], conv_dw_b[j], conv_ln_g[j], conv_ln_b[j], nb, s_len, nm)
            w_out, b_out = conv_pw2_w[j], conv_pw2_b[j]
        elif kind == 1:
            mix_in = mla_mixer(hb, mla_wdq[j], mla_q_norm_g[j], mla_wuq[j], mla_wdkv[j], mla_kv_norm_g[j],
                               mla_wukv[j], nb, s_len, nm)
            w_out, b_out = mla_wo[j], zero_bias
        else:
            qfig = mm_plain(hb, hgrn_w_in[j].astype(BF16))
            mix_in = hgrn_mixer(qfig, lower_bounds[i], hgrn_norm_g[j], nb, s_len, nm)
            w_out, b_out = hgrn_wo[j], zero_bias
        h, hb = mm_res_ln(mix_in, h, w_out.astype(BF16), b_out, ln_mix_g[i], ln_mix_b[i], alpha, tail)
        tail = None
        n_rows_out = nb * s_len if i == depth - 1 else h.shape[0]
        h, hb = moe_layer(h, hb, router_w, router_b, moe_w_gate, moe_w_up, moe_w_down, i,
                          ln_ffn_g[i], ln_ffn_b[i], alpha, n_rows_out)
    return h.reshape(nb, s_len, d)
```
